```python
import math
import functools
import jax, jax.numpy as jnp
from jax import lax
import numpy as np

D_MODEL = 1024
BATCH = 8
SEQ = 2048
DEPTH = 4
DEC_BATCH = 128
DEC_SEQ = 1
PAST_LEN = 2048
PAGE_SIZE = 128

F32 = jnp.float32
EPS = 1e-6
N_EVEN = (DEPTH + 1) // 2
N_ODD = DEPTH // 2
MIX_WIDTH = D_MODEL
WIDTH_A = MIX_WIDTH // 2
HEAD_DIM_A = 128
H_A = WIDTH_A // HEAD_DIM_A
DQK_A = HEAD_DIM_A // 2
Q_BLOCK = 128
WIDTH_B = MIX_WIDTH - WIDTH_A
HEAD_DIM_B = 128
H_B = WIDTH_B // HEAD_DIM_B
K_CONV = 4
CHUNK = 64
WIDTH_C = MIX_WIDTH // 2
GROUP_C = 16
G_C = WIDTH_C // GROUP_C
P_C = 64
WIDTH_D = MIX_WIDTH - WIDTH_C
HEAD_DIM_D = 128
H_D = WIDTH_D // HEAD_DIM_D
E_IN = 4 * WIDTH_A + 4 * WIDTH_B + 2 * H_B
O_IN = 2 * WIDTH_C + 4 * WIDTH_D

kernel_name = 'hybrid_diffattn_gdn_s5_retnet_step'


def rms_norm(x, g):
    xf = x.astype(F32)
    return xf * lax.rsqrt(jnp.mean(xf * xf, axis=-1, keepdims=True) + EPS) * g.astype(F32)


def layer_norm(x, g):
    xf = x.astype(F32)
    xc = xf - jnp.mean(xf, axis=-1, keepdims=True)
    return xc * lax.rsqrt(jnp.mean(xc * xc, axis=-1, keepdims=True) + EPS) * g.astype(F32)


def l2_normalize(x):
    return x * lax.rsqrt(jnp.sum(x * x, axis=-1, keepdims=True) + EPS)


def alibi_slopes(n_heads):
    return jnp.exp2(-8.0 * jnp.arange(1, n_heads + 1, dtype=F32) / n_heads)


def modulation(c, w_ada, b_ada):
    mod = jax.nn.silu(c) @ w_ada + b_ada
    return jnp.split(mod, 3, axis=-1)


def modulate(x, g, shift, scale):
    return (rms_norm(x, g) * (1.0 + scale[:, None, :]) + shift[:, None, :]).astype(x.dtype)


def diff_attend(q, q_pos, segments, lam):
    slopes = alibi_slopes(H_A)
    scale = DQK_A ** -0.5

    def probs(qh, half):
        parts = []
        for k, _, k_pos in segments:
            kh = k[..., half * DQK_A:(half + 1) * DQK_A].astype(F32)
            rel = q_pos[:, None] - k_pos[None, :]
            s = jnp.einsum('bqhd,bkhd->bhqk', qh, kh) * scale - slopes[:, None, None] * rel.astype(F32)
            parts.append(jnp.where(rel >= 0, s, -jnp.inf))
        return jax.nn.softmax(jnp.concatenate(parts, axis=-1), axis=-1)

    p = probs(q[..., :DQK_A], 0) - lam * probs(q[..., DQK_A:], 1)
    out = 0.0
    start = 0
    for _, v, k_pos in segments:
        n = k_pos.shape[0]
        out = out + jnp.einsum('bhqk,bkhd->bqhd', p[..., start:start + n], v.astype(F32))
        start += n
    return out


def diff_attn_prompt(q, k, v, lam):
    b, l, h, dq = q.shape
    nb = l // Q_BLOCK
    pos = jnp.arange(l, dtype=jnp.int32)
    q_blocks = q.reshape(b, nb, Q_BLOCK, h, dq).transpose(1, 0, 2, 3, 4)
    pos_blocks = pos.reshape(nb, Q_BLOCK)
    out = lax.map(lambda qp: diff_attend(qp[0], qp[1], [(k, v, pos)], lam), (q_blocks, pos_blocks))
    return out.transpose(1, 0, 2, 3, 4).reshape(b, l, h, v.shape[-1])


def diff_attn_sample(q, k, v, lam, past_k, past_v):
    past_len, l = past_k.shape[1], q.shape[1]
    past_pos = jnp.arange(past_len, dtype=jnp.int32)
    new_pos = past_len + jnp.arange(l, dtype=jnp.int32)
    return diff_attend(q, new_pos, [(past_k, past_v, past_pos), (k, v, new_pos)], lam)


def pad_time(x, n_pad):
    return jnp.pad(x, [(0, 0), (0, n_pad)] + [(0, 0)] * (x.ndim - 2))


def to_chunks(x, c):
    b, lp, h = x.shape[:3]
    rest = x.shape[3:]
    x = x.reshape((b, lp // c, c, h) + rest)
    return x.transpose((1, 0, 3, 2) + tuple(range(4, x.ndim)))


def from_chunks(o, length):
    n, b, h, c, dv = o.shape
    return o.transpose(1, 0, 3, 2, 4).reshape(b, n * c, h, dv)[:, :length]


def split_chunks(arrays, length):
    c = min(CHUNK, length)
    n_pad = -length % c
    return c, tuple(to_chunks(pad_time(t, n_pad), c) for t in arrays)


def gated_delta_chunked(q, k, v, g, beta, s0):
    length = q.shape[1]
    dv = v.shape[-1]
    c, xs = split_chunks((q, k, v, g, beta), length)
    incl = jnp.tril(jnp.ones((c, c), bool))
    strict = jnp.tril(jnp.ones((c, c), bool), -1)
    eye = jnp.eye(c, dtype=F32)

    def step(s, inp):
        qc, kc, vc, gc, bc = inp
        gam = jnp.cumsum(gc, axis=-1)
        decay = jnp.exp(jnp.where(incl, gam[..., :, None] - gam[..., None, :], -jnp.inf))
        kk = jnp.einsum('bhid,bhjd->bhij', kc, kc)
        m = eye + jnp.where(strict, bc[..., :, None] * decay * kk, 0.0)
        rhs = jnp.concatenate([bc[..., None] * vc, (bc * jnp.exp(gam))[..., None] * kc], axis=-1)
        sol = lax.linalg.triangular_solve(m, rhs, left_side=True, lower=True)
        u = sol[..., :dv] - jnp.einsum('bhik,bhkv->bhiv', sol[..., dv:], s)
        qk = jnp.einsum('bhik,bhjk->bhij', qc, kc) * decay
        o = jnp.exp(gam)[..., None] * jnp.einsum('bhik,bhkv->bhiv', qc, s) + jnp.einsum('bhij,bhjv->bhiv', qk, u)
        gl = gam[..., -1:]
        s_new = jnp.exp(gl)[..., None] * s + jnp.einsum('bhjk,bhjv->bhkv', kc * jnp.exp(gl - gam)[..., None], u)
        return s_new, o

    s_fin, o = lax.scan(step, s0, xs)
    return from_chunks(o, length), s_fin


def causal_conv(x, buf, w):
    length = x.shape[1]
    xp = jnp.concatenate([buf, x], axis=1)
    y = sum(xp[:, j:j + length] * w[j] for j in range(K_CONV))
    return y, xp[:, length:]


def decay_attn_chunked(q, k, v, logd, s0):
    length = q.shape[1]
    c, xs = split_chunks((q, k, v, logd), length)
    incl = jnp.tril(jnp.ones((c, c), bool))

    def step(s, inp):
        qc, kc, vc, lc = inp
        gam = jnp.cumsum(lc, axis=-1)
        decay = jnp.exp(jnp.where(incl, gam[..., :, None] - gam[..., None, :], -jnp.inf))
        qk = jnp.einsum('bhik,bhjk->bhij', qc, kc) * decay
        o = jnp.exp(gam)[..., None] * jnp.einsum('bhik,bhkv->bhiv', qc, s) + jnp.einsum('bhij,bhjv->bhiv', qk, vc)
        gl = gam[..., -1:]
        s_new = jnp.exp(gl)[..., None] * s + jnp.einsum('bhjk,bhjv->bhkv', kc * jnp.exp(gl - gam)[..., None], vc)
        return s_new, o

    s_fin, o = lax.scan(step, s0, xs)
    return from_chunks(o, length), s_fin


def complex_affine_combine(e1, e2):
    a1r, a1i, b1r, b1i = e1
    a2r, a2i, b2r, b2i = e2
    return (a2r * a1r - a2i * a1i, a2r * a1i + a2i * a1r,
            a2r * b1r - a2i * b1i + b2r, a2r * b1i + a2i * b1r + b2i)


def s5_scan(u, x0_re, x0_im, a_re, a_im, b_re, b_im, c_re, c_im, d_skip, log_dt):
    dt = jnp.exp(log_dt.astype(F32))[:, None]
    a_re, a_im = a_re.astype(F32), a_im.astype(F32)
    mag = jnp.exp(a_re * dt)
    ang = a_im * dt
    lb_re, lb_im = mag * jnp.cos(ang), mag * jnp.sin(ang)
    den = a_re * a_re + a_im * a_im
    f_re = ((lb_re - 1.0) * a_re + lb_im * a_im) / den
    f_im = (lb_im * a_re - (lb_re - 1.0) * a_im) / den
    bb_re = f_re[..., None] * b_re - f_im[..., None] * b_im
    bb_im = f_re[..., None] * b_im + f_im[..., None] * b_re
    bu_re = jnp.einsum('blgc,gpc->blgp', u, bb_re)
    bu_im = jnp.einsum('blgc,gpc->blgp', u, bb_im)
    bu_re = bu_re.at[:, 0].add(lb_re * x0_re - lb_im * x0_im)
    bu_im = bu_im.at[:, 0].add(lb_re * x0_im + lb_im * x0_re)
    shape = bu_re.shape
    elems = (jnp.broadcast_to(lb_re, shape), jnp.broadcast_to(lb_im, shape), bu_re, bu_im)
    _, _, xr, xi = lax.associative_scan(complex_affine_combine, elems, axis=1)
    y = (jnp.einsum('blgp,gcp->blgc', xr, c_re) - jnp.einsum('blgp,gcp->blgc', xi, c_im)
         + d_skip.reshape(G_C, GROUP_C) * u)
    return y, xr[:, -1], xi[:, -1]


def even_mixer(h, lam_init, attn_fn, conv_buf, s0, w_in, w_out, qn_g, kn_g, lam_q1, lam_k1, lam_q2, lam_k2,
               subln_g, conv_w, a_log, dt_bias, gn_b):
    b, l, _ = h.shape
    proj = (h @ w_in).astype(F32)
    sizes = [WIDTH_A, WIDTH_A, WIDTH_A, WIDTH_A, 3 * WIDTH_B, WIDTH_B, H_B, H_B]
    qa, ka, va, za, qkv_b, zb, a_b, b_b = jnp.split(proj, np.cumsum(sizes)[:-1].tolist(), axis=-1)
    qa = rms_norm(qa.reshape(b, l, H_A, 2, DQK_A), qn_g).reshape(b, l, H_A, 2 * DQK_A)
    ka = rms_norm(ka.reshape(b, l, H_A, 2, DQK_A), kn_g).reshape(b, l, H_A, 2 * DQK_A)
    va = va.reshape(b, l, H_A, HEAD_DIM_A)
    lam = (jnp.exp(jnp.sum(lam_q1.astype(F32) * lam_k1.astype(F32)))
           - jnp.exp(jnp.sum(lam_q2.astype(F32) * lam_k2.astype(F32))) + lam_init)
    oa = attn_fn(qa, ka, va, lam)
    oa = (rms_norm(oa, subln_g) * (1.0 - lam_init)).reshape(b, l, WIDTH_A) * jax.nn.silu(za)
    conv_out, conv_new = causal_conv(qkv_b, conv_buf.astype(F32), conv_w.astype(F32))
    qb, kb, vb = jnp.split(jax.nn.silu(conv_out), 3, axis=-1)
    qb = l2_normalize(qb.reshape(b, l, H_B, HEAD_DIM_B)) * (HEAD_DIM_B ** -0.5)
    kb = l2_normalize(kb.reshape(b, l, H_B, HEAD_DIM_B))
    vb = vb.reshape(b, l, H_B, HEAD_DIM_B)
    beta = jax.nn.sigmoid(b_b)
    g = -jnp.exp(a_log.astype(F32)) * jax.nn.softplus(a_b + dt_bias)
    ob, s_new = gated_delta_chunked(qb, kb, vb, g, beta, s0.astype(F32))
    ob = rms_norm(ob, gn_b).reshape(b, l, WIDTH_B) * jax.nn.silu(zb)
    out = jnp.concatenate([oa, ob], axis=-1).astype(h.dtype) @ w_out
    return out, ka, va, conv_new, s_new


def odd_mixer(h, x0_re, x0_im, r0, w_in, w_out, a_re, a_im, b_re, b_im, c_re, c_im, d_skip, log_dt, w_glu, gn_d):
    b, l, _ = h.shape
    proj = (h @ w_in).astype(F32)
    sizes = [WIDTH_C, WIDTH_C, WIDTH_D, WIDTH_D, WIDTH_D, WIDTH_D]
    u, zc, qd, kd, vd, zd = jnp.split(proj, np.cumsum(sizes)[:-1].tolist(), axis=-1)
    y, xr, xi = s5_scan(u.reshape(b, l, G_C, GROUP_C), x0_re.astype(F32), x0_im.astype(F32), a_re, a_im,
                        b_re, b_im, c_re, c_im, d_skip, log_dt)
    yg = jax.nn.gelu(y.reshape(b, l, WIDTH_C))
    oc = yg * jax.nn.sigmoid(yg @ w_glu.astype(F32)) * jax.nn.silu(zc)
    qd = qd.reshape(b, l, H_D, HEAD_DIM_D)
    kd = kd.reshape(b, l, H_D, HEAD_DIM_D) * (HEAD_DIM_D ** -0.5)
    vd = vd.reshape(b, l, H_D, HEAD_DIM_D)
    log_gamma = jnp.log1p(-jnp.exp2(-5.0 - jnp.arange(H_D, dtype=F32)))
    od, r_new = decay_attn_chunked(qd, kd, vd, jnp.broadcast_to(log_gamma, (b, l, H_D)), r0.astype(F32))
    od = layer_norm(od, gn_d).reshape(b, l, WIDTH_D) * jax.nn.silu(zd)
    out = jnp.concatenate([oc, od], axis=-1).astype(h.dtype) @ w_out
    return out, xr, xi, r_new


def setup_inputs(seed: int = 0) -> dict:
    key = jax.random.key(seed)
    ks = iter(jax.random.split(key, 48))

    def nrm(shape, scale):
        return jax.random.normal(next(ks), shape, F32) * scale

    n_pages = PAST_LEN // PAGE_SIZE
    n_phys = (5 * DEC_BATCH * n_pages + 3) // 4
    page_table = jax.random.permutation(next(ks), n_phys)[:DEC_BATCH * n_pages].reshape(DEC_BATCH, n_pages).astype(jnp.int32)
    dt_b = jnp.exp(jax.random.uniform(next(ks), (N_EVEN, H_B), F32, math.log(1e-3), math.log(1e-1)))
    return {
        'x_prompt': nrm((BATCH, SEQ, D_MODEL), 1.0),
        'x_sample': nrm((DEC_BATCH, DEC_SEQ, D_MODEL), 1.0),
        'c_prompt': nrm((BATCH, D_MODEL), 1.0),
        'c_sample': nrm((DEC_BATCH, D_MODEL), 1.0),
        'page_table': page_table,
        'cache_k': nrm((N_EVEN, n_phys, PAGE_SIZE, H_A, 2 * DQK_A), 1.0),
        'cache_v': nrm((N_EVEN, n_phys, PAGE_SIZE, H_A, HEAD_DIM_A), 1.0),
        'state_b_conv': nrm((N_EVEN, DEC_BATCH, K_CONV - 1, 3 * WIDTH_B), 1.0),
        'state_b_ssm': nrm((N_EVEN, DEC_BATCH, H_B, HEAD_DIM_B, HEAD_DIM_B), 0.1),
        'state_c_re': nrm((N_ODD, DEC_BATCH, G_C, P_C), 0.5),
        'state_c_im': nrm((N_ODD, DEC_BATCH, G_C, P_C), 0.5),
        'state_d_ret': nrm((N_ODD, DEC_BATCH, H_D, HEAD_DIM_D, HEAD_DIM_D), 0.5),
        'norm_g': 1.0 + nrm((DEPTH, D_MODEL), 0.02),
        'w_ada': nrm((DEPTH, D_MODEL, 3 * D_MODEL), 0.5 * D_MODEL ** -0.5),
        'b_ada': nrm((DEPTH, 3 * D_MODEL), 0.02),
        'w_in_e': nrm((N_EVEN, D_MODEL, E_IN), D_MODEL ** -0.5),
        'w_out_e': nrm((N_EVEN, WIDTH_A + WIDTH_B, D_MODEL), (WIDTH_A + WIDTH_B) ** -0.5),
        'qn_g': 1.0 + nrm((N_EVEN, DQK_A), 0.02),
        'kn_g': 1.0 + nrm((N_EVEN, DQK_A), 0.02),
        'lam_q1': nrm((N_EVEN, DQK_A), 0.1),
        'lam_k1': nrm((N_EVEN, DQK_A), 0.1),
        'lam_q2': nrm((N_EVEN, DQK_A), 0.1),
        'lam_k2': nrm((N_EVEN, DQK_A), 0.1),
        'subln_g': 1.0 + nrm((N_EVEN, HEAD_DIM_A), 0.02),
        'conv_w': nrm((N_EVEN, K_CONV, 3 * WIDTH_B), K_CONV ** -0.5),
        'a_log': jnp.log(jax.random.uniform(next(ks), (N_EVEN, H_B), F32, 1.0, 16.0)),
        'dt_bias': dt_b + jnp.log(-jnp.expm1(-dt_b)),
        'gn_b': 1.0 + nrm((N_EVEN, HEAD_DIM_B), 0.02),
        'w_in_o': nrm((N_ODD, D_MODEL, O_IN), D_MODEL ** -0.5),
        'w_out_o': nrm((N_ODD, WIDTH_C + WIDTH_D, D_MODEL), (WIDTH_C + WIDTH_D) ** -0.5),
        's5_a_re': -0.5 + nrm((N_ODD, G_C, P_C), 0.01),
        's5_a_im': math.pi * jnp.arange(P_C, dtype=F32) + nrm((N_ODD, G_C, P_C), 0.01),
        's5_b_re': nrm((N_ODD, G_C, P_C, GROUP_C), (2 * GROUP_C) ** -0.5),
        's5_b_im': nrm((N_ODD, G_C, P_C, GROUP_C), (2 * GROUP_C) ** -0.5),
        's5_c_re': nrm((N_ODD, G_C, GROUP_C, P_C), P_C ** -0.5),
        's5_c_im': nrm((N_ODD, G_C, GROUP_C, P_C), P_C ** -0.5),
        's5_d': nrm((N_ODD, WIDTH_C), 0.5),
        's5_log_dt': jax.random.uniform(next(ks), (N_ODD, G_C), F32, math.log(1e-3), math.log(1e-1)),
        'w_glu': nrm((N_ODD, WIDTH_C, WIDTH_C), WIDTH_C ** -0.5),
        'gn_d': 1.0 + nrm((N_ODD, HEAD_DIM_D), 0.02),
    }


def reference(x_prompt, x_sample, c_prompt, c_sample, page_table, cache_k, cache_v, state_b_conv, state_b_ssm,
              state_c_re, state_c_im, state_d_ret, norm_g, w_ada, b_ada, w_in_e, w_out_e, qn_g, kn_g,
              lam_q1, lam_k1, lam_q2, lam_k2, subln_g, conv_w, a_log, dt_bias, gn_b, w_in_o, w_out_o,
              s5_a_re, s5_a_im, s5_b_re, s5_b_im, s5_c_re, s5_c_im, s5_d, s5_log_dt, w_glu, gn_d):
    b_p, b_s = x_prompt.shape[0], x_sample.shape[0]
    n_pages, page_size = page_table.shape[1], cache_k.shape[2]
    past_len = n_pages * page_size
    xp, xs = x_prompt, x_sample
    nk_p, nv_p, nk_s, nv_s, cv_p, cv_s, dl_p, dl_s = ([] for _ in range(8))
    s5r_p, s5i_p, s5r_s, s5i_s, rt_p, rt_s = ([] for _ in range(6))
    for li in range(DEPTH):
        sh_p, sc_p, gt_p = modulation(c_prompt, w_ada[li], b_ada[li])
        sh_s, sc_s, gt_s = modulation(c_sample, w_ada[li], b_ada[li])
        hp = modulate(xp, norm_g[li], sh_p, sc_p)
        hs = modulate(xs, norm_g[li], sh_s, sc_s)
        if li % 2 == 0:
            e = li // 2
            lam_init = 0.8 - 0.6 * math.exp(-0.3 * li)
            ew = (w_in_e[e], w_out_e[e], qn_g[e], kn_g[e], lam_q1[e], lam_k1[e], lam_q2[e], lam_k2[e],
                  subln_g[e], conv_w[e], a_log[e], dt_bias[e], gn_b[e])
            out_p, k_p, v_p, c_p, s_p = even_mixer(
                hp, lam_init, diff_attn_prompt, jnp.zeros((b_p, K_CONV - 1, 3 * WIDTH_B), F32),
                jnp.zeros((b_p, H_B, HEAD_DIM_B, HEAD_DIM_B), F32), *ew)
            past_k = cache_k[e][page_table].reshape(b_s, past_len, H_A, 2 * DQK_A)
            past_v = cache_v[e][page_table].reshape(b_s, past_len, H_A, HEAD_DIM_A)
            attn_s = functools.partial(diff_attn_sample, past_k=past_k, past_v=past_v)
            out_s, k_s, v_s, c_s, s_s = even_mixer(hs, lam_init, attn_s, state_b_conv[e], state_b_ssm[e], *ew)
            nk_p.append(k_p); nv_p.append(v_p); nk_s.append(k_s); nv_s.append(v_s)
            cv_p.append(c_p); cv_s.append(c_s); dl_p.append(s_p); dl_s.append(s_s)
        else:
            o = li // 2
            ow = (w_in_o[o], w_out_o[o], s5_a_re[o], s5_a_im[o], s5_b_re[o], s5_b_im[o], s5_c_re[o], s5_c_im[o],
                  s5_d[o], s5_log_dt[o], w_glu[o], gn_d[o])
            zeros_c = jnp.zeros((b_p, G_C, P_C), F32)
            out_p, r_p, i_p, t_p = odd_mixer(hp, zeros_c, zeros_c, jnp.zeros((b_p, H_D, HEAD_DIM_D, HEAD_DIM_D), F32), *ow)
            out_s, r_s, i_s, t_s = odd_mixer(hs, state_c_re[o], state_c_im[o], state_d_ret[o], *ow)
            s5r_p.append(r_p); s5i_p.append(i_p); s5r_s.append(r_s); s5i_s.append(i_s)
            rt_p.append(t_p); rt_s.append(t_s)
        xp = xp + (gt_p[:, None, :] * out_p).astype(xp.dtype)
        xs = xs + (gt_s[:, None, :] * out_s).astype(xs.dtype)
    return (xp, xs, jnp.stack(nk_p), jnp.stack(nv_p), jnp.stack(nk_s), jnp.stack(nv_s),
            jnp.stack(cv_p), jnp.stack(cv_s), jnp.stack(dl_p), jnp.stack(dl_s),
            jnp.stack(s5r_p), jnp.stack(s5i_p), jnp.stack(s5r_s), jnp.stack(s5i_s),
            jnp.stack(rt_p), jnp.stack(rt_s))
```

```python
import functools
import math

import jax
import jax.numpy as jnp
from jax import lax
from jax.experimental import pallas as pl
from jax.experimental.pallas import tpu as pltpu

F32 = jnp.float32
BF16 = jnp.bfloat16
EPS = 1e-6

LANES = 128
SUBLANES = 8
VMEM_LIMIT_BYTES = 48 * 1024 * 1024

HEAD = 128
DQK = HEAD // 2
N_HEADS = 4
WIDTH = N_HEADS * HEAD
K_CONV = 4
DELTA_CHUNK = 64
GROUP_C = 16
P_C = 64
S5_HALF_GROUPS = 16

ROW_TILE = 256
ATTN_TILE = 512
DELTA_ROWS = 256
RET_ROWS = 256
S5_STEPS = 64
S5_LANE_GROUP = 512


def _params(n_axes, vmem=VMEM_LIMIT_BYTES):
    return pltpu.CompilerParams(dimension_semantics=("arbitrary",) * n_axes, vmem_limit_bytes=vmem)


def _sigmoid(x):
    return 1.0 / (1.0 + jnp.exp(-x))


def _silu(x):
    return x * _sigmoid(x)


def _softplus(x):
    return jnp.maximum(x, 0.0) + jnp.log1p(jnp.exp(-jnp.abs(x)))


def _gelu_tanh(x):
    return 0.5 * x * (1.0 + jnp.tanh(math.sqrt(2.0 / math.pi) * (x + 0.044715 * (x * x * x))))


def _mm(a, b):
    return jnp.dot(a.astype(BF16), b.astype(BF16), preferred_element_type=F32)


def _mm_nt(a, b):
    return lax.dot_general(a.astype(BF16), b.astype(BF16), (((1,), (1,)), ((), ())),
                           preferred_element_type=F32)


def _mod_kernel(c_ref, w_ref, b_ref, o_ref):
    o_ref[...] = _mm(_silu(c_ref[...]), w_ref[...]) + b_ref[...]


def _modulation(c_all, w_ada, b_ada):
    depth, d, n3 = w_ada.shape
    rows = c_all.shape[0]
    tn = d
    return pl.pallas_call(
        _mod_kernel,
        grid=(depth, n3 // tn),
        in_specs=[pl.BlockSpec((rows, d), lambda l, n: (0, 0)),
                  pl.BlockSpec((None, d, tn), lambda l, n: (l, 0, n)),
                  pl.BlockSpec((None, 1, tn), lambda l, n: (l, 0, n))],
        out_specs=pl.BlockSpec((None, rows, tn), lambda l, n: (l, 0, n)),
        out_shape=jax.ShapeDtypeStruct((depth, rows, n3), F32),
        compiler_params=_params(2),
        name="modulation",
    )(c_all, w_ada, b_ada.reshape(depth, 1, n3))


def _in_kernel(segs, x_ref, shift_ref, scale_ref, g_ref, w_ref, bd_ref, qg_ref, kg_ref, *refs):
    out_refs, h_ref = refs[:-1], refs[-1]
    x = x_ref[...]
    h = x * lax.rsqrt(jnp.mean(x * x, axis=-1, keepdims=True) + EPS) * g_ref[...]
    h_ref[...] = (h * (1.0 + scale_ref[...]) + shift_ref[...]).astype(BF16)
    for (c0, width, kind), o_ref in zip(segs, out_refs):
        acc = jnp.dot(h_ref[...], w_ref[:, c0:c0 + width], preferred_element_type=F32)
        if kind == "silu":
            acc = _silu(acc)
        elif kind in ("qnorm", "knorm"):
            sq = acc * acc
            hi = sq.astype(BF16)
            lo = (sq - hi.astype(F32)).astype(BF16)
            ms = (jnp.dot(hi, bd_ref[...], preferred_element_type=F32)
                  + jnp.dot(lo, bd_ref[...], preferred_element_type=F32))
            acc = acc * lax.rsqrt(ms + EPS) * (qg_ref if kind == "qnorm" else kg_ref)[...]
            if kind == "qnorm":
                acc = acc * DQK ** -0.5
        o_ref[...] = acc.astype(o_ref.dtype)


def _in_proj(x3, shift, scale, g, w, segs, bd, qg, kg):
    n_g, rows, d = x3.shape
    tm = min(ROW_TILE, rows)
    nt = rows // tm
    r_mod = shift.shape[1]
    assert r_mod in (1, tm)
    out_shapes, out_specs = [], []
    for _, width, _, dtype, by_time in segs:
        if by_time:
            out_shapes.append(jax.ShapeDtypeStruct((rows, n_g * width), dtype))
            out_specs.append(pl.BlockSpec((tm, width), lambda gi, i: (i, gi)))
        else:
            out_shapes.append(jax.ShapeDtypeStruct((n_g * rows, width), dtype))
            out_specs.append(pl.BlockSpec((tm, width), lambda gi, i: (gi * nt + i, 0)))
    const = lambda gi, i: (0, 0)
    return pl.pallas_call(
        functools.partial(_in_kernel, tuple(s[:3] for s in segs)),
        grid=(n_g, nt),
        in_specs=[pl.BlockSpec((None, tm, d), lambda gi, i: (gi, i, 0)),
                  pl.BlockSpec((None, r_mod, d), lambda gi, i: (gi, 0, 0)),
                  pl.BlockSpec((None, r_mod, d), lambda gi, i: (gi, 0, 0)),
                  pl.BlockSpec((1, d), const),
                  pl.BlockSpec(w.shape, const),
                  pl.BlockSpec(bd.shape, const),
                  pl.BlockSpec(qg.shape, const),
                  pl.BlockSpec(kg.shape, const)],
        out_specs=out_specs,
        out_shape=out_shapes,
        scratch_shapes=[pltpu.VMEM((tm, d), BF16)],
        compiler_params=_params(2),
        name="in_proj",
    )(x3, shift, scale, g.reshape(1, d), w, bd, qg, kg)


def _lambda_value(lamp, lam_init):
    a = jnp.sum(lamp[0:1] * lamp[1:2], axis=-1, keepdims=True)
    b = jnp.sum(lamp[2:3] * lamp[3:4], axis=-1, keepdims=True)
    return jnp.exp(a) - jnp.exp(b) + lam_init


def _sub_ln(o, gain, lam_init):
    return o * lax.rsqrt(jnp.mean(o * o, axis=-1, keepdims=True) + EPS) * gain * (1.0 - lam_init)


def _attn_kernel(lam_init, q_ref, k_ref, v_ref, nsl_ref, lamp_ref, sg_ref, o_ref,
                 m_ref, l_ref, acc_ref, base_ref):
    qi, ki = pl.program_id(2), pl.program_id(3)
    tq, tk = q_ref.shape[0], k_ref.shape[0]

    @pl.when(ki == 0)
    def _init():
        m_ref[...] = jnp.full(m_ref.shape, -jnp.inf, F32)
        l_ref[...] = jnp.zeros(l_ref.shape, F32)
        acc_ref[...] = jnp.zeros(acc_ref.shape, F32)
        row = lax.broadcasted_iota(jnp.int32, (tq, tk), 0)
        col = lax.broadcasted_iota(jnp.int32, (tq, tk), 1)
        base_ref[...] = (row - col).astype(F32)

    def step(masked):
        q = q_ref[...]
        kb = k_ref[...].astype(BF16)
        vb = v_ref[...].astype(BF16)
        base = base_ref[...]
        rel = base + ((qi - ki) * tq).astype(F32)
        bias = nsl_ref[...] * rel
        lane = lax.broadcasted_iota(jnp.int32, (1, HEAD), 1)
        for half in range(2):
            keep = (lane < DQK) if half == 0 else (lane >= DQK)
            s = _mm_nt(jnp.where(keep, q, jnp.zeros_like(q)), kb) + bias
            if masked:
                s = jnp.where(base >= 0.0, s, -jnp.inf)
            m_prev = m_ref[half]
            m_new = jnp.maximum(m_prev, jnp.max(s, axis=-1, keepdims=True))
            alpha = jnp.exp(m_prev - m_new)
            p = jnp.exp(s - m_new)
            l_ref[half] = alpha * l_ref[half] + jnp.sum(p, axis=-1, keepdims=True)
            acc_ref[half] = alpha * acc_ref[half] + jnp.dot(p.astype(BF16), vb, preferred_element_type=F32)
            m_ref[half] = m_new

    @pl.when(ki < qi)
    def _off_diagonal():
        step(False)

    @pl.when(ki == qi)
    def _diagonal():
        step(True)
        lam = _lambda_value(lamp_ref[...], lam_init)
        o = acc_ref[0] / l_ref[0] - lam * (acc_ref[1] / l_ref[1])
        o_ref[...] = _sub_ln(o, sg_ref[...], lam_init)


def _alibi_neg_slopes(n_heads):
    return -jnp.exp2(-8.0 * jnp.arange(1, n_heads + 1, dtype=F32) / n_heads)


def _attn_prompt(q, k, v, lamp, sg, lam_init, n_b, seq):
    t = min(ATTN_TILE, seq)
    nq = seq // t
    nsl = jnp.broadcast_to(_alibi_neg_slopes(N_HEADS)[:, None, None], (N_HEADS, 1, t))
    kv_map = lambda b, h, qi, ki: (b * nq + jnp.minimum(ki, qi), h)
    return pl.pallas_call(
        functools.partial(_attn_kernel, lam_init),
        grid=(n_b, N_HEADS, nq, nq),
        in_specs=[pl.BlockSpec((t, HEAD), lambda b, h, qi, ki: (b * nq + qi, h)),
                  pl.BlockSpec((t, HEAD), kv_map),
                  pl.BlockSpec((t, HEAD), kv_map),
                  pl.BlockSpec((None, 1, t), lambda b, h, qi, ki: (h, 0, 0)),
                  pl.BlockSpec(lamp.shape, lambda b, h, qi, ki: (0, 0)),
                  pl.BlockSpec((1, HEAD), lambda b, h, qi, ki: (0, 0))],
        out_specs=pl.BlockSpec((t, HEAD), lambda b, h, qi, ki: (b * nq + qi, h)),
        out_shape=jax.ShapeDtypeStruct((n_b * seq, WIDTH), F32),
        scratch_shapes=[pltpu.VMEM((2, t, 1), F32), pltpu.VMEM((2, t, 1), F32),
                        pltpu.VMEM((2, t, HEAD), F32), pltpu.VMEM((t, t), F32)],
        compiler_params=_params(4),
        name="attn_prompt",
    )(q, k, v, nsl, lamp, sg)


def _decode_kernel(lam_init, n_pages, pt_ref, q_ref, kn_ref, vn_ref, nsl_ref, lamp_ref, sg_ref, *refs):
    del pt_ref
    k_refs, v_refs, o_ref = refs[:n_pages], refs[n_pages:2 * n_pages], refs[2 * n_pages]
    page = k_refs[0].shape[0]
    n_rows = 4 * N_HEADS
    row = lax.broadcasted_iota(jnp.int32, (n_rows, WIDTH), 0)
    lane = lax.broadcasted_iota(jnp.int32, (n_rows, WIDTH), 1)
    seg = lane // DQK
    qm = jnp.where((seg % 2) * N_HEADS + seg // 2 == row, q_ref[...].astype(F32), 0.0).astype(BF16)
    s = jnp.concatenate([_mm_nt(qm, k_refs[j][...]) for j in range(n_pages)], axis=1)
    past = n_pages * page
    pos = lax.broadcasted_iota(jnp.int32, (1, past), 1)
    s = s + nsl_ref[...] * (past - pos).astype(F32)
    s_self = jnp.sum(qm.astype(F32) * kn_ref[...].astype(BF16).astype(F32), axis=-1, keepdims=True)
    m = jnp.maximum(jnp.max(s, axis=-1, keepdims=True), s_self)
    p = jnp.exp(s - m)
    p_self = jnp.exp(s_self - m)
    denom = jnp.sum(p, axis=-1, keepdims=True) + p_self
    lam = _lambda_value(lamp_ref[...], lam_init)
    r1 = row[:, 0:1]
    coef = jnp.where(r1 < N_HEADS, 1.0, jnp.where(r1 < 2 * N_HEADS, -lam, 0.0)) / denom
    pw = p * coef
    acc = (p_self * coef) * vn_ref[...].astype(BF16).astype(F32)
    for j in range(n_pages):
        acc = acc + _mm(pw[:, j * page:(j + 1) * page], v_refs[j][...])
    own_head = (lane // HEAD == row % N_HEADS) & (row < 2 * N_HEADS)
    o = jnp.sum(jnp.where(own_head, acc, 0.0), axis=0, keepdims=True)
    for h in range(N_HEADS):
        cols = slice(h * HEAD, (h + 1) * HEAD)
        o_ref[:, cols] = _sub_ln(o[:, cols], sg_ref[...], lam_init)


def _attn_decode(q, k_new, v_new, cache_k, cache_v, page_table, lamp, sg, lam_init):
    n_s, n_pages = page_table.shape
    page = cache_k.shape[1]
    nsl = _alibi_neg_slopes(N_HEADS)
    nsl8 = jnp.concatenate([nsl, nsl, jnp.zeros((2 * N_HEADS,), F32)]).reshape(4 * N_HEADS, 1)
    row_spec = pl.BlockSpec((None, 1, WIDTH), lambda b, pt: (b, 0, 0))
    const = lambda b, pt: (0, 0)

    def page_spec(j):
        return pl.BlockSpec((None, page, WIDTH), lambda b, pt: (pt[b, j], 0, 0))

    grid_spec = pltpu.PrefetchScalarGridSpec(
        num_scalar_prefetch=1,
        grid=(n_s,),
        in_specs=[row_spec, row_spec, row_spec,
                  pl.BlockSpec(nsl8.shape, const), pl.BlockSpec(lamp.shape, const),
                  pl.BlockSpec((1, HEAD), const)]
                 + [page_spec(j) for j in range(n_pages)] * 2,
        out_specs=row_spec,
    )
    out = pl.pallas_call(
        functools.partial(_decode_kernel, lam_init, n_pages),
        grid_spec=grid_spec,
        out_shape=jax.ShapeDtypeStruct((n_s, 1, WIDTH), F32),
        compiler_params=_params(1),
        name="attn_decode",
    )(page_table, q.reshape(n_s, 1, WIDTH), k_new.reshape(n_s, 1, WIDTH), v_new.reshape(n_s, 1, WIDTH),
      nsl8, lamp, sg, *([cache_k] * n_pages), *([cache_v] * n_pages))
    return out.reshape(n_s, WIDTH)


def _unit_lower_inverse(a, n):
    row = lax.broadcasted_iota(jnp.int32, (n, n), 0)
    col = lax.broadcasted_iota(jnp.int32, (n, n), 1)
    inv = jnp.where(row == col, 1.0, 0.0) - a
    power, order = a, 2
    while order < n:
        power = _mm(power, power)
        inv = inv + _mm(inv, power)
        order *= 2
    return inv


def _mm_tn(a, b):
    return lax.dot_general(a.astype(BF16), b.astype(BF16), (((0,), (0,)), ((), ())),
                           preferred_element_type=F32)


def _delta_kernel(chunk, qkv_ref, ab_ref, cw_ref, hp_ref, gn_ref, cbuf_ref, s0_ref, tri_ref,
                  o_ref, cnew_ref, sout_ref, xs_ref, st_ref):
    r = pl.program_id(1)
    rows = qkv_ref.shape[0]
    n_chunks = rows // chunk
    tail = K_CONV - 1

    @pl.when(r == 0)
    def _init():
        st_ref[...] = s0_ref[...]
        xs_ref[0:SUBLANES, :] = jnp.zeros((SUBLANES, xs_ref.shape[1]), F32)
        xs_ref[SUBLANES - tail:SUBLANES, :] = cbuf_ref[...]

    xs_ref[SUBLANES:SUBLANES + rows, :] = qkv_ref[...]
    w = cw_ref[...]
    y = w[0:1] * xs_ref[SUBLANES - 3:SUBLANES - 3 + rows, :]
    for j in range(1, K_CONV):
        y = y + w[j:j + 1] * xs_ref[SUBLANES - 3 + j:SUBLANES - 3 + j + rows, :]
    new_tail = xs_ref[SUBLANES + rows - tail:SUBLANES + rows, :]
    cnew_ref[...] = new_tail
    xs_ref[SUBLANES - tail:SUBLANES, :] = new_tail
    act = _silu(y)

    ab = ab_ref[...]
    hp = hp_ref[...]
    g_all = -jnp.exp(hp[0:1]) * _softplus(ab + hp[1:2])
    beta_all = _sigmoid(ab)
    gam_all = jnp.dot(tri_ref[...], g_all, preferred_element_type=F32, precision=lax.Precision.HIGHEST)

    ri = lax.broadcasted_iota(jnp.int32, (chunk, chunk), 0)
    ci = lax.broadcasted_iota(jnp.int32, (chunk, chunk), 1)
    incl, strict, diag = ri >= ci, ri > ci, ri == ci
    gn = gn_ref[...]

    for h in range(N_HEADS):
        state = st_ref[h]
        for c in range(n_chunks):
            rs = slice(c * chunk, (c + 1) * chunk)
            qh = act[rs, h * HEAD:(h + 1) * HEAD]
            kh = act[rs, WIDTH + h * HEAD:WIDTH + (h + 1) * HEAD]
            vh = act[rs, 2 * WIDTH + h * HEAD:2 * WIDTH + (h + 1) * HEAD]
            qh = qh * lax.rsqrt(jnp.sum(qh * qh, axis=-1, keepdims=True) + EPS) * HEAD ** -0.5
            kh = kh * lax.rsqrt(jnp.sum(kh * kh, axis=-1, keepdims=True) + EPS)
            gc = gam_all[rs, h:h + 1]
            gr = jnp.sum(jnp.where(diag, gc, 0.0), axis=0, keepdims=True)
            bc = beta_all[rs, N_HEADS + h:N_HEADS + h + 1]
            decay = jnp.exp(jnp.where(incl, gc - gr, -jnp.inf))
            kk = _mm_nt(kh, kh)
            inv = _unit_lower_inverse(jnp.where(strict, bc * decay * kk, 0.0), chunk)
            eg = jnp.exp(gc)
            sol = _mm(inv, jnp.concatenate([bc * vh, (bc * eg) * kh], axis=1))
            qk = _mm_nt(qh, kh) * decay
            u = sol[:, :HEAD] - _mm(sol[:, HEAD:], state)
            o = eg * _mm(qh, state) + _mm(qk, u)
            gl = gc[chunk - 1:chunk, :]
            kd = kh * jnp.exp(gl - gc)
            state = jnp.exp(gl) * state + _mm_tn(kd, u)
            o_ref[rs, h * HEAD:(h + 1) * HEAD] = (
                o * lax.rsqrt(jnp.mean(o * o, axis=-1, keepdims=True) + EPS) * gn)
        st_ref[h] = state

    @pl.when(r == pl.num_programs(1) - 1)
    def _finish():
        sout_ref[...] = st_ref[...]


def _chunk_tri(rows, chunk):
    r = jnp.arange(rows)
    return ((r[:, None] >= r[None, :]) & (r[:, None] // chunk == r[None, :] // chunk)).astype(F32)


def _head_params(a_log, dt_bias):
    return jnp.zeros((SUBLANES, LANES), F32).at[0, :N_HEADS].set(a_log).at[1, :N_HEADS].set(dt_bias)


def _delta(qkv, ab, conv_w, a_log, dt_bias, gn, cbuf, s0, n_b, seq):
    chunk = min(DELTA_CHUNK, seq)
    rows = min(DELTA_ROWS, seq)
    assert seq % rows == 0 and rows % chunk == 0
    nr = seq // rows
    width3 = 3 * WIDTH
    hp = _head_params(a_log, dt_bias)
    tri = _chunk_tri(rows, chunk)
    const = lambda b, r: (0, 0)
    return pl.pallas_call(
        functools.partial(_delta_kernel, chunk),
        grid=(n_b, nr),
        in_specs=[pl.BlockSpec((rows, width3), lambda b, r: (b * nr + r, 0)),
                  pl.BlockSpec((rows, LANES), lambda b, r: (b * nr + r, 0)),
                  pl.BlockSpec((K_CONV, width3), const),
                  pl.BlockSpec((SUBLANES, LANES), const),
                  pl.BlockSpec((1, HEAD), const),
                  pl.BlockSpec((None, K_CONV - 1, width3), lambda b, r: (b, 0, 0)),
                  pl.BlockSpec((None, N_HEADS, HEAD, HEAD), lambda b, r: (b, 0, 0, 0)),
                  pl.BlockSpec((rows, rows), const)],
        out_specs=[pl.BlockSpec((rows, WIDTH), lambda b, r: (b * nr + r, 0)),
                   pl.BlockSpec((None, K_CONV - 1, width3), lambda b, r: (b, 0, 0)),
                   pl.BlockSpec((None, N_HEADS, HEAD, HEAD), lambda b, r: (b, 0, 0, 0))],
        out_shape=[jax.ShapeDtypeStruct((n_b * seq, WIDTH), F32),
                   jax.ShapeDtypeStruct((n_b, K_CONV - 1, width3), F32),
                   jax.ShapeDtypeStruct((n_b, N_HEADS, HEAD, HEAD), F32)],
        scratch_shapes=[pltpu.VMEM((rows + SUBLANES, width3), F32),
                        pltpu.VMEM((N_HEADS, HEAD, HEAD), F32)],
        compiler_params=_params(2),
        name="delta_rule",
    )(qkv, ab, conv_w, hp, gn.reshape(1, HEAD), cbuf, s0, tri)


def _as_column(row_vec, eye):
    return jnp.sum(jnp.where(eye, row_vec, 0.0), axis=1, keepdims=True)


def _eye(n):
    return lax.broadcasted_iota(jnp.int32, (n, n), 0) == lax.broadcasted_iota(jnp.int32, (n, n), 1)


def _delta_step_kernel(x_ref, ab_ref, cw_ref, hp_ref, gn_ref, cbuf_ref, s0_ref, o_ref, cnew_ref, sout_ref):
    x = x_ref[...]
    buf = cbuf_ref[...]
    w = cw_ref[...]
    y = w[0:1] * buf[0:1] + w[1:2] * buf[1:2] + w[2:3] * buf[2:3] + w[3:4] * x
    cnew_ref[0:2, :] = buf[1:3]
    cnew_ref[2:3, :] = x
    act = _silu(y)
    ab = ab_ref[...]
    hp = hp_ref[...]
    decay_all = jnp.exp(-jnp.exp(hp[0:1]) * _softplus(ab + hp[1:2]))
    beta_all = _sigmoid(ab)
    eye = _eye(HEAD)
    for h in range(N_HEADS):
        qh = act[:, h * HEAD:(h + 1) * HEAD]
        kh = act[:, WIDTH + h * HEAD:WIDTH + (h + 1) * HEAD]
        vh = act[:, 2 * WIDTH + h * HEAD:2 * WIDTH + (h + 1) * HEAD]
        qh = qh * lax.rsqrt(jnp.sum(qh * qh, axis=-1, keepdims=True) + EPS) * HEAD ** -0.5
        kh = kh * lax.rsqrt(jnp.sum(kh * kh, axis=-1, keepdims=True) + EPS)
        a = decay_all[:, h:h + 1]
        beta = beta_all[:, N_HEADS + h:N_HEADS + h + 1]
        state = s0_ref[h]
        k_col = _as_column(kh, eye)
        u = beta * (vh - a * jnp.sum(state * k_col, axis=0, keepdims=True))
        state = a * state + k_col * u
        sout_ref[h] = state
        o = jnp.sum(state * _as_column(qh, eye), axis=0, keepdims=True)
        o_ref[:, h * HEAD:(h + 1) * HEAD] = o * lax.rsqrt(jnp.mean(o * o, axis=-1, keepdims=True) + EPS) * gn_ref[...]


def _delta_step(qkv, ab, conv_w, a_log, dt_bias, gn, cbuf, s0):
    n_s = qkv.shape[0]
    width3 = 3 * WIDTH
    const = lambda b: (0, 0)
    row = lambda width: pl.BlockSpec((None, 1, width), lambda b: (b, 0, 0))
    tail = pl.BlockSpec((None, K_CONV - 1, width3), lambda b: (b, 0, 0))
    st = pl.BlockSpec((None, N_HEADS, HEAD, HEAD), lambda b: (b, 0, 0, 0))
    o, cnew, s_new = pl.pallas_call(
        _delta_step_kernel,
        grid=(n_s,),
        in_specs=[row(width3), row(LANES), pl.BlockSpec((K_CONV, width3), const),
                  pl.BlockSpec((SUBLANES, LANES), const), pl.BlockSpec((1, HEAD), const), tail, st],
        out_specs=[row(WIDTH), tail, st],
        out_shape=[jax.ShapeDtypeStruct((n_s, 1, WIDTH), F32),
                   jax.ShapeDtypeStruct((n_s, K_CONV - 1, width3), F32),
                   jax.ShapeDtypeStruct((n_s, N_HEADS, HEAD, HEAD), F32)],
        compiler_params=_params(1),
        name="delta_step",
    )(qkv.reshape(n_s, 1, width3), ab.reshape(n_s, 1, LANES), conv_w, _head_params(a_log, dt_bias),
      gn.reshape(1, HEAD), cbuf, s0)
    return o.reshape(n_s, WIDTH), cnew, s_new


def _log_gamma(h):
    return math.log1p(-(2.0 ** (-5.0 - h)))


def _layer_norm(o, gain):
    oc = o - jnp.mean(o, axis=-1, keepdims=True)
    return oc * lax.rsqrt(jnp.mean(oc * oc, axis=-1, keepdims=True) + EPS) * gain


def _ret_kernel(q_ref, k_ref, v_ref, gn_ref, s0_ref, o_ref, sout_ref, st_ref):
    r = pl.program_id(1)
    rows = q_ref.shape[0]

    @pl.when(r == 0)
    def _init():
        st_ref[...] = s0_ref[...]

    ri = lax.broadcasted_iota(jnp.int32, (rows, 1), 0)
    ci = lax.broadcasted_iota(jnp.int32, (1, rows), 1)
    cnt_r = (ri + 1).astype(F32)
    cnt_c = (ci + 1).astype(F32)
    for h in range(N_HEADS):
        log_gamma = _log_gamma(h)
        cols = slice(h * HEAD, (h + 1) * HEAD)
        gc = cnt_r * log_gamma
        decay = jnp.exp(jnp.where(ri >= ci, gc - cnt_c * log_gamma, -jnp.inf))
        qh = q_ref[:, cols]
        kh = k_ref[:, cols] * HEAD ** -0.5
        vh = v_ref[:, cols]
        state = st_ref[h]
        o = jnp.exp(gc) * _mm(qh, state) + _mm(_mm_nt(qh, kh) * decay, vh)
        gl = rows * log_gamma
        st_ref[h] = math.exp(gl) * state + _mm_tn(kh * jnp.exp(gl - gc), vh)
        o_ref[:, cols] = _layer_norm(o, gn_ref[...])

    @pl.when(r == pl.num_programs(1) - 1)
    def _finish():
        sout_ref[...] = st_ref[...]


def _ret_step_kernel(q_ref, k_ref, v_ref, gn_ref, s0_ref, o_ref, sout_ref):
    eye = _eye(HEAD)
    for h in range(N_HEADS):
        cols = slice(h * HEAD, (h + 1) * HEAD)
        k_col = _as_column(k_ref[:, cols] * HEAD ** -0.5, eye)
        state = math.exp(_log_gamma(h)) * s0_ref[h] + k_col * v_ref[:, cols]
        sout_ref[h] = state
        o = jnp.sum(state * _as_column(q_ref[:, cols], eye), axis=0, keepdims=True)
        o_ref[:, cols] = _layer_norm(o, gn_ref[...])


def _retention_step(q, k, v, gn, s0):
    n_s = q.shape[0]
    row = pl.BlockSpec((None, 1, WIDTH), lambda b: (b, 0, 0))
    st = pl.BlockSpec((None, N_HEADS, HEAD, HEAD), lambda b: (b, 0, 0, 0))
    r3 = lambda t: t.reshape(n_s, 1, WIDTH)
    o, s_new = pl.pallas_call(
        _ret_step_kernel,
        grid=(n_s,),
        in_specs=[row, row, row, pl.BlockSpec((1, HEAD), lambda b: (0, 0)), st],
        out_specs=[row, st],
        out_shape=[jax.ShapeDtypeStruct((n_s, 1, WIDTH), F32),
                   jax.ShapeDtypeStruct((n_s, N_HEADS, HEAD, HEAD), F32)],
        compiler_params=_params(1),
        name="retention_step",
    )(r3(q), r3(k), r3(v), gn.reshape(1, HEAD), s0)
    return o.reshape(n_s, WIDTH), s_new


def _retention(q, k, v, gn, s0, n_b, seq):
    rows = min(RET_ROWS, seq)
    assert seq % rows == 0
    nr = seq // rows
    tok = pl.BlockSpec((rows, WIDTH), lambda b, r: (b * nr + r, 0))
    st = pl.BlockSpec((None, N_HEADS, HEAD, HEAD), lambda b, r: (b, 0, 0, 0))
    return pl.pallas_call(
        _ret_kernel,
        grid=(n_b, nr),
        in_specs=[tok, tok, tok, pl.BlockSpec((1, HEAD), lambda b, r: (0, 0)), st],
        out_specs=[tok, st],
        out_shape=[jax.ShapeDtypeStruct((n_b * seq, WIDTH), F32),
                   jax.ShapeDtypeStruct((n_b, N_HEADS, HEAD, HEAD), F32)],
        scratch_shapes=[pltpu.VMEM((N_HEADS, HEAD, HEAD), F32)],
        compiler_params=_params(2),
        name="retention",
    )(q, k, v, gn.reshape(1, HEAD), s0)


def _s5_prep_kernel(are_ref, aim_ref, ldt_ref, brt_ref, bit_ref, lbr_ref, lbi_ref, bbr_ref, bbi_ref):
    dt = jnp.exp(ldt_ref[...])
    ar, ai = are_ref[...], aim_ref[...]
    mag = jnp.exp(ar * dt)
    ang = ai * dt
    lr, li = mag * jnp.cos(ang), mag * jnp.sin(ang)
    den = ar * ar + ai * ai
    fr = ((lr - 1.0) * ar + li * ai) / den
    fi = (li * ar - (lr - 1.0) * ai) / den
    lbr_ref[...] = lr
    lbi_ref[...] = li
    brt, bit = brt_ref[...], bit_ref[...]
    bbr_ref[...] = fr[:, None, :] * brt - fi[:, None, :] * bit
    bbi_ref[...] = fr[:, None, :] * bit + fi[:, None, :] * brt


def _s5_prep(a_re, a_im, log_dt, b_re, b_im):
    n_g, n_p = a_re.shape
    brt, bit = jnp.swapaxes(b_re, 1, 2), jnp.swapaxes(b_im, 1, 2)
    gp = jax.ShapeDtypeStruct((n_g, n_p), F32)
    gcp = jax.ShapeDtypeStruct(brt.shape, F32)
    return pl.pallas_call(_s5_prep_kernel, out_shape=[gp, gp, gcp, gcp], name="s5_prep")(
        a_re, a_im, log_dt.reshape(n_g, 1), brt, bit)


def _block_diag(blocks):
    n, r, c = blocks.shape
    return jnp.einsum("grc,gh->grhc", blocks, jnp.eye(n, dtype=blocks.dtype)).reshape(n * r, n * c)


def _s5_kernel(n_b, n_t, u_ref, x0r_ref, x0i_ref, lbr_ref, lbi_ref, bb_ref, cc_ref, d_ref,
               y_ref, xr_ref, xi_ref, utb_ref, x_ref, carry_ref):
    i = pl.program_id(0)
    n_p = lbr_ref.shape[1]
    half_in = S5_HALF_GROUPS * GROUP_C
    half_st = S5_HALF_GROUPS * P_C

    @pl.when(i == 0)
    def _init():
        carry_ref[:, :n_p] = x0r_ref[...]
        carry_ref[:, n_p:] = x0i_ref[...]

    n_lc = WIDTH // LANES
    if n_t > 1:
        for b in range(n_b):
            for c in range(n_lc):
                utb_ref[c, pl.ds(b, n_t, stride=n_b), :] = u_ref[:, b * WIDTH + c * LANES:b * WIDTH + (c + 1) * LANES]
        u = jnp.concatenate([utb_ref[c] for c in range(n_lc)], axis=1)
    else:
        u = u_ref[...]
    ub = u.astype(BF16)
    for half in range(2):
        for part in range(2):
            c0 = part * n_p + half * half_st
            x_ref[:, c0:c0 + half_st] = jnp.dot(ub[:, half * half_in:(half + 1) * half_in],
                                                bb_ref[half * 2 + part], preferred_element_type=F32)

    if n_t == 1:
        lr, li = lbr_ref[...], lbi_ref[...]
        x0r, x0i = carry_ref[:, :n_p], carry_ref[:, n_p:]
        xr = lr * x0r - li * x0i + x_ref[:, :n_p]
        xi = lr * x0i + li * x0r + x_ref[:, n_p:]
        x_ref[:, :n_p] = xr
        x_ref[:, n_p:] = xi
        carry_ref[:, :n_p] = xr
        carry_ref[:, n_p:] = xi
    else:
        for lg in range(n_p // S5_LANE_GROUP):
            l0 = lg * S5_LANE_GROUP
            lr = jnp.broadcast_to(lbr_ref[:, l0:l0 + S5_LANE_GROUP], (n_b, S5_LANE_GROUP))
            li = jnp.broadcast_to(lbi_ref[:, l0:l0 + S5_LANE_GROUP], (n_b, S5_LANE_GROUP))

            def body(t, carry, l0=l0, lr=lr, li=li):
                xr, xi = carry
                row = pl.multiple_of(t * n_b, n_b)
                nxr = lr * xr - li * xi + x_ref[pl.ds(row, n_b), l0:l0 + S5_LANE_GROUP]
                nxi = lr * xi + li * xr + x_ref[pl.ds(row, n_b), n_p + l0:n_p + l0 + S5_LANE_GROUP]
                x_ref[pl.ds(row, n_b), l0:l0 + S5_LANE_GROUP] = nxr
                x_ref[pl.ds(row, n_b), n_p + l0:n_p + l0 + S5_LANE_GROUP] = nxi
                return nxr, nxi

            xr, xi = lax.fori_loop(
                0, n_t, body,
                (carry_ref[:, l0:l0 + S5_LANE_GROUP], carry_ref[:, n_p + l0:n_p + l0 + S5_LANE_GROUP]),
                unroll=4)
            carry_ref[:, l0:l0 + S5_LANE_GROUP] = xr
            carry_ref[:, n_p + l0:n_p + l0 + S5_LANE_GROUP] = xi

    ys = []
    for half in range(2):
        xr_b = x_ref[:, half * half_st:(half + 1) * half_st].astype(BF16)
        xi_b = x_ref[:, n_p + half * half_st:n_p + (half + 1) * half_st].astype(BF16)
        ys.append(jnp.dot(xr_b, cc_ref[half * 2], preferred_element_type=F32)
                  + jnp.dot(xi_b, cc_ref[half * 2 + 1], preferred_element_type=F32))
    yg = _gelu_tanh(jnp.concatenate(ys, axis=1) + d_ref[...] * u)
    if n_t > 1:
        for c in range(n_lc):
            utb_ref[c] = yg[:, c * LANES:(c + 1) * LANES]
        for b in range(n_b):
            for c in range(n_lc):
                y_ref[:, b * WIDTH + c * LANES:b * WIDTH + (c + 1) * LANES] = utb_ref[c, pl.ds(b, n_t, stride=n_b), :]
    else:
        y_ref[...] = yg

    @pl.when(i == pl.num_programs(0) - 1)
    def _finish():
        xr_ref[...] = carry_ref[:, :n_p]
        xi_ref[...] = carry_ref[:, n_p:]


def _s5(u, x0_re, x0_im, lb_re, lb_im, bb4, cc4, d_skip, n_b, seq):
    n_p = lb_re.shape[1]
    n_t = min(S5_STEPS, seq)
    n_steps = seq // n_t
    rows = n_t * n_b
    u_block = (n_t, n_b * WIDTH) if seq > 1 else (n_b, WIDTH)
    const2 = lambda i: (0, 0)
    const3 = lambda i: (0, 0, 0)
    return pl.pallas_call(
        functools.partial(_s5_kernel, n_b, n_t),
        grid=(n_steps,),
        in_specs=[pl.BlockSpec(u_block, lambda i: (i, 0)),
                  pl.BlockSpec((n_b, n_p), const2), pl.BlockSpec((n_b, n_p), const2),
                  pl.BlockSpec((1, n_p), const2), pl.BlockSpec((1, n_p), const2),
                  pl.BlockSpec(bb4.shape, const3), pl.BlockSpec(cc4.shape, const3),
                  pl.BlockSpec((1, WIDTH), const2)],
        out_specs=[pl.BlockSpec(u_block, lambda i: (i, 0)),
                   pl.BlockSpec((n_b, n_p), const2), pl.BlockSpec((n_b, n_p), const2)],
        out_shape=[jax.ShapeDtypeStruct(u.shape, F32),
                   jax.ShapeDtypeStruct((n_b, n_p), F32), jax.ShapeDtypeStruct((n_b, n_p), F32)],
        scratch_shapes=[pltpu.VMEM((WIDTH // LANES, rows, LANES), F32), pltpu.VMEM((rows, 2 * n_p), F32),
                        pltpu.VMEM((n_b, 2 * n_p), F32)],
        compiler_params=_params(1),
        name="s5_scan",
    )(u, x0_re, x0_im, lb_re, lb_im, bb4, cc4, d_skip.reshape(1, WIDTH))


def _out_kernel(glu, a1_ref, g1_ref, a2_ref, g2_ref, x_ref, gate_ref, w_ref, wg_ref, o_ref):
    a1 = a1_ref[...]
    if glu:
        a1 = a1 * _sigmoid(jnp.dot(a1.astype(BF16), wg_ref[...], preferred_element_type=F32))
    y = (jnp.dot((a1 * g1_ref[...]).astype(BF16), w_ref[0:WIDTH, :], preferred_element_type=F32)
         + jnp.dot((a2_ref[...] * g2_ref[...]).astype(BF16), w_ref[WIDTH:2 * WIDTH, :],
                   preferred_element_type=F32))
    o_ref[...] = x_ref[...] + gate_ref[...] * y


def _out_proj(a1, g1, a2, g2, x3, gate, w, wg, glu, a1_by_time):
    n_g, rows, d = x3.shape
    tm = min(ROW_TILE, rows)
    nt = rows // tm
    r_mod = gate.shape[1]
    tok = pl.BlockSpec((tm, WIDTH), lambda gi, i: (gi * nt + i, 0))
    a1_spec = pl.BlockSpec((tm, WIDTH), lambda gi, i: (i, gi)) if a1_by_time else tok
    const = lambda gi, i: (0, 0)
    return pl.pallas_call(
        functools.partial(_out_kernel, glu),
        grid=(n_g, nt),
        in_specs=[a1_spec, tok, tok, tok,
                  pl.BlockSpec((None, tm, d), lambda gi, i: (gi, i, 0)),
                  pl.BlockSpec((None, r_mod, d), lambda gi, i: (gi, 0, 0)),
                  pl.BlockSpec(w.shape, const), pl.BlockSpec(wg.shape, const)],
        out_specs=pl.BlockSpec((None, tm, d), lambda gi, i: (gi, i, 0)),
        out_shape=jax.ShapeDtypeStruct(x3.shape, F32),
        compiler_params=_params(2),
        name="out_proj",
    )(a1, g1, a2, g2, x3, gate, w, wg)


def _pad_cols(w, n):
    return jnp.pad(w, ((0, 0), (0, n - w.shape[1])))


def _even_layer(li, x3, mods, norm_g, p, attn_fn, delta_fn):
    shift, scale, gate = mods
    n_g, rows, d = x3.shape
    lam_init = 0.8 - 0.6 * math.exp(-0.3 * li)
    n_cols = p["w_in"].shape[1]
    n_pad = -(-n_cols // LANES) * LANES
    w_in = _pad_cols(p["w_in"], n_pad).astype(BF16)
    segs = ((0, WIDTH, "qnorm", BF16, False), (WIDTH, WIDTH, "knorm", F32, False),
            (2 * WIDTH, WIDTH, "raw", F32, False), (3 * WIDTH, WIDTH, "silu", F32, False),
            (4 * WIDTH, 3 * WIDTH, "raw", F32, False), (7 * WIDTH, WIDTH, "silu", F32, False),
            (8 * WIDTH, LANES, "raw", F32, False))
    bd = _block_diag(jnp.full((WIDTH // DQK, DQK, DQK), 1.0 / DQK, F32)).astype(BF16)
    qg = jnp.tile(p["qn_g"], WIDTH // DQK).reshape(1, WIDTH)
    kg = jnp.tile(p["kn_g"], WIDTH // DQK).reshape(1, WIDTH)
    q, k, v, za, qkv_b, zb, ab = _in_proj(x3, shift, scale, norm_g, w_in, segs, bd, qg, kg)
    lamp = jnp.zeros((SUBLANES, LANES), F32)
    for i, name in enumerate(("lam_q1", "lam_k1", "lam_q2", "lam_k2")):
        lamp = lamp.at[i, :DQK].set(p[name])
    sg = p["subln_g"].reshape(1, HEAD)
    oa = attn_fn(q, k, v, lamp, sg, lam_init)
    ob, conv_new, s_new = delta_fn(qkv_b, ab, p["conv_w"], p["a_log"], p["dt_bias"], p["gn_b"])
    w_out = p["w_out"].astype(BF16)
    x_new = _out_proj(oa, za, ob, zb, x3, gate, w_out, jnp.zeros((SUBLANES, LANES), BF16), False, False)
    return x_new, (k, v, conv_new, s_new)


def _odd_layer(x3, mods, norm_g, p, s5_mats, x0_re, x0_im, r0, by_time):
    shift, scale, gate = mods
    n_g, rows, d = x3.shape
    w_in = p["w_in"].astype(BF16)
    segs = ((0, WIDTH, "raw", F32, by_time), (WIDTH, WIDTH, "silu", F32, False),
            (2 * WIDTH, WIDTH, "raw", F32, False), (3 * WIDTH, WIDTH, "raw", F32, False),
            (4 * WIDTH, WIDTH, "raw", F32, False), (5 * WIDTH, WIDTH, "silu", F32, False))
    dummy = jnp.zeros((SUBLANES, LANES), BF16)
    dummy_g = jnp.zeros((1, LANES), F32)
    u, zc, qd, kd, vd, zd = _in_proj(x3, shift, scale, norm_g, w_in, segs, dummy, dummy_g, dummy_g)
    lb_re, lb_im, bb4, cc4 = s5_mats
    n_b = n_g if by_time else rows
    seq = rows if by_time else 1
    yg, xr, xi = _s5(u, x0_re, x0_im, lb_re, lb_im, bb4, cc4, p["s5_d"], n_b, seq)
    if by_time:
        od, r_new = _retention(qd, kd, vd, p["gn_d"], r0, n_g, rows)
    else:
        od, r_new = _retention_step(qd, kd, vd, p["gn_d"], r0)
    x_new = _out_proj(yg, zc, od, zd, x3, gate, p["w_out"].astype(BF16), p["w_glu"].astype(BF16), True, by_time)
    return x_new, (xr, xi, r_new)


def kernel(x_prompt, x_sample, c_prompt, c_sample, page_table, cache_k, cache_v, state_b_conv, state_b_ssm,
           state_c_re, state_c_im, state_d_ret, norm_g, w_ada, b_ada, w_in_e, w_out_e, qn_g, kn_g,
           lam_q1, lam_k1, lam_q2, lam_k2, subln_g, conv_w, a_log, dt_bias, gn_b, w_in_o, w_out_o,
           s5_a_re, s5_a_im, s5_b_re, s5_b_im, s5_c_re, s5_c_im, s5_d, s5_log_dt, w_glu, gn_d):
    n_bp, seq, d = x_prompt.shape
    n_bs = x_sample.shape[0]
    depth = norm_g.shape[0]
    n_pages, page = page_table.shape[1], cache_k.shape[2]
    n_g, n_p = s5_a_re.shape[1], s5_a_re.shape[2]

    mod = _modulation(jnp.concatenate([c_prompt, c_sample], axis=0), w_ada, b_ada)
    xp = x_prompt
    xs = x_sample.reshape(1, n_bs, d)
    outs = {name: [] for name in ("k_p", "v_p", "k_s", "v_s", "cv_p", "cv_s", "dl_p", "dl_s",
                                  "s5r_p", "s5i_p", "s5r_s", "s5i_s", "rt_p", "rt_s")}
    for li in range(depth):
        mods_p = tuple(mod[li, :n_bp, j * d:(j + 1) * d].reshape(n_bp, 1, d) for j in range(3))
        mods_s = tuple(mod[li, n_bp:, j * d:(j + 1) * d].reshape(1, n_bs, d) for j in range(3))
        if li % 2 == 0:
            e = li // 2
            p = dict(w_in=w_in_e[e], w_out=w_out_e[e], qn_g=qn_g[e], kn_g=kn_g[e], lam_q1=lam_q1[e],
                     lam_k1=lam_k1[e], lam_q2=lam_q2[e], lam_k2=lam_k2[e], subln_g=subln_g[e],
                     conv_w=conv_w[e], a_log=a_log[e], dt_bias=dt_bias[e], gn_b=gn_b[e])
            attn_p = functools.partial(_attn_prompt, n_b=n_bp, seq=seq)
            delta_p = functools.partial(
                _delta, cbuf=jnp.zeros((n_bp, K_CONV - 1, 3 * WIDTH), F32),
                s0=jnp.zeros((n_bp, N_HEADS, HEAD, HEAD), F32), n_b=n_bp, seq=seq)
            xp, (k_p, v_p, c_p, s_p) = _even_layer(li, xp, mods_p, norm_g[li], p, attn_p, delta_p)
            ck = cache_k[e].reshape(-1, page, WIDTH)
            cv = cache_v[e].reshape(-1, page, WIDTH)

            def attn_s(q, k, v, lamp, sg, lam_init, ck=ck, cv=cv):
                return _attn_decode(q, k, v, ck, cv, page_table, lamp, sg, lam_init)

            delta_s = functools.partial(_delta_step, cbuf=state_b_conv[e], s0=state_b_ssm[e])
            xs, (k_s, v_s, c_s, s_s) = _even_layer(li, xs, mods_s, norm_g[li], p, attn_s, delta_s)
            outs["k_p"].append(k_p.reshape(n_bp, seq, N_HEADS, HEAD))
            outs["v_p"].append(v_p.reshape(n_bp, seq, N_HEADS, HEAD))
            outs["k_s"].append(k_s.reshape(n_bs, 1, N_HEADS, HEAD))
            outs["v_s"].append(v_s.reshape(n_bs, 1, N_HEADS, HEAD))
            outs["cv_p"].append(c_p)
            outs["cv_s"].append(c_s)
            outs["dl_p"].append(s_p)
            outs["dl_s"].append(s_s)
        else:
            o = li // 2
            p = dict(w_in=w_in_o[o], w_out=w_out_o[o], s5_d=s5_d[o], w_glu=w_glu[o], gn_d=gn_d[o])
            lb_re, lb_im, bbr, bbi = _s5_prep(s5_a_re[o], s5_a_im[o], s5_log_dt[o], s5_b_re[o], s5_b_im[o])
            hg = S5_HALF_GROUPS
            bb4 = jnp.stack([_block_diag(t[h * hg:(h + 1) * hg]) for h in range(2) for t in (bbr, bbi)]).astype(BF16)
            cre = jnp.swapaxes(s5_c_re[o], 1, 2)
            cim = -jnp.swapaxes(s5_c_im[o], 1, 2)
            cc4 = jnp.stack([_block_diag(t[h * hg:(h + 1) * hg]) for h in range(2) for t in (cre, cim)]).astype(BF16)
            s5_mats = (lb_re.reshape(1, n_g * n_p), lb_im.reshape(1, n_g * n_p), bb4, cc4)
            zeros_c = jnp.zeros((n_bp, n_g * n_p), F32)
            xp, (r_p, i_p, t_p) = _odd_layer(xp, mods_p, norm_g[li], p, s5_mats, zeros_c, zeros_c,
                                             jnp.zeros((n_bp, N_HEADS, HEAD, HEAD), F32), True)
            xs, (r_s, i_s, t_s) = _odd_layer(xs, mods_s, norm_g[li], p, s5_mats,
                                             state_c_re[o].reshape(n_bs, n_g * n_p),
                                             state_c_im[o].reshape(n_bs, n_g * n_p), state_d_ret[o], False)
            outs["s5r_p"].append(r_p.reshape(n_bp, n_g, n_p))
            outs["s5i_p"].append(i_p.reshape(n_bp, n_g, n_p))
            outs["s5r_s"].append(r_s.reshape(n_bs, n_g, n_p))
            outs["s5i_s"].append(i_s.reshape(n_bs, n_g, n_p))
            outs["rt_p"].append(t_p)
            outs["rt_s"].append(t_s)
    st = lambda name: jnp.stack(outs[name])
    return (xp, xs.reshape(n_bs, 1, d), st("k_p"), st("v_p"), st("k_s"), st("v_s"), st("cv_p"), st("cv_s"),
            st("dl_p"), st("dl_s"), st("s5r_p"), st("s5i_p"), st("s5r_s"), st("s5i_s"), st("rt_p"), st("rt_s"))
```

```python
import functools
import math

import jax
import jax.numpy as jnp
from jax import lax
from jax.experimental import pallas as pl
from jax.experimental.pallas import tpu as pltpu

F32 = jnp.float32
BF16 = jnp.bfloat16
EPS = 1e-6

LANES = 128
SUBLANES = 8
VMEM_LIMIT_BYTES = 48 * 1024 * 1024

HEAD = 128
DQK = HEAD // 2
N_HEADS = 4
WIDTH = N_HEADS * HEAD
K_CONV = 4
DELTA_CHUNK = 64
GROUP_C = 16
P_C = 64
S5_HALF_GROUPS = 16

ROW_TILE = 256
ATTN_TILE = 512
ATTN_ROW_CHUNK = 256
DELTA_ROWS = 256
RET_ROWS = 256
S5_STEPS = 64
S5_LANE_GROUP = 512


def _params(n_axes, vmem=VMEM_LIMIT_BYTES):
    return pltpu.CompilerParams(dimension_semantics=("arbitrary",) * n_axes, vmem_limit_bytes=vmem)


def _sigmoid(x):
    return 1.0 / (1.0 + jnp.exp(-x))


def _silu(x):
    return x * _sigmoid(x)


def _softplus(x):
    return jnp.maximum(x, 0.0) + jnp.log1p(jnp.exp(-jnp.abs(x)))


def _gelu_tanh(x):
    return 0.5 * x * (1.0 + jnp.tanh(math.sqrt(2.0 / math.pi) * (x + 0.044715 * (x * x * x))))


def _mm(a, b):
    return jnp.dot(a.astype(BF16), b.astype(BF16), preferred_element_type=F32)


def _mm_nt(a, b):
    return lax.dot_general(a.astype(BF16), b.astype(BF16), (((1,), (1,)), ((), ())),
                           preferred_element_type=F32)


def _mod_kernel(c_ref, w_ref, b_ref, o_ref):
    o_ref[...] = _mm(_silu(c_ref[...]), w_ref[...]) + b_ref[...]


def _modulation(c_all, w_ada, b_ada):
    depth, d, n3 = w_ada.shape
    rows = c_all.shape[0]
    tn = d
    return pl.pallas_call(
        _mod_kernel,
        grid=(depth, n3 // tn),
        in_specs=[pl.BlockSpec((rows, d), lambda l, n: (0, 0)),
                  pl.BlockSpec((None, d, tn), lambda l, n: (l, 0, n)),
                  pl.BlockSpec((None, 1, tn), lambda l, n: (l, 0, n))],
        out_specs=pl.BlockSpec((None, rows, tn), lambda l, n: (l, 0, n)),
        out_shape=jax.ShapeDtypeStruct((depth, rows, n3), F32),
        compiler_params=_params(2),
        name="modulation",
    )(c_all, w_ada, b_ada.reshape(depth, 1, n3))


def _in_kernel(segs, x_ref, shift_ref, scale_ref, g_ref, w_ref, bd_ref, qg_ref, kg_ref, *refs):
    out_refs, h_ref = refs[:-1], refs[-1]
    x = x_ref[...]
    h = x * lax.rsqrt(jnp.mean(x * x, axis=-1, keepdims=True) + EPS) * g_ref[...]
    h_ref[...] = (h * (1.0 + scale_ref[...]) + shift_ref[...]).astype(BF16)
    for (c0, width, kind), o_ref in zip(segs, out_refs):
        acc = jnp.dot(h_ref[...], w_ref[:, c0:c0 + width], preferred_element_type=F32)
        if kind == "silu":
            acc = _silu(acc)
        elif kind in ("qnorm", "knorm"):
            sq = acc * acc
            hi = sq.astype(BF16)
            lo = (sq - hi.astype(F32)).astype(BF16)
            ms = (jnp.dot(hi, bd_ref[...], preferred_element_type=F32)
                  + jnp.dot(lo, bd_ref[...], preferred_element_type=F32))
            acc = acc * lax.rsqrt(ms + EPS) * (qg_ref if kind == "qnorm" else kg_ref)[...]
            if kind == "qnorm":
                acc = acc * DQK ** -0.5
        o_ref[...] = acc.astype(o_ref.dtype)


def _in_proj(x3, shift, scale, g, w, segs, bd, qg, kg):
    n_g, rows, d = x3.shape
    tm = min(ROW_TILE, rows)
    nt = rows // tm
    r_mod = shift.shape[1]
    assert r_mod in (1, tm)
    out_shapes, out_specs = [], []
    for _, width, _, dtype, by_time in segs:
        if by_time:
            out_shapes.append(jax.ShapeDtypeStruct((rows, n_g * width), dtype))
            out_specs.append(pl.BlockSpec((tm, width), lambda gi, i: (i, gi)))
        else:
            out_shapes.append(jax.ShapeDtypeStruct((n_g * rows, width), dtype))
            out_specs.append(pl.BlockSpec((tm, width), lambda gi, i: (gi * nt + i, 0)))
    const = lambda gi, i: (0, 0)
    return pl.pallas_call(
        functools.partial(_in_kernel, tuple(s[:3] for s in segs)),
        grid=(n_g, nt),
        in_specs=[pl.BlockSpec((None, tm, d), lambda gi, i: (gi, i, 0)),
                  pl.BlockSpec((None, r_mod, d), lambda gi, i: (gi, 0, 0)),
                  pl.BlockSpec((None, r_mod, d), lambda gi, i: (gi, 0, 0)),
                  pl.BlockSpec((1, d), const),
                  pl.BlockSpec(w.shape, const),
                  pl.BlockSpec(bd.shape, const),
                  pl.BlockSpec(qg.shape, const),
                  pl.BlockSpec(kg.shape, const)],
        out_specs=out_specs,
        out_shape=out_shapes,
        scratch_shapes=[pltpu.VMEM((tm, d), BF16)],
        compiler_params=_params(2),
        name="in_proj",
    )(x3, shift, scale, g.reshape(1, d), w, bd, qg, kg)


def _lambda_value(lamp, lam_init):
    a = jnp.sum(lamp[0:1] * lamp[1:2], axis=-1, keepdims=True)
    b = jnp.sum(lamp[2:3] * lamp[3:4], axis=-1, keepdims=True)
    return jnp.exp(a) - jnp.exp(b) + lam_init


def _sub_ln(o, gain, lam_init):
    return o * lax.rsqrt(jnp.mean(o * o, axis=-1, keepdims=True) + EPS) * gain * (1.0 - lam_init)


def _attn_kernel(lam_init, row_chunk, q_ref, k_ref, v_ref, kbias_ref, lamp_ref, sg_ref, o_ref,
                 qs_ref, m_ref, l_ref, acc_ref):
    qi, ki = pl.program_id(2), pl.program_id(3)
    tq, tk = q_ref.shape[0], k_ref.shape[0]

    @pl.when(ki == 0)
    def _init():
        m_ref[...] = jnp.full(m_ref.shape, -jnp.inf, F32)
        l_ref[...] = jnp.zeros(l_ref.shape, F32)
        acc_ref[...] = jnp.zeros(acc_ref.shape, F32)
        q = q_ref[...]
        lane = lax.broadcasted_iota(jnp.int32, (1, HEAD), 1)
        qs_ref[0:tq, :] = jnp.where(lane < DQK, q, jnp.zeros_like(q))
        qs_ref[tq:2 * tq, :] = jnp.where(lane >= DQK, q, jnp.zeros_like(q))

    def step(masked):
        kb = k_ref[...].astype(BF16)
        vb = v_ref[...].astype(BF16)
        kbias = kbias_ref[...]
        for r0 in range(0, 2 * tq, row_chunk):
            rs = slice(r0, r0 + row_chunk)
            s = _mm_nt(qs_ref[rs, :], kb) + kbias
            if masked:
                q_row = lax.broadcasted_iota(jnp.int32, (row_chunk, tk), 0) + (r0 % tq)
                col = lax.broadcasted_iota(jnp.int32, (row_chunk, tk), 1)
                s = jnp.where(col <= q_row, s, -jnp.inf)
            m_prev = m_ref[rs, :]
            m_new = jnp.maximum(m_prev, jnp.max(s, axis=-1, keepdims=True))
            alpha = jnp.exp(m_prev - m_new)
            p = jnp.exp(s - m_new)
            l_ref[rs, :] = alpha * l_ref[rs, :] + jnp.sum(p, axis=-1, keepdims=True)
            acc_ref[rs, :] = alpha * acc_ref[rs, :] + jnp.dot(p.astype(BF16), vb, preferred_element_type=F32)
            m_ref[rs, :] = m_new

    @pl.when(ki < qi)
    def _off_diagonal():
        step(False)

    @pl.when(ki == qi)
    def _diagonal():
        step(True)
        lam = _lambda_value(lamp_ref[...], lam_init)
        o = acc_ref[0:tq, :] / l_ref[0:tq, :] - lam * (acc_ref[tq:2 * tq, :] / l_ref[tq:2 * tq, :])
        o_ref[...] = _sub_ln(o, sg_ref[...], lam_init)


def _alibi_slopes(n_heads):
    return jnp.exp2(-8.0 * jnp.arange(1, n_heads + 1, dtype=F32) / n_heads)


def _attn_prompt(q, k, v, lamp, sg, lam_init, n_b, seq):
    t = min(ATTN_TILE, seq)
    nq = seq // t
    row_chunk = min(ATTN_ROW_CHUNK, t)
    kbias = (_alibi_slopes(N_HEADS)[:, None] * jnp.arange(seq, dtype=F32)[None, :]).reshape(N_HEADS, nq, 1, t)
    kv_map = lambda b, h, qi, ki: (b * nq + jnp.minimum(ki, qi), h)
    return pl.pallas_call(
        functools.partial(_attn_kernel, lam_init, row_chunk),
        grid=(n_b, N_HEADS, nq, nq),
        in_specs=[pl.BlockSpec((t, HEAD), lambda b, h, qi, ki: (b * nq + qi, h)),
                  pl.BlockSpec((t, HEAD), kv_map),
                  pl.BlockSpec((t, HEAD), kv_map),
                  pl.BlockSpec((None, None, 1, t), lambda b, h, qi, ki: (h, jnp.minimum(ki, qi), 0, 0)),
                  pl.BlockSpec(lamp.shape, lambda b, h, qi, ki: (0, 0)),
                  pl.BlockSpec((1, HEAD), lambda b, h, qi, ki: (0, 0))],
        out_specs=pl.BlockSpec((t, HEAD), lambda b, h, qi, ki: (b * nq + qi, h)),
        out_shape=jax.ShapeDtypeStruct((n_b * seq, WIDTH), F32),
        scratch_shapes=[pltpu.VMEM((2 * t, HEAD), BF16), pltpu.VMEM((2 * t, 1), F32),
                        pltpu.VMEM((2 * t, 1), F32), pltpu.VMEM((2 * t, HEAD), F32)],
        compiler_params=_params(4),
        name="attn_prompt",
    )(q, k, v, kbias, lamp, sg)


def _decode_kernel(lam_init, n_pages, pt_ref, q_ref, kn_ref, vn_ref, nsl_ref, lamp_ref, sg_ref, *refs):
    del pt_ref
    k_refs, v_refs, o_ref = refs[:n_pages], refs[n_pages:2 * n_pages], refs[2 * n_pages]
    page = k_refs[0].shape[0]
    n_rows = 4 * N_HEADS
    row = lax.broadcasted_iota(jnp.int32, (n_rows, WIDTH), 0)
    lane = lax.broadcasted_iota(jnp.int32, (n_rows, WIDTH), 1)
    seg = lane // DQK
    qm = jnp.where((seg % 2) * N_HEADS + seg // 2 == row, q_ref[...].astype(F32), 0.0).astype(BF16)
    s = jnp.concatenate([_mm_nt(qm, k_refs[j][...]) for j in range(n_pages)], axis=1)
    past = n_pages * page
    pos = lax.broadcasted_iota(jnp.int32, (1, past), 1)
    s = s + nsl_ref[...] * (past - pos).astype(F32)
    s_self = jnp.sum(qm.astype(F32) * kn_ref[...].astype(BF16).astype(F32), axis=-1, keepdims=True)
    m = jnp.maximum(jnp.max(s, axis=-1, keepdims=True), s_self)
    p = jnp.exp(s - m)
    p_self = jnp.exp(s_self - m)
    denom = jnp.sum(p, axis=-1, keepdims=True) + p_self
    lam = _lambda_value(lamp_ref[...], lam_init)
    r1 = row[:, 0:1]
    coef = jnp.where(r1 < N_HEADS, 1.0, jnp.where(r1 < 2 * N_HEADS, -lam, 0.0)) / denom
    pw = p * coef
    acc = (p_self * coef) * vn_ref[...].astype(BF16).astype(F32)
    for j in range(n_pages):
        acc = acc + _mm(pw[:, j * page:(j + 1) * page], v_refs[j][...])
    own_head = (lane // HEAD == row % N_HEADS) & (row < 2 * N_HEADS)
    o = jnp.sum(jnp.where(own_head, acc, 0.0), axis=0, keepdims=True)
    for h in range(N_HEADS):
        cols = slice(h * HEAD, (h + 1) * HEAD)
        o_ref[:, cols] = _sub_ln(o[:, cols], sg_ref[...], lam_init)


def _attn_decode(q, k_new, v_new, cache_k, cache_v, page_table, lamp, sg, lam_init, page_off):
    n_s, n_pages = page_table.shape
    page = cache_k.shape[1]
    nsl = -_alibi_slopes(N_HEADS)
    nsl8 = jnp.concatenate([nsl, nsl, jnp.zeros((2 * N_HEADS,), F32)]).reshape(4 * N_HEADS, 1)
    row_spec = pl.BlockSpec((None, 1, WIDTH), lambda b, pt: (b, 0, 0))
    const = lambda b, pt: (0, 0)

    def page_spec(j):
        return pl.BlockSpec((None, page, WIDTH), lambda b, pt: (pt[b, j] + page_off, 0, 0))

    grid_spec = pltpu.PrefetchScalarGridSpec(
        num_scalar_prefetch=1,
        grid=(n_s,),
        in_specs=[row_spec, row_spec, row_spec,
                  pl.BlockSpec(nsl8.shape, const), pl.BlockSpec(lamp.shape, const),
                  pl.BlockSpec((1, HEAD), const)]
                 + [page_spec(j) for j in range(n_pages)] * 2,
        out_specs=row_spec,
    )
    out = pl.pallas_call(
        functools.partial(_decode_kernel, lam_init, n_pages),
        grid_spec=grid_spec,
        out_shape=jax.ShapeDtypeStruct((n_s, 1, WIDTH), F32),
        compiler_params=_params(1),
        name="attn_decode",
    )(page_table, q.reshape(n_s, 1, WIDTH), k_new.reshape(n_s, 1, WIDTH), v_new.reshape(n_s, 1, WIDTH),
      nsl8, lamp, sg, *([cache_k] * n_pages), *([cache_v] * n_pages))
    return out.reshape(n_s, WIDTH)


INV_BASE = SUBLANES


def _unit_lower_inverse(a, block, row, col):
    in_base = row // INV_BASE == col // INV_BASE
    power = jnp.where(in_base, a, 0.0)
    inv = jnp.where(row == col, 1.0, 0.0) - power
    order = 2
    while order < INV_BASE:
        power = _mm(power, power)
        inv = inv + _mm(inv, power)
        order *= 2
    size = INV_BASE
    while size < block:
        coupling = jnp.where((row // (2 * size) == col // (2 * size)) & (row // size != col // size), a, 0.0)
        inv = inv - _mm(_mm(inv, coupling), inv)
        size *= 2
    return inv


def _mm_tn(a, b):
    return lax.dot_general(a.astype(BF16), b.astype(BF16), (((0,), (0,)), ((), ())),
                           preferred_element_type=F32)


def _delta_kernel(chunk, qkv_ref, ab_ref, cw_ref, hp_ref, gn_ref, cbuf_ref, s0_ref, tri_ref,
                  o_ref, cnew_ref, sout_ref, xs_ref, st_ref):
    r = pl.program_id(1)
    rows = qkv_ref.shape[0]
    n_chunks = rows // chunk
    tail = K_CONV - 1

    @pl.when(r == 0)
    def _init():
        st_ref[...] = s0_ref[...]
        xs_ref[0:SUBLANES, :] = jnp.zeros((SUBLANES, xs_ref.shape[1]), F32)
        xs_ref[SUBLANES - tail:SUBLANES, :] = cbuf_ref[...]

    xs_ref[SUBLANES:SUBLANES + rows, :] = qkv_ref[...]
    w = cw_ref[...]
    y = w[0:1] * xs_ref[SUBLANES - 3:SUBLANES - 3 + rows, :]
    for j in range(1, K_CONV):
        y = y + w[j:j + 1] * xs_ref[SUBLANES - 3 + j:SUBLANES - 3 + j + rows, :]
    new_tail = xs_ref[SUBLANES + rows - tail:SUBLANES + rows, :]
    cnew_ref[...] = new_tail
    xs_ref[SUBLANES - tail:SUBLANES, :] = new_tail
    act = _silu(y)

    ab = ab_ref[...]
    hp = hp_ref[...]
    g_all = -jnp.exp(hp[0:1]) * _softplus(ab + hp[1:2])
    beta_all = _sigmoid(ab)
    gam_all = jnp.dot(tri_ref[...], g_all, preferred_element_type=F32, precision=lax.Precision.HIGHEST)

    n_st = N_HEADS * chunk
    ri = lax.broadcasted_iota(jnp.int32, (n_st, n_st), 0)
    ci = lax.broadcasted_iota(jnp.int32, (n_st, n_st), 1)
    same_head = ri // chunk == ci // chunk
    incl, strict, diag = same_head & (ri >= ci), same_head & (ri > ci), ri == ci
    gn = gn_ref[...]
    heads = [slice(h * chunk, (h + 1) * chunk) for h in range(N_HEADS)]
    states = [st_ref[h] for h in range(N_HEADS)]

    for c in range(n_chunks):
        rs = slice(c * chunk, (c + 1) * chunk)

        def stacked(col0):
            return jnp.concatenate([act[rs, col0 + h * HEAD:col0 + (h + 1) * HEAD] for h in range(N_HEADS)], axis=0)

        q, k, v = stacked(0), stacked(WIDTH), stacked(2 * WIDTH)
        q = q * lax.rsqrt(jnp.sum(q * q, axis=-1, keepdims=True) + EPS) * HEAD ** -0.5
        k = k * lax.rsqrt(jnp.sum(k * k, axis=-1, keepdims=True) + EPS)
        gc = jnp.concatenate([gam_all[rs, h:h + 1] for h in range(N_HEADS)], axis=0)
        bc = jnp.concatenate([beta_all[rs, N_HEADS + h:N_HEADS + h + 1] for h in range(N_HEADS)], axis=0)
        gr = jnp.sum(jnp.where(diag, gc, 0.0), axis=0, keepdims=True)
        decay = jnp.exp(jnp.where(incl, gc - gr, -jnp.inf))
        inv = _unit_lower_inverse(jnp.where(strict, bc * decay * _mm_nt(k, k), 0.0), chunk, ri, ci)
        eg = jnp.exp(gc)
        sol = _mm(inv, jnp.concatenate([bc * v, (bc * eg) * k], axis=1))
        qk = _mm_nt(q, k) * decay
        u = jnp.concatenate([sol[hs, :HEAD] - _mm(sol[hs, HEAD:], states[h]) for h, hs in enumerate(heads)], axis=0)
        o = eg * jnp.concatenate([_mm(q[hs], states[h]) for h, hs in enumerate(heads)], axis=0) + _mm(qk, u)
        for h, hs in enumerate(heads):
            gl = gc[(h + 1) * chunk - 1:(h + 1) * chunk, :]
            states[h] = jnp.exp(gl) * states[h] + _mm_tn(k[hs] * jnp.exp(gl - gc[hs]), u[hs])
        on = o * lax.rsqrt(jnp.mean(o * o, axis=-1, keepdims=True) + EPS) * gn
        for h, hs in enumerate(heads):
            o_ref[rs, h * HEAD:(h + 1) * HEAD] = on[hs]

    for h in range(N_HEADS):
        st_ref[h] = states[h]

    @pl.when(r == pl.num_programs(1) - 1)
    def _finish():
        sout_ref[...] = st_ref[...]


def _chunk_tri(rows, chunk):
    r = jnp.arange(rows)
    return ((r[:, None] >= r[None, :]) & (r[:, None] // chunk == r[None, :] // chunk)).astype(F32)


def _head_params(a_log, dt_bias):
    return jnp.zeros((SUBLANES, LANES), F32).at[0, :N_HEADS].set(a_log).at[1, :N_HEADS].set(dt_bias)


def _delta(qkv, ab, conv_w, a_log, dt_bias, gn, cbuf, s0, n_b, seq):
    chunk = min(DELTA_CHUNK, seq)
    rows = min(DELTA_ROWS, seq)
    assert seq % rows == 0 and rows % chunk == 0
    nr = seq // rows
    width3 = 3 * WIDTH
    hp = _head_params(a_log, dt_bias)
    tri = _chunk_tri(rows, chunk)
    const = lambda b, r: (0, 0)
    return pl.pallas_call(
        functools.partial(_delta_kernel, chunk),
        grid=(n_b, nr),
        in_specs=[pl.BlockSpec((rows, width3), lambda b, r: (b * nr + r, 0)),
                  pl.BlockSpec((rows, LANES), lambda b, r: (b * nr + r, 0)),
                  pl.BlockSpec((K_CONV, width3), const),
                  pl.BlockSpec((SUBLANES, LANES), const),
                  pl.BlockSpec((1, HEAD), const),
                  pl.BlockSpec((None, K_CONV - 1, width3), lambda b, r: (b, 0, 0)),
                  pl.BlockSpec((None, N_HEADS, HEAD, HEAD), lambda b, r: (b, 0, 0, 0)),
                  pl.BlockSpec((rows, rows), const)],
        out_specs=[pl.BlockSpec((rows, WIDTH), lambda b, r: (b * nr + r, 0)),
                   pl.BlockSpec((None, K_CONV - 1, width3), lambda b, r: (b, 0, 0)),
                   pl.BlockSpec((None, N_HEADS, HEAD, HEAD), lambda b, r: (b, 0, 0, 0))],
        out_shape=[jax.ShapeDtypeStruct((n_b * seq, WIDTH), F32),
                   jax.ShapeDtypeStruct((n_b, K_CONV - 1, width3), F32),
                   jax.ShapeDtypeStruct((n_b, N_HEADS, HEAD, HEAD), F32)],
        scratch_shapes=[pltpu.VMEM((rows + SUBLANES, width3), F32),
                        pltpu.VMEM((N_HEADS, HEAD, HEAD), F32)],
        compiler_params=_params(2),
        name="delta_rule",
    )(qkv, ab, conv_w, hp, gn.reshape(1, HEAD), cbuf, s0, tri)


def _as_column(row_vec, eye):
    return jnp.sum(jnp.where(eye, row_vec, 0.0), axis=1, keepdims=True)


def _eye(n):
    return lax.broadcasted_iota(jnp.int32, (n, n), 0) == lax.broadcasted_iota(jnp.int32, (n, n), 1)


def _delta_step_kernel(x_ref, ab_ref, cw_ref, hp_ref, gn_ref, cbuf_ref, s0_ref, o_ref, cnew_ref, sout_ref):
    x = x_ref[...]
    buf = cbuf_ref[...]
    w = cw_ref[...]
    y = w[0:1] * buf[0:1] + w[1:2] * buf[1:2] + w[2:3] * buf[2:3] + w[3:4] * x
    cnew_ref[0:2, :] = buf[1:3]
    cnew_ref[2:3, :] = x
    act = _silu(y)
    ab = ab_ref[...]
    hp = hp_ref[...]
    decay_all = jnp.exp(-jnp.exp(hp[0:1]) * _softplus(ab + hp[1:2]))
    beta_all = _sigmoid(ab)
    eye = _eye(HEAD)
    for h in range(N_HEADS):
        qh = act[:, h * HEAD:(h + 1) * HEAD]
        kh = act[:, WIDTH + h * HEAD:WIDTH + (h + 1) * HEAD]
        vh = act[:, 2 * WIDTH + h * HEAD:2 * WIDTH + (h + 1) * HEAD]
        qh = qh * lax.rsqrt(jnp.sum(qh * qh, axis=-1, keepdims=True) + EPS) * HEAD ** -0.5
        kh = kh * lax.rsqrt(jnp.sum(kh * kh, axis=-1, keepdims=True) + EPS)
        a = decay_all[:, h:h + 1]
        beta = beta_all[:, N_HEADS + h:N_HEADS + h + 1]
        state = s0_ref[h]
        k_col = _as_column(kh, eye)
        u = beta * (vh - a * jnp.sum(state * k_col, axis=0, keepdims=True))
        state = a * state + k_col * u
        sout_ref[h] = state
        o = jnp.sum(state * _as_column(qh, eye), axis=0, keepdims=True)
        o_ref[:, h * HEAD:(h + 1) * HEAD] = o * lax.rsqrt(jnp.mean(o * o, axis=-1, keepdims=True) + EPS) * gn_ref[...]


def _delta_step(qkv, ab, conv_w, a_log, dt_bias, gn, cbuf, s0, seq_off):
    n_s = qkv.shape[0]
    width3 = 3 * WIDTH
    const = lambda b: (0, 0)
    row = lambda width: pl.BlockSpec((None, 1, width), lambda b: (b, 0, 0))
    tail = pl.BlockSpec((None, K_CONV - 1, width3), lambda b: (b, 0, 0))
    st = pl.BlockSpec((None, N_HEADS, HEAD, HEAD), lambda b: (b, 0, 0, 0))
    tail_in = pl.BlockSpec((None, K_CONV - 1, width3), lambda b: (b + seq_off, 0, 0))
    st_in = pl.BlockSpec((None, N_HEADS, HEAD, HEAD), lambda b: (b + seq_off, 0, 0, 0))
    o, cnew, s_new = pl.pallas_call(
        _delta_step_kernel,
        grid=(n_s,),
        in_specs=[row(width3), row(LANES), pl.BlockSpec((K_CONV, width3), const),
                  pl.BlockSpec((SUBLANES, LANES), const), pl.BlockSpec((1, HEAD), const), tail_in, st_in],
        out_specs=[row(WIDTH), tail, st],
        out_shape=[jax.ShapeDtypeStruct((n_s, 1, WIDTH), F32),
                   jax.ShapeDtypeStruct((n_s, K_CONV - 1, width3), F32),
                   jax.ShapeDtypeStruct((n_s, N_HEADS, HEAD, HEAD), F32)],
        compiler_params=_params(1),
        name="delta_step",
    )(qkv.reshape(n_s, 1, width3), ab.reshape(n_s, 1, LANES), conv_w, _head_params(a_log, dt_bias),
      gn.reshape(1, HEAD), cbuf, s0)
    return o.reshape(n_s, WIDTH), cnew, s_new


def _log_gamma(h):
    return math.log1p(-(2.0 ** (-5.0 - h)))


def _layer_norm(o, gain):
    oc = o - jnp.mean(o, axis=-1, keepdims=True)
    return oc * lax.rsqrt(jnp.mean(oc * oc, axis=-1, keepdims=True) + EPS) * gain


def _ret_kernel(q_ref, k_ref, v_ref, gn_ref, s0_ref, o_ref, sout_ref, st_ref):
    r = pl.program_id(1)
    rows = q_ref.shape[0]

    @pl.when(r == 0)
    def _init():
        st_ref[...] = s0_ref[...]

    ri = lax.broadcasted_iota(jnp.int32, (rows, 1), 0)
    ci = lax.broadcasted_iota(jnp.int32, (1, rows), 1)
    cnt_r = (ri + 1).astype(F32)
    cnt_c = (ci + 1).astype(F32)
    for h in range(N_HEADS):
        log_gamma = _log_gamma(h)
        cols = slice(h * HEAD, (h + 1) * HEAD)
        gc = cnt_r * log_gamma
        decay = jnp.exp(jnp.where(ri >= ci, gc - cnt_c * log_gamma, -jnp.inf))
        qh = q_ref[:, cols]
        kh = k_ref[:, cols] * HEAD ** -0.5
        vh = v_ref[:, cols]
        state = st_ref[h]
        o = jnp.exp(gc) * _mm(qh, state) + _mm(_mm_nt(qh, kh) * decay, vh)
        gl = rows * log_gamma
        st_ref[h] = math.exp(gl) * state + _mm_tn(kh * jnp.exp(gl - gc), vh)
        o_ref[:, cols] = _layer_norm(o, gn_ref[...])

    @pl.when(r == pl.num_programs(1) - 1)
    def _finish():
        sout_ref[...] = st_ref[...]


def _ret_step_kernel(q_ref, k_ref, v_ref, gn_ref, s0_ref, o_ref, sout_ref):
    eye = _eye(HEAD)
    for h in range(N_HEADS):
        cols = slice(h * HEAD, (h + 1) * HEAD)
        k_col = _as_column(k_ref[:, cols] * HEAD ** -0.5, eye)
        state = math.exp(_log_gamma(h)) * s0_ref[h] + k_col * v_ref[:, cols]
        sout_ref[h] = state
        o = jnp.sum(state * _as_column(q_ref[:, cols], eye), axis=0, keepdims=True)
        o_ref[:, cols] = _layer_norm(o, gn_ref[...])


def _retention_step(q, k, v, gn, s0, seq_off):
    n_s = q.shape[0]
    row = pl.BlockSpec((None, 1, WIDTH), lambda b: (b, 0, 0))
    st = pl.BlockSpec((None, N_HEADS, HEAD, HEAD), lambda b: (b, 0, 0, 0))
    st_in = pl.BlockSpec((None, N_HEADS, HEAD, HEAD), lambda b: (b + seq_off, 0, 0, 0))
    r3 = lambda t: t.reshape(n_s, 1, WIDTH)
    o, s_new = pl.pallas_call(
        _ret_step_kernel,
        grid=(n_s,),
        in_specs=[row, row, row, pl.BlockSpec((1, HEAD), lambda b: (0, 0)), st_in],
        out_specs=[row, st],
        out_shape=[jax.ShapeDtypeStruct((n_s, 1, WIDTH), F32),
                   jax.ShapeDtypeStruct((n_s, N_HEADS, HEAD, HEAD), F32)],
        compiler_params=_params(1),
        name="retention_step",
    )(r3(q), r3(k), r3(v), gn.reshape(1, HEAD), s0)
    return o.reshape(n_s, WIDTH), s_new


def _retention(q, k, v, gn, s0, n_b, seq):
    rows = min(RET_ROWS, seq)
    assert seq % rows == 0
    nr = seq // rows
    tok = pl.BlockSpec((rows, WIDTH), lambda b, r: (b * nr + r, 0))
    st = pl.BlockSpec((None, N_HEADS, HEAD, HEAD), lambda b, r: (b, 0, 0, 0))
    return pl.pallas_call(
        _ret_kernel,
        grid=(n_b, nr),
        in_specs=[tok, tok, tok, pl.BlockSpec((1, HEAD), lambda b, r: (0, 0)), st],
        out_specs=[tok, st],
        out_shape=[jax.ShapeDtypeStruct((n_b * seq, WIDTH), F32),
                   jax.ShapeDtypeStruct((n_b, N_HEADS, HEAD, HEAD), F32)],
        scratch_shapes=[pltpu.VMEM((N_HEADS, HEAD, HEAD), F32)],
        compiler_params=_params(2),
        name="retention",
    )(q, k, v, gn.reshape(1, HEAD), s0)


def _s5_prep_kernel(are_ref, aim_ref, ldt_ref, brt_ref, bit_ref, lbr_ref, lbi_ref, bbr_ref, bbi_ref):
    dt = jnp.exp(ldt_ref[...])
    ar, ai = are_ref[...], aim_ref[...]
    mag = jnp.exp(ar * dt)
    ang = ai * dt
    lr, li = mag * jnp.cos(ang), mag * jnp.sin(ang)
    den = ar * ar + ai * ai
    fr = ((lr - 1.0) * ar + li * ai) / den
    fi = (li * ar - (lr - 1.0) * ai) / den
    lbr_ref[...] = lr
    lbi_ref[...] = li
    brt, bit = brt_ref[...], bit_ref[...]
    bbr_ref[...] = fr[:, None, :] * brt - fi[:, None, :] * bit
    bbi_ref[...] = fr[:, None, :] * bit + fi[:, None, :] * brt


def _s5_prep(a_re, a_im, log_dt, b_re, b_im):
    n_g, n_p = a_re.shape
    brt, bit = jnp.swapaxes(b_re, 1, 2), jnp.swapaxes(b_im, 1, 2)
    gp = jax.ShapeDtypeStruct((n_g, n_p), F32)
    gcp = jax.ShapeDtypeStruct(brt.shape, F32)
    return pl.pallas_call(_s5_prep_kernel, out_shape=[gp, gp, gcp, gcp], name="s5_prep")(
        a_re, a_im, log_dt.reshape(n_g, 1), brt, bit)


def _block_diag(blocks):
    n, r, c = blocks.shape
    return jnp.einsum("grc,gh->grhc", blocks, jnp.eye(n, dtype=blocks.dtype)).reshape(n * r, n * c)


def _s5_kernel(n_b, n_t, u_ref, x0r_ref, x0i_ref, lbr_ref, lbi_ref, bb_ref, cc_ref, d_ref,
               y_ref, xr_ref, xi_ref, utb_ref, x_ref, carry_ref):
    i = pl.program_id(0)
    n_p = lbr_ref.shape[1]
    half_in = S5_HALF_GROUPS * GROUP_C
    half_st = S5_HALF_GROUPS * P_C

    @pl.when(i == 0)
    def _init():
        carry_ref[:, :n_p] = x0r_ref[...]
        carry_ref[:, n_p:] = x0i_ref[...]

    n_lc = WIDTH // LANES
    if n_t > 1:
        for b in range(n_b):
            for c in range(n_lc):
                utb_ref[c, pl.ds(b, n_t, stride=n_b), :] = u_ref[:, b * WIDTH + c * LANES:b * WIDTH + (c + 1) * LANES]
        u = jnp.concatenate([utb_ref[c] for c in range(n_lc)], axis=1)
    else:
        u = u_ref[...]
    ub = u.astype(BF16)
    for half in range(2):
        for part in range(2):
            c0 = part * n_p + half * half_st
            x_ref[:, c0:c0 + half_st] = jnp.dot(ub[:, half * half_in:(half + 1) * half_in],
                                                bb_ref[half * 2 + part], preferred_element_type=F32)

    if n_t == 1:
        lr, li = lbr_ref[...], lbi_ref[...]
        x0r, x0i = carry_ref[:, :n_p], carry_ref[:, n_p:]
        xr = lr * x0r - li * x0i + x_ref[:, :n_p]
        xi = lr * x0i + li * x0r + x_ref[:, n_p:]
        x_ref[:, :n_p] = xr
        x_ref[:, n_p:] = xi
        carry_ref[:, :n_p] = xr
        carry_ref[:, n_p:] = xi
    else:
        for lg in range(n_p // S5_LANE_GROUP):
            l0 = lg * S5_LANE_GROUP
            lr = jnp.broadcast_to(lbr_ref[:, l0:l0 + S5_LANE_GROUP], (n_b, S5_LANE_GROUP))
            li = jnp.broadcast_to(lbi_ref[:, l0:l0 + S5_LANE_GROUP], (n_b, S5_LANE_GROUP))

            def body(t, carry, l0=l0, lr=lr, li=li):
                xr, xi = carry
                row = pl.multiple_of(t * n_b, n_b)
                nxr = lr * xr - li * xi + x_ref[pl.ds(row, n_b), l0:l0 + S5_LANE_GROUP]
                nxi = lr * xi + li * xr + x_ref[pl.ds(row, n_b), n_p + l0:n_p + l0 + S5_LANE_GROUP]
                x_ref[pl.ds(row, n_b), l0:l0 + S5_LANE_GROUP] = nxr
                x_ref[pl.ds(row, n_b), n_p + l0:n_p + l0 + S5_LANE_GROUP] = nxi
                return nxr, nxi

            xr, xi = lax.fori_loop(
                0, n_t, body,
                (carry_ref[:, l0:l0 + S5_LANE_GROUP], carry_ref[:, n_p + l0:n_p + l0 + S5_LANE_GROUP]),
                unroll=4)
            carry_ref[:, l0:l0 + S5_LANE_GROUP] = xr
            carry_ref[:, n_p + l0:n_p + l0 + S5_LANE_GROUP] = xi

    ys = []
    for half in range(2):
        xr_b = x_ref[:, half * half_st:(half + 1) * half_st].astype(BF16)
        xi_b = x_ref[:, n_p + half * half_st:n_p + (half + 1) * half_st].astype(BF16)
        ys.append(jnp.dot(xr_b, cc_ref[half * 2], preferred_element_type=F32)
                  + jnp.dot(xi_b, cc_ref[half * 2 + 1], preferred_element_type=F32))
    yg = _gelu_tanh(jnp.concatenate(ys, axis=1) + d_ref[...] * u)
    if n_t > 1:
        for c in range(n_lc):
            utb_ref[c] = yg[:, c * LANES:(c + 1) * LANES]
        for b in range(n_b):
            for c in range(n_lc):
                y_ref[:, b * WIDTH + c * LANES:b * WIDTH + (c + 1) * LANES] = utb_ref[c, pl.ds(b, n_t, stride=n_b), :]
    else:
        y_ref[...] = yg

    @pl.when(i == pl.num_programs(0) - 1)
    def _finish():
        xr_ref[...] = carry_ref[:, :n_p]
        xi_ref[...] = carry_ref[:, n_p:]


def _s5(u, x0_re, x0_im, lb_re, lb_im, bb4, cc4, d_skip, n_b, seq):
    n_p = lb_re.shape[1]
    n_t = min(S5_STEPS, seq)
    n_steps = seq // n_t
    rows = n_t * n_b
    u_block = (n_t, n_b * WIDTH) if seq > 1 else (n_b, WIDTH)
    const2 = lambda i: (0, 0)
    const3 = lambda i: (0, 0, 0)
    return pl.pallas_call(
        functools.partial(_s5_kernel, n_b, n_t),
        grid=(n_steps,),
        in_specs=[pl.BlockSpec(u_block, lambda i: (i, 0)),
                  pl.BlockSpec((n_b, n_p), const2), pl.BlockSpec((n_b, n_p), const2),
                  pl.BlockSpec((1, n_p), const2), pl.BlockSpec((1, n_p), const2),
                  pl.BlockSpec(bb4.shape, const3), pl.BlockSpec(cc4.shape, const3),
                  pl.BlockSpec((1, WIDTH), const2)],
        out_specs=[pl.BlockSpec(u_block, lambda i: (i, 0)),
                   pl.BlockSpec((n_b, n_p), const2), pl.BlockSpec((n_b, n_p), const2)],
        out_shape=[jax.ShapeDtypeStruct(u.shape, F32),
                   jax.ShapeDtypeStruct((n_b, n_p), F32), jax.ShapeDtypeStruct((n_b, n_p), F32)],
        scratch_shapes=[pltpu.VMEM((WIDTH // LANES, rows, LANES), F32), pltpu.VMEM((rows, 2 * n_p), F32),
                        pltpu.VMEM((n_b, 2 * n_p), F32)],
        compiler_params=_params(1),
        name="s5_scan",
    )(u, x0_re, x0_im, lb_re, lb_im, bb4, cc4, d_skip.reshape(1, WIDTH))


def _out_kernel(glu, a1_ref, g1_ref, a2_ref, g2_ref, x_ref, gate_ref, w_ref, wg_ref, o_ref):
    a1 = a1_ref[...]
    if glu:
        a1 = a1 * _sigmoid(jnp.dot(a1.astype(BF16), wg_ref[...], preferred_element_type=F32))
    y = (jnp.dot((a1 * g1_ref[...]).astype(BF16), w_ref[0:WIDTH, :], preferred_element_type=F32)
         + jnp.dot((a2_ref[...] * g2_ref[...]).astype(BF16), w_ref[WIDTH:2 * WIDTH, :],
                   preferred_element_type=F32))
    o_ref[...] = x_ref[...] + gate_ref[...] * y


def _out_proj(a1, g1, a2, g2, x3, gate, w, wg, glu, a1_by_time):
    n_g, rows, d = x3.shape
    tm = min(ROW_TILE, rows)
    nt = rows // tm
    r_mod = gate.shape[1]
    tok = pl.BlockSpec((tm, WIDTH), lambda gi, i: (gi * nt + i, 0))
    a1_spec = pl.BlockSpec((tm, WIDTH), lambda gi, i: (i, gi)) if a1_by_time else tok
    const = lambda gi, i: (0, 0)
    return pl.pallas_call(
        functools.partial(_out_kernel, glu),
        grid=(n_g, nt),
        in_specs=[a1_spec, tok, tok, tok,
                  pl.BlockSpec((None, tm, d), lambda gi, i: (gi, i, 0)),
                  pl.BlockSpec((None, r_mod, d), lambda gi, i: (gi, 0, 0)),
                  pl.BlockSpec(w.shape, const), pl.BlockSpec(wg.shape, const)],
        out_specs=pl.BlockSpec((None, tm, d), lambda gi, i: (gi, i, 0)),
        out_shape=jax.ShapeDtypeStruct(x3.shape, F32),
        compiler_params=_params(2),
        name="out_proj",
    )(a1, g1, a2, g2, x3, gate, w, wg)


def _pad_cols(w, n):
    return jnp.pad(w, ((0, 0), (0, n - w.shape[1])))


def _even_layer(li, x3, mods, norm_g, p, attn_fn, delta_fn):
    shift, scale, gate = mods
    n_g, rows, d = x3.shape
    lam_init = 0.8 - 0.6 * math.exp(-0.3 * li)
    n_cols = p["w_in"].shape[1]
    n_pad = -(-n_cols // LANES) * LANES
    w_in = _pad_cols(p["w_in"], n_pad).astype(BF16)
    segs = ((0, WIDTH, "qnorm", BF16, False), (WIDTH, WIDTH, "knorm", F32, False),
            (2 * WIDTH, WIDTH, "raw", F32, False), (3 * WIDTH, WIDTH, "silu", F32, False),
            (4 * WIDTH, 3 * WIDTH, "raw", F32, False), (7 * WIDTH, WIDTH, "silu", F32, False),
            (8 * WIDTH, LANES, "raw", F32, False))
    bd = _block_diag(jnp.full((WIDTH // DQK, DQK, DQK), 1.0 / DQK, F32)).astype(BF16)
    qg = jnp.tile(p["qn_g"], WIDTH // DQK).reshape(1, WIDTH)
    kg = jnp.tile(p["kn_g"], WIDTH // DQK).reshape(1, WIDTH)
    q, k, v, za, qkv_b, zb, ab = _in_proj(x3, shift, scale, norm_g, w_in, segs, bd, qg, kg)
    lamp = jnp.zeros((SUBLANES, LANES), F32)
    for i, name in enumerate(("lam_q1", "lam_k1", "lam_q2", "lam_k2")):
        lamp = lamp.at[i, :DQK].set(p[name])
    sg = p["subln_g"].reshape(1, HEAD)
    oa = attn_fn(q, k, v, lamp=lamp, sg=sg, lam_init=lam_init)
    ob, conv_new, s_new = delta_fn(qkv_b, ab, p["conv_w"], p["a_log"], p["dt_bias"], p["gn_b"])
    w_out = p["w_out"].astype(BF16)
    x_new = _out_proj(oa, za, ob, zb, x3, gate, w_out, jnp.zeros((SUBLANES, LANES), BF16), False, False)
    return x_new, (k, v, conv_new, s_new)


def _odd_layer(x3, mods, norm_g, p, s5_mats, x0_re, x0_im, r0, by_time, seq_off=0):
    shift, scale, gate = mods
    n_g, rows, d = x3.shape
    w_in = p["w_in"].astype(BF16)
    segs = ((0, WIDTH, "raw", F32, by_time), (WIDTH, WIDTH, "silu", F32, False),
            (2 * WIDTH, WIDTH, "raw", F32, False), (3 * WIDTH, WIDTH, "raw", F32, False),
            (4 * WIDTH, WIDTH, "raw", F32, False), (5 * WIDTH, WIDTH, "silu", F32, False))
    dummy = jnp.zeros((SUBLANES, LANES), BF16)
    dummy_g = jnp.zeros((1, LANES), F32)
    u, zc, qd, kd, vd, zd = _in_proj(x3, shift, scale, norm_g, w_in, segs, dummy, dummy_g, dummy_g)
    lb_re, lb_im, bb4, cc4 = s5_mats
    n_b = n_g if by_time else rows
    seq = rows if by_time else 1
    yg, xr, xi = _s5(u, x0_re, x0_im, lb_re, lb_im, bb4, cc4, p["s5_d"], n_b, seq)
    if by_time:
        od, r_new = _retention(qd, kd, vd, p["gn_d"], r0, n_g, rows)
    else:
        od, r_new = _retention_step(qd, kd, vd, p["gn_d"], r0, seq_off)
    x_new = _out_proj(yg, zc, od, zd, x3, gate, p["w_out"].astype(BF16), p["w_glu"].astype(BF16), True, by_time)
    return x_new, (xr, xi, r_new)


def kernel(x_prompt, x_sample, c_prompt, c_sample, page_table, cache_k, cache_v, state_b_conv, state_b_ssm,
           state_c_re, state_c_im, state_d_ret, norm_g, w_ada, b_ada, w_in_e, w_out_e, qn_g, kn_g,
           lam_q1, lam_k1, lam_q2, lam_k2, subln_g, conv_w, a_log, dt_bias, gn_b, w_in_o, w_out_o,
           s5_a_re, s5_a_im, s5_b_re, s5_b_im, s5_c_re, s5_c_im, s5_d, s5_log_dt, w_glu, gn_d):
    n_bp, seq, d = x_prompt.shape
    n_bs = x_sample.shape[0]
    depth = norm_g.shape[0]
    n_pages, page = page_table.shape[1], cache_k.shape[2]
    n_g, n_p = s5_a_re.shape[1], s5_a_re.shape[2]
    n_phys = cache_k.shape[1]
    ck_all = cache_k.reshape(-1, page, WIDTH)
    cv_all = cache_v.reshape(-1, page, WIDTH)
    conv_all = state_b_conv.reshape(-1, K_CONV - 1, 3 * WIDTH)
    ssm_all = state_b_ssm.reshape(-1, N_HEADS, HEAD, HEAD)
    ret_all = state_d_ret.reshape(-1, N_HEADS, HEAD, HEAD)

    mod = _modulation(jnp.concatenate([c_prompt, c_sample], axis=0), w_ada, b_ada)
    xp = x_prompt
    xs = x_sample.reshape(1, n_bs, d)
    outs = {name: [] for name in ("k_p", "v_p", "k_s", "v_s", "cv_p", "cv_s", "dl_p", "dl_s",
                                  "s5r_p", "s5i_p", "s5r_s", "s5i_s", "rt_p", "rt_s")}
    for li in range(depth):
        mods_p = tuple(mod[li, :n_bp, j * d:(j + 1) * d].reshape(n_bp, 1, d) for j in range(3))
        mods_s = tuple(mod[li, n_bp:, j * d:(j + 1) * d].reshape(1, n_bs, d) for j in range(3))
        if li % 2 == 0:
            e = li // 2
            p = dict(w_in=w_in_e[e], w_out=w_out_e[e], qn_g=qn_g[e], kn_g=kn_g[e], lam_q1=lam_q1[e],
                     lam_k1=lam_k1[e], lam_q2=lam_q2[e], lam_k2=lam_k2[e], subln_g=subln_g[e],
                     conv_w=conv_w[e], a_log=a_log[e], dt_bias=dt_bias[e], gn_b=gn_b[e])
            attn_p = functools.partial(_attn_prompt, n_b=n_bp, seq=seq)
            delta_p = functools.partial(
                _delta, cbuf=jnp.zeros((n_bp, K_CONV - 1, 3 * WIDTH), F32),
                s0=jnp.zeros((n_bp, N_HEADS, HEAD, HEAD), F32), n_b=n_bp, seq=seq)
            xp, (k_p, v_p, c_p, s_p) = _even_layer(li, xp, mods_p, norm_g[li], p, attn_p, delta_p)
            attn_s = functools.partial(_attn_decode, cache_k=ck_all, cache_v=cv_all, page_table=page_table,
                                       page_off=e * n_phys)
            delta_s = functools.partial(_delta_step, cbuf=conv_all, s0=ssm_all, seq_off=e * n_bs)
            xs, (k_s, v_s, c_s, s_s) = _even_layer(li, xs, mods_s, norm_g[li], p, attn_s, delta_s)
            outs["k_p"].append(k_p.reshape(n_bp, seq, N_HEADS, HEAD))
            outs["v_p"].append(v_p.reshape(n_bp, seq, N_HEADS, HEAD))
            outs["k_s"].append(k_s.reshape(n_bs, 1, N_HEADS, HEAD))
            outs["v_s"].append(v_s.reshape(n_bs, 1, N_HEADS, HEAD))
            outs["cv_p"].append(c_p)
            outs["cv_s"].append(c_s)
            outs["dl_p"].append(s_p)
            outs["dl_s"].append(s_s)
        else:
            o = li // 2
            p = dict(w_in=w_in_o[o], w_out=w_out_o[o], s5_d=s5_d[o], w_glu=w_glu[o], gn_d=gn_d[o])
            lb_re, lb_im, bbr, bbi = _s5_prep(s5_a_re[o], s5_a_im[o], s5_log_dt[o], s5_b_re[o], s5_b_im[o])
            hg = S5_HALF_GROUPS
            bb4 = jnp.stack([_block_diag(t[h * hg:(h + 1) * hg]) for h in range(2) for t in (bbr, bbi)]).astype(BF16)
            cre = jnp.swapaxes(s5_c_re[o], 1, 2)
            cim = -jnp.swapaxes(s5_c_im[o], 1, 2)
            cc4 = jnp.stack([_block_diag(t[h * hg:(h + 1) * hg]) for h in range(2) for t in (cre, cim)]).astype(BF16)
            s5_mats = (lb_re.reshape(1, n_g * n_p), lb_im.reshape(1, n_g * n_p), bb4, cc4)
            zeros_c = jnp.zeros((n_bp, n_g * n_p), F32)
            xp, (r_p, i_p, t_p) = _odd_layer(xp, mods_p, norm_g[li], p, s5_mats, zeros_c, zeros_c,
                                             jnp.zeros((n_bp, N_HEADS, HEAD, HEAD), F32), True)
            xs, (r_s, i_s, t_s) = _odd_layer(xs, mods_s, norm_g[li], p, s5_mats,
                                             state_c_re[o].reshape(n_bs, n_g * n_p),
                                             state_c_im[o].reshape(n_bs, n_g * n_p), ret_all, False, o * n_bs)
            outs["s5r_p"].append(r_p.reshape(n_bp, n_g, n_p))
            outs["s5i_p"].append(i_p.reshape(n_bp, n_g, n_p))
            outs["s5r_s"].append(r_s.reshape(n_bs, n_g, n_p))
            outs["s5i_s"].append(i_s.reshape(n_bs, n_g, n_p))
            outs["rt_p"].append(t_p)
            outs["rt_s"].append(t_s)
    st = lambda name: jnp.stack(outs[name])
    return (xp, xs.reshape(n_bs, 1, d), st("k_p"), st("v_p"), st("k_s"), st("v_s"), st("cv_p"), st("cv_s"),
            st("dl_p"), st("dl_s"), st("s5r_p"), st("s5i_p"), st("s5r_s"), st("s5i_s"), st("rt_p"), st("rt_s"))
```

```python
import functools
import math

import jax
import jax.numpy as jnp
from jax import lax
from jax.experimental import pallas as pl
from jax.experimental.pallas import tpu as pltpu

F32 = jnp.float32
BF16 = jnp.bfloat16
EPS = 1e-6

LANES = 128
SUBLANES = 8
VMEM_LIMIT_BYTES = 48 * 1024 * 1024

HEAD = 128
DQK = HEAD // 2
LOG2E = math.log2(math.e)
N_HEADS = 4
WIDTH = N_HEADS * HEAD
K_CONV = 4
DELTA_CHUNK = 64
GROUP_C = 16
P_C = 64
S5_HALF_GROUPS = 16

ROW_TILE = 256
ATTN_TILE = 512
ATTN_ROW_CHUNK = 256
DELTA_ROWS = 256
RET_ROWS = 256
S5_STEPS = 64
S5_LANE_GROUP = 512
STEP_SEQS = 4


def _params(n_axes, vmem=VMEM_LIMIT_BYTES):
    return pltpu.CompilerParams(dimension_semantics=("arbitrary",) * n_axes, vmem_limit_bytes=vmem)


def _sigmoid(x):
    return 1.0 / (1.0 + jnp.exp(-x))


def _silu(x):
    return x * _sigmoid(x)


def _softplus(x):
    return jnp.maximum(x, 0.0) + jnp.log1p(jnp.exp(-jnp.abs(x)))


def _gelu_tanh(x):
    return 0.5 * x * (1.0 + jnp.tanh(math.sqrt(2.0 / math.pi) * (x + 0.044715 * (x * x * x))))


def _mm(a, b):
    return jnp.dot(a.astype(BF16), b.astype(BF16), preferred_element_type=F32)


def _mm_nt(a, b):
    return lax.dot_general(a.astype(BF16), b.astype(BF16), (((1,), (1,)), ((), ())),
                           preferred_element_type=F32)


def _mod_kernel(c_ref, w_ref, b_ref, o_ref):
    o_ref[...] = _mm(_silu(c_ref[...]), w_ref[...]) + b_ref[...]


def _modulation(c_all, w_ada, b_ada):
    depth, d, n3 = w_ada.shape
    rows = c_all.shape[0]
    tn = d
    return pl.pallas_call(
        _mod_kernel,
        grid=(depth, n3 // tn),
        in_specs=[pl.BlockSpec((rows, d), lambda l, n: (0, 0)),
                  pl.BlockSpec((None, d, tn), lambda l, n: (l, 0, n)),
                  pl.BlockSpec((None, 1, tn), lambda l, n: (l, 0, n))],
        out_specs=pl.BlockSpec((None, rows, tn), lambda l, n: (l, 0, n)),
        out_shape=jax.ShapeDtypeStruct((depth, rows, n3), F32),
        compiler_params=_params(2),
        name="modulation",
    )(c_all, w_ada, b_ada.reshape(depth, 1, n3))


def _in_kernel(segs, x_ref, shift_ref, scale_ref, g_ref, w_ref, bd_ref, qg_ref, kg_ref, *refs):
    out_refs, h_ref = refs[:-1], refs[-1]
    x = x_ref[...]
    h = x * lax.rsqrt(jnp.mean(x * x, axis=-1, keepdims=True) + EPS) * g_ref[...]
    h_ref[...] = (h * (1.0 + scale_ref[...]) + shift_ref[...]).astype(BF16)
    for (c0, width, kind), o_ref in zip(segs, out_refs):
        acc = jnp.dot(h_ref[...], w_ref[:, c0:c0 + width], preferred_element_type=F32)
        if kind == "silu":
            acc = _silu(acc)
        elif kind in ("qnorm", "knorm"):
            sq = acc * acc
            hi = sq.astype(BF16)
            lo = (sq - hi.astype(F32)).astype(BF16)
            ms = (jnp.dot(hi, bd_ref[...], preferred_element_type=F32)
                  + jnp.dot(lo, bd_ref[...], preferred_element_type=F32))
            acc = acc * lax.rsqrt(ms + EPS) * (qg_ref if kind == "qnorm" else kg_ref)[...]
            if kind == "qnorm":
                acc = acc * (DQK ** -0.5 * LOG2E)
        o_ref[...] = acc.astype(o_ref.dtype)


def _in_proj(x3, shift, scale, g, w, segs, bd, qg, kg):
    n_g, rows, d = x3.shape
    tm = min(ROW_TILE, rows)
    nt = rows // tm
    r_mod = shift.shape[1]
    assert r_mod in (1, tm)
    out_shapes, out_specs = [], []
    for _, width, _, dtype, by_time in segs:
        if by_time:
            out_shapes.append(jax.ShapeDtypeStruct((rows, n_g * width), dtype))
            out_specs.append(pl.BlockSpec((tm, width), lambda gi, i: (i, gi)))
        else:
            out_shapes.append(jax.ShapeDtypeStruct((n_g * rows, width), dtype))
            out_specs.append(pl.BlockSpec((tm, width), lambda gi, i: (gi * nt + i, 0)))
    const = lambda gi, i: (0, 0)
    return pl.pallas_call(
        functools.partial(_in_kernel, tuple(s[:3] for s in segs)),
        grid=(n_g, nt),
        in_specs=[pl.BlockSpec((None, tm, d), lambda gi, i: (gi, i, 0)),
                  pl.BlockSpec((None, r_mod, d), lambda gi, i: (gi, 0, 0)),
                  pl.BlockSpec((None, r_mod, d), lambda gi, i: (gi, 0, 0)),
                  pl.BlockSpec((1, d), const),
                  pl.BlockSpec(w.shape, const),
                  pl.BlockSpec(bd.shape, const),
                  pl.BlockSpec(qg.shape, const),
                  pl.BlockSpec(kg.shape, const)],
        out_specs=out_specs,
        out_shape=out_shapes,
        scratch_shapes=[pltpu.VMEM((tm, d), BF16)],
        compiler_params=_params(2),
        name="in_proj",
    )(x3, shift, scale, g.reshape(1, d), w, bd, qg, kg)


def _lambda_value(lamp, lam_init):
    a = jnp.sum(lamp[0:1] * lamp[1:2], axis=-1, keepdims=True)
    b = jnp.sum(lamp[2:3] * lamp[3:4], axis=-1, keepdims=True)
    return jnp.exp(a) - jnp.exp(b) + lam_init


def _sub_ln(o, gain, lam_init):
    return o * lax.rsqrt(jnp.mean(o * o, axis=-1, keepdims=True) + EPS) * gain * (1.0 - lam_init)


def _attn_kernel(lam_init, row_chunk, qt_ref, kt_ref, q_ref, k_ref, v_ref, kbias_ref, lamp_ref, sg_ref, o_ref,
                 qs_ref, m_ref, l_ref, acc_ref):
    qi, ki = qt_ref[pl.program_id(2)], kt_ref[pl.program_id(2)]
    tq, tk = q_ref.shape[0], k_ref.shape[0]

    @pl.when(ki == 0)
    def _init():
        m_ref[...] = jnp.full(m_ref.shape, -jnp.inf, F32)
        l_ref[...] = jnp.zeros(l_ref.shape, F32)
        acc_ref[...] = jnp.zeros(acc_ref.shape, F32)
        q = q_ref[...]
        lane = lax.broadcasted_iota(jnp.int32, (1, HEAD), 1)
        qs_ref[0:tq, :] = jnp.where(lane < DQK, q, jnp.zeros_like(q))
        qs_ref[tq:2 * tq, :] = jnp.where(lane >= DQK, q, jnp.zeros_like(q))

    def step(masked):
        kb = k_ref[...].astype(BF16)
        vb = v_ref[...].astype(BF16)
        kbias = kbias_ref[...]
        for r0 in range(0, 2 * tq, row_chunk):
            rs = slice(r0, r0 + row_chunk)
            n_k = min(tk, r0 % tq + row_chunk) if masked else tk
            s = _mm_nt(qs_ref[rs, :], kb[:n_k]) + kbias[:, :n_k]
            if masked:
                q_row = lax.broadcasted_iota(jnp.int32, (row_chunk, n_k), 0) + (r0 % tq)
                col = lax.broadcasted_iota(jnp.int32, (row_chunk, n_k), 1)
                s = jnp.where(col <= q_row, s, -jnp.inf)
            m_prev = m_ref[rs, :]
            m_new = jnp.maximum(m_prev, jnp.max(s, axis=-1, keepdims=True))
            alpha = jnp.exp2(m_prev - m_new)
            p = jnp.exp2(s - m_new)
            l_ref[rs, :] = alpha * l_ref[rs, :] + jnp.sum(p, axis=-1, keepdims=True)
            acc_ref[rs, :] = alpha * acc_ref[rs, :] + jnp.dot(p.astype(BF16), vb[:n_k], preferred_element_type=F32)
            m_ref[rs, :] = m_new

    @pl.when(ki < qi)
    def _off_diagonal():
        step(False)

    @pl.when(ki == qi)
    def _diagonal():
        step(True)
        lam = _lambda_value(lamp_ref[...], lam_init)
        o = acc_ref[0:tq, :] / l_ref[0:tq, :] - lam * (acc_ref[tq:2 * tq, :] / l_ref[tq:2 * tq, :])
        o_ref[...] = _sub_ln(o, sg_ref[...], lam_init)


def _alibi_slopes(n_heads):
    return jnp.exp2(-8.0 * jnp.arange(1, n_heads + 1, dtype=F32) / n_heads)


def _attn_prompt(q, k, v, lamp, sg, lam_init, n_b, seq):
    t = min(ATTN_TILE, seq)
    nq = seq // t
    row_chunk = min(ATTN_ROW_CHUNK, t)
    kbias = (LOG2E * _alibi_slopes(N_HEADS)[:, None] * jnp.arange(seq, dtype=F32)[None, :]).reshape(N_HEADS, nq, 1, t)
    pairs = [(qi, ki) for qi in range(nq) for ki in range(qi + 1)]
    qt = jnp.asarray([p[0] for p in pairs], jnp.int32)
    kt = jnp.asarray([p[1] for p in pairs], jnp.int32)
    q_map = lambda b, h, t_, qt_, kt_: (b * nq + qt_[t_], h)
    kv_map = lambda b, h, t_, qt_, kt_: (b * nq + kt_[t_], h)
    const = lambda b, h, t_, qt_, kt_: (0, 0)
    grid_spec = pltpu.PrefetchScalarGridSpec(
        num_scalar_prefetch=2,
        grid=(n_b, N_HEADS, len(pairs)),
        in_specs=[pl.BlockSpec((t, HEAD), q_map),
                  pl.BlockSpec((t, HEAD), kv_map),
                  pl.BlockSpec((t, HEAD), kv_map),
                  pl.BlockSpec((None, None, 1, t), lambda b, h, t_, qt_, kt_: (h, kt_[t_], 0, 0)),
                  pl.BlockSpec(lamp.shape, const),
                  pl.BlockSpec((1, HEAD), const)],
        out_specs=pl.BlockSpec((t, HEAD), q_map),
        scratch_shapes=[pltpu.VMEM((2 * t, HEAD), BF16), pltpu.VMEM((2 * t, 1), F32),
                        pltpu.VMEM((2 * t, 1), F32), pltpu.VMEM((2 * t, HEAD), F32)],
    )
    return pl.pallas_call(
        functools.partial(_attn_kernel, lam_init, row_chunk),
        grid_spec=grid_spec,
        out_shape=jax.ShapeDtypeStruct((n_b * seq, WIDTH), F32),
        compiler_params=_params(3),
        name="attn_prompt",
    )(qt, kt, q, k, v, kbias, lamp, sg)


def _decode_kernel(lam_init, n_pages, pt_ref, q_ref, kn_ref, vn_ref, nsl_ref, lamp_ref, sg_ref, *refs):
    del pt_ref
    k_refs, v_refs, o_ref = refs[:n_pages], refs[n_pages:2 * n_pages], refs[2 * n_pages]
    cols_page = k_refs[0].shape[0]
    n_rows = 4 * N_HEADS
    row = lax.broadcasted_iota(jnp.int32, (n_rows, HEAD), 0)
    lane = lax.broadcasted_iota(jnp.int32, (n_rows, HEAD), 1)

    def per_row_head(x):
        out = jnp.zeros((n_rows, HEAD), F32)
        for h in range(N_HEADS):
            out = jnp.where((row % N_HEADS == h) & (row < 2 * N_HEADS), x[:, h * HEAD:(h + 1) * HEAD], out)
        return out

    qm = jnp.where(lane // DQK == row // N_HEADS, per_row_head(q_ref[...].astype(F32)), 0.0).astype(BF16)
    s = jnp.concatenate([_mm_nt(qm, k_refs[j][...]) for j in range(n_pages)], axis=1)
    n_cols = n_pages * cols_page
    past = n_cols // N_HEADS
    col = lax.broadcasted_iota(jnp.int32, (n_rows, n_cols), 1)
    row_c = lax.broadcasted_iota(jnp.int32, (n_rows, n_cols), 0)
    own_head = (col % N_HEADS == row_c % N_HEADS) & (row_c < 2 * N_HEADS)
    s = jnp.where(own_head, s + nsl_ref[...] * (past - col // N_HEADS).astype(F32), -jnp.inf)
    s_self = jnp.sum(qm.astype(F32) * per_row_head(kn_ref[...]).astype(BF16).astype(F32), axis=-1, keepdims=True)
    m = jnp.maximum(jnp.max(s, axis=-1, keepdims=True), s_self)
    p = jnp.exp2(s - m)
    p_self = jnp.exp2(s_self - m)
    denom = jnp.sum(p, axis=-1, keepdims=True) + p_self
    lam = _lambda_value(lamp_ref[...], lam_init)
    r1 = row[:, 0:1]
    coef = jnp.where(r1 < N_HEADS, 1.0, jnp.where(r1 < 2 * N_HEADS, -lam, 0.0)) / denom
    pw = p * coef
    acc = (p_self * coef) * per_row_head(vn_ref[...]).astype(BF16).astype(F32)
    for j in range(n_pages):
        acc = acc + _mm(pw[:, j * cols_page:(j + 1) * cols_page], v_refs[j][...])
    for h in range(N_HEADS):
        o = acc[h:h + 1, :] + acc[N_HEADS + h:N_HEADS + h + 1, :]
        o_ref[:, h * HEAD:(h + 1) * HEAD] = _sub_ln(o, sg_ref[...], lam_init)


def _attn_decode(q, k_new, v_new, cache_k, cache_v, page_table, lamp, sg, lam_init, layer):
    n_s, n_pages = page_table.shape
    n_l, n_phys, page = cache_k.shape[:3]
    cache_k = cache_k.reshape(n_l, n_phys, page * N_HEADS, HEAD)
    cache_v = cache_v.reshape(n_l, n_phys, page * N_HEADS, HEAD)
    nsl = -LOG2E * _alibi_slopes(N_HEADS)
    nsl8 = jnp.concatenate([nsl, nsl, jnp.zeros((2 * N_HEADS,), F32)]).reshape(4 * N_HEADS, 1)
    row_spec = pl.BlockSpec((None, 1, WIDTH), lambda b, pt: (b, 0, 0))
    const = lambda b, pt: (0, 0)

    def page_spec(j):
        return pl.BlockSpec((None, None, page * N_HEADS, HEAD), lambda b, pt: (layer, pt[b, j], 0, 0))

    grid_spec = pltpu.PrefetchScalarGridSpec(
        num_scalar_prefetch=1,
        grid=(n_s,),
        in_specs=[row_spec, row_spec, row_spec,
                  pl.BlockSpec(nsl8.shape, const), pl.BlockSpec(lamp.shape, const),
                  pl.BlockSpec((1, HEAD), const)]
                 + [page_spec(j) for j in range(n_pages)] * 2,
        out_specs=row_spec,
    )
    out = pl.pallas_call(
        functools.partial(_decode_kernel, lam_init, n_pages),
        grid_spec=grid_spec,
        out_shape=jax.ShapeDtypeStruct((n_s, 1, WIDTH), F32),
        compiler_params=_params(1),
        name="attn_decode",
    )(page_table, q.reshape(n_s, 1, WIDTH), k_new.reshape(n_s, 1, WIDTH), v_new.reshape(n_s, 1, WIDTH),
      nsl8, lamp, sg, *([cache_k] * n_pages), *([cache_v] * n_pages))
    return out.reshape(n_s, WIDTH)


INV_BASE = SUBLANES


def _unit_lower_inverse(a, block, row, col):
    in_base = row // INV_BASE == col // INV_BASE
    power = jnp.where(in_base, a, 0.0)
    inv = jnp.where(row == col, 1.0, 0.0) - power
    order = 2
    while order < INV_BASE:
        power = _mm(power, power)
        inv = inv + _mm(inv, power)
        order *= 2
    size = INV_BASE
    while size < block:
        coupling = jnp.where((row // (2 * size) == col // (2 * size)) & (row // size != col // size), a, 0.0)
        inv = inv - _mm(_mm(inv, coupling), inv)
        size *= 2
    return inv


def _mm_tn(a, b):
    return lax.dot_general(a.astype(BF16), b.astype(BF16), (((0,), (0,)), ((), ())),
                           preferred_element_type=F32)


def _delta_kernel(chunk, qkv_ref, ab_ref, cw_ref, hp_ref, gn_ref, cbuf_ref, s0_ref, tri_ref,
                  o_ref, cnew_ref, sout_ref, xs_ref, st_ref):
    r = pl.program_id(1)
    rows = qkv_ref.shape[0]
    n_chunks = rows // chunk
    tail = K_CONV - 1

    @pl.when(r == 0)
    def _init():
        st_ref[...] = s0_ref[...]
        xs_ref[0:SUBLANES, :] = jnp.zeros((SUBLANES, xs_ref.shape[1]), F32)
        xs_ref[SUBLANES - tail:SUBLANES, :] = cbuf_ref[...]

    xs_ref[SUBLANES:SUBLANES + rows, :] = qkv_ref[...]
    w = cw_ref[...]
    y = w[0:1] * xs_ref[SUBLANES - 3:SUBLANES - 3 + rows, :]
    for j in range(1, K_CONV):
        y = y + w[j:j + 1] * xs_ref[SUBLANES - 3 + j:SUBLANES - 3 + j + rows, :]
    new_tail = xs_ref[SUBLANES + rows - tail:SUBLANES + rows, :]
    cnew_ref[...] = new_tail
    xs_ref[SUBLANES - tail:SUBLANES, :] = new_tail
    act = _silu(y)

    ab = ab_ref[...]
    hp = hp_ref[...]
    g_all = -jnp.exp(hp[0:1]) * _softplus(ab + hp[1:2])
    beta_all = _sigmoid(ab)
    gam_all = jnp.dot(tri_ref[...], g_all, preferred_element_type=F32, precision=lax.Precision.HIGHEST)

    n_st = N_HEADS * chunk
    ri = lax.broadcasted_iota(jnp.int32, (n_st, n_st), 0)
    ci = lax.broadcasted_iota(jnp.int32, (n_st, n_st), 1)
    same_head = ri // chunk == ci // chunk
    incl, strict, diag = same_head & (ri >= ci), same_head & (ri > ci), ri == ci
    gn = gn_ref[...]
    heads = [slice(h * chunk, (h + 1) * chunk) for h in range(N_HEADS)]
    states = [st_ref[h] for h in range(N_HEADS)]

    for c in range(n_chunks):
        rs = slice(c * chunk, (c + 1) * chunk)

        def stacked(col0):
            return jnp.concatenate([act[rs, col0 + h * HEAD:col0 + (h + 1) * HEAD] for h in range(N_HEADS)], axis=0)

        q, k, v = stacked(0), stacked(WIDTH), stacked(2 * WIDTH)
        q = q * lax.rsqrt(jnp.sum(q * q, axis=-1, keepdims=True) + EPS) * HEAD ** -0.5
        k = k * lax.rsqrt(jnp.sum(k * k, axis=-1, keepdims=True) + EPS)
        gc = jnp.concatenate([gam_all[rs, h:h + 1] for h in range(N_HEADS)], axis=0)
        bc = jnp.concatenate([beta_all[rs, N_HEADS + h:N_HEADS + h + 1] for h in range(N_HEADS)], axis=0)
        gr = jnp.sum(jnp.where(diag, gc, 0.0), axis=0, keepdims=True)
        decay = jnp.exp(jnp.where(incl, gc - gr, -jnp.inf))
        inv = _unit_lower_inverse(jnp.where(strict, bc * decay * _mm_nt(k, k), 0.0), chunk, ri, ci)
        eg = jnp.exp(gc)
        sol = _mm(inv, jnp.concatenate([bc * v, (bc * eg) * k], axis=1))
        qk = _mm_nt(q, k) * decay
        u = jnp.concatenate([sol[hs, :HEAD] - _mm(sol[hs, HEAD:], states[h]) for h, hs in enumerate(heads)], axis=0)
        o = eg * jnp.concatenate([_mm(q[hs], states[h]) for h, hs in enumerate(heads)], axis=0) + _mm(qk, u)
        for h, hs in enumerate(heads):
            gl = gc[(h + 1) * chunk - 1:(h + 1) * chunk, :]
            states[h] = jnp.exp(gl) * states[h] + _mm_tn(k[hs] * jnp.exp(gl - gc[hs]), u[hs])
        on = o * lax.rsqrt(jnp.mean(o * o, axis=-1, keepdims=True) + EPS) * gn
        for h, hs in enumerate(heads):
            o_ref[rs, h * HEAD:(h + 1) * HEAD] = on[hs]

    for h in range(N_HEADS):
        st_ref[h] = states[h]

    @pl.when(r == pl.num_programs(1) - 1)
    def _finish():
        sout_ref[...] = st_ref[...]


def _chunk_tri(rows, chunk):
    r = jnp.arange(rows)
    return ((r[:, None] >= r[None, :]) & (r[:, None] // chunk == r[None, :] // chunk)).astype(F32)


def _head_params(a_log, dt_bias):
    return jnp.zeros((SUBLANES, LANES), F32).at[0, :N_HEADS].set(a_log).at[1, :N_HEADS].set(dt_bias)


def _delta(qkv, ab, conv_w, a_log, dt_bias, gn, cbuf, s0, n_b, seq):
    chunk = min(DELTA_CHUNK, seq)
    rows = min(DELTA_ROWS, seq)
    assert seq % rows == 0 and rows % chunk == 0
    nr = seq // rows
    width3 = 3 * WIDTH
    hp = _head_params(a_log, dt_bias)
    tri = _chunk_tri(rows, chunk)
    const = lambda b, r: (0, 0)
    return pl.pallas_call(
        functools.partial(_delta_kernel, chunk),
        grid=(n_b, nr),
        in_specs=[pl.BlockSpec((rows, width3), lambda b, r: (b * nr + r, 0)),
                  pl.BlockSpec((rows, LANES), lambda b, r: (b * nr + r, 0)),
                  pl.BlockSpec((K_CONV, width3), const),
                  pl.BlockSpec((SUBLANES, LANES), const),
                  pl.BlockSpec((1, HEAD), const),
                  pl.BlockSpec((None, K_CONV - 1, width3), lambda b, r: (b, 0, 0)),
                  pl.BlockSpec((None, N_HEADS, HEAD, HEAD), lambda b, r: (b, 0, 0, 0)),
                  pl.BlockSpec((rows, rows), const)],
        out_specs=[pl.BlockSpec((rows, WIDTH), lambda b, r: (b * nr + r, 0)),
                   pl.BlockSpec((None, K_CONV - 1, width3), lambda b, r: (b, 0, 0)),
                   pl.BlockSpec((None, N_HEADS, HEAD, HEAD), lambda b, r: (b, 0, 0, 0))],
        out_shape=[jax.ShapeDtypeStruct((n_b * seq, WIDTH), F32),
                   jax.ShapeDtypeStruct((n_b, K_CONV - 1, width3), F32),
                   jax.ShapeDtypeStruct((n_b, N_HEADS, HEAD, HEAD), F32)],
        scratch_shapes=[pltpu.VMEM((rows + SUBLANES, width3), F32),
                        pltpu.VMEM((N_HEADS, HEAD, HEAD), F32)],
        compiler_params=_params(2),
        name="delta_rule",
    )(qkv, ab, conv_w, hp, gn.reshape(1, HEAD), cbuf, s0, tri)


def _as_column(row_vec, eye):
    return jnp.sum(jnp.where(eye, row_vec, 0.0), axis=1, keepdims=True)


def _eye(n):
    return lax.broadcasted_iota(jnp.int32, (n, n), 0) == lax.broadcasted_iota(jnp.int32, (n, n), 1)


def _delta_step_kernel(x_ref, ab_ref, cw_ref, hp_ref, gn_ref, cbuf_ref, s0_ref, o_ref, cnew_ref, sout_ref):
    w = cw_ref[...]
    hp = hp_ref[...]
    eye = _eye(HEAD)
    for i in range(x_ref.shape[0]):
        x = x_ref[i]
        buf = cbuf_ref[i]
        y = w[0:1] * buf[0:1] + w[1:2] * buf[1:2] + w[2:3] * buf[2:3] + w[3:4] * x
        cnew_ref[i, 0:2, :] = buf[1:3]
        cnew_ref[i, 2:3, :] = x
        act = _silu(y)
        ab = ab_ref[i]
        decay_all = jnp.exp(-jnp.exp(hp[0:1]) * _softplus(ab + hp[1:2]))
        beta_all = _sigmoid(ab)
        for h in range(N_HEADS):
            qh = act[:, h * HEAD:(h + 1) * HEAD]
            kh = act[:, WIDTH + h * HEAD:WIDTH + (h + 1) * HEAD]
            vh = act[:, 2 * WIDTH + h * HEAD:2 * WIDTH + (h + 1) * HEAD]
            qh = qh * lax.rsqrt(jnp.sum(qh * qh, axis=-1, keepdims=True) + EPS) * HEAD ** -0.5
            kh = kh * lax.rsqrt(jnp.sum(kh * kh, axis=-1, keepdims=True) + EPS)
            a = decay_all[:, h:h + 1]
            beta = beta_all[:, N_HEADS + h:N_HEADS + h + 1]
            state = s0_ref[i, h]
            k_col = _as_column(kh, eye)
            u = beta * (vh - a * jnp.sum(state * k_col, axis=0, keepdims=True))
            state = a * state + k_col * u
            sout_ref[i, h] = state
            o = jnp.sum(state * _as_column(qh, eye), axis=0, keepdims=True)
            o_ref[i, :, h * HEAD:(h + 1) * HEAD] = (
                o * lax.rsqrt(jnp.mean(o * o, axis=-1, keepdims=True) + EPS) * gn_ref[...])


def _step_block(n_s, seq_off):
    return math.gcd(STEP_SEQS, math.gcd(n_s, seq_off))


def _delta_step(qkv, ab, conv_w, a_log, dt_bias, gn, cbuf, s0, seq_off):
    n_s = qkv.shape[0]
    nb = _step_block(n_s, seq_off)
    off = seq_off // nb
    width3 = 3 * WIDTH
    const = lambda b: (0, 0)
    row = lambda width: pl.BlockSpec((nb, 1, width), lambda b: (b, 0, 0))
    tail = pl.BlockSpec((nb, K_CONV - 1, width3), lambda b: (b, 0, 0))
    st = pl.BlockSpec((nb, N_HEADS, HEAD, HEAD), lambda b: (b, 0, 0, 0))
    tail_in = pl.BlockSpec((nb, K_CONV - 1, width3), lambda b: (b + off, 0, 0))
    st_in = pl.BlockSpec((nb, N_HEADS, HEAD, HEAD), lambda b: (b + off, 0, 0, 0))
    o, cnew, s_new = pl.pallas_call(
        _delta_step_kernel,
        grid=(n_s // nb,),
        in_specs=[row(width3), row(LANES), pl.BlockSpec((K_CONV, width3), const),
                  pl.BlockSpec((SUBLANES, LANES), const), pl.BlockSpec((1, HEAD), const), tail_in, st_in],
        out_specs=[row(WIDTH), tail, st],
        out_shape=[jax.ShapeDtypeStruct((n_s, 1, WIDTH), F32),
                   jax.ShapeDtypeStruct((n_s, K_CONV - 1, width3), F32),
                   jax.ShapeDtypeStruct((n_s, N_HEADS, HEAD, HEAD), F32)],
        compiler_params=_params(1),
        name="delta_step",
    )(qkv.reshape(n_s, 1, width3), ab.reshape(n_s, 1, LANES), conv_w, _head_params(a_log, dt_bias),
      gn.reshape(1, HEAD), cbuf, s0)
    return o.reshape(n_s, WIDTH), cnew, s_new


def _log_gamma(h):
    return math.log1p(-(2.0 ** (-5.0 - h)))


def _layer_norm(o, gain):
    oc = o - jnp.mean(o, axis=-1, keepdims=True)
    return oc * lax.rsqrt(jnp.mean(oc * oc, axis=-1, keepdims=True) + EPS) * gain


def _ret_kernel(q_ref, k_ref, v_ref, gn_ref, s0_ref, o_ref, sout_ref, st_ref):
    r = pl.program_id(1)
    rows = q_ref.shape[0]

    @pl.when(r == 0)
    def _init():
        st_ref[...] = s0_ref[...]

    ri = lax.broadcasted_iota(jnp.int32, (rows, 1), 0)
    ci = lax.broadcasted_iota(jnp.int32, (1, rows), 1)
    cnt_r = (ri + 1).astype(F32)
    cnt_c = (ci + 1).astype(F32)
    for h in range(N_HEADS):
        log_gamma = _log_gamma(h)
        cols = slice(h * HEAD, (h + 1) * HEAD)
        gc = cnt_r * log_gamma
        decay = jnp.exp(jnp.where(ri >= ci, gc - cnt_c * log_gamma, -jnp.inf))
        qh = q_ref[:, cols]
        kh = k_ref[:, cols] * HEAD ** -0.5
        vh = v_ref[:, cols]
        state = st_ref[h]
        o = jnp.exp(gc) * _mm(qh, state) + _mm(_mm_nt(qh, kh) * decay, vh)
        gl = rows * log_gamma
        st_ref[h] = math.exp(gl) * state + _mm_tn(kh * jnp.exp(gl - gc), vh)
        o_ref[:, cols] = _layer_norm(o, gn_ref[...])

    @pl.when(r == pl.num_programs(1) - 1)
    def _finish():
        sout_ref[...] = st_ref[...]


def _ret_step_kernel(q_ref, k_ref, v_ref, gn_ref, s0_ref, o_ref, sout_ref):
    eye = _eye(HEAD)
    for i in range(q_ref.shape[0]):
        for h in range(N_HEADS):
            cols = slice(h * HEAD, (h + 1) * HEAD)
            k_col = _as_column(k_ref[i, :, cols] * HEAD ** -0.5, eye)
            state = math.exp(_log_gamma(h)) * s0_ref[i, h] + k_col * v_ref[i, :, cols]
            sout_ref[i, h] = state
            o = jnp.sum(state * _as_column(q_ref[i, :, cols], eye), axis=0, keepdims=True)
            o_ref[i, :, cols] = _layer_norm(o, gn_ref[...])


def _retention_step(q, k, v, gn, s0, seq_off):
    n_s = q.shape[0]
    nb = _step_block(n_s, seq_off)
    off = seq_off // nb
    row = pl.BlockSpec((nb, 1, WIDTH), lambda b: (b, 0, 0))
    st = pl.BlockSpec((nb, N_HEADS, HEAD, HEAD), lambda b: (b, 0, 0, 0))
    st_in = pl.BlockSpec((nb, N_HEADS, HEAD, HEAD), lambda b: (b + off, 0, 0, 0))
    r3 = lambda t: t.reshape(n_s, 1, WIDTH)
    o, s_new = pl.pallas_call(
        _ret_step_kernel,
        grid=(n_s // nb,),
        in_specs=[row, row, row, pl.BlockSpec((1, HEAD), lambda b: (0, 0)), st_in],
        out_specs=[row, st],
        out_shape=[jax.ShapeDtypeStruct((n_s, 1, WIDTH), F32),
                   jax.ShapeDtypeStruct((n_s, N_HEADS, HEAD, HEAD), F32)],
        compiler_params=_params(1),
        name="retention_step",
    )(r3(q), r3(k), r3(v), gn.reshape(1, HEAD), s0)
    return o.reshape(n_s, WIDTH), s_new


def _retention(q, k, v, gn, s0, n_b, seq):
    rows = min(RET_ROWS, seq)
    assert seq % rows == 0
    nr = seq // rows
    tok = pl.BlockSpec((rows, WIDTH), lambda b, r: (b * nr + r, 0))
    st = pl.BlockSpec((None, N_HEADS, HEAD, HEAD), lambda b, r: (b, 0, 0, 0))
    return pl.pallas_call(
        _ret_kernel,
        grid=(n_b, nr),
        in_specs=[tok, tok, tok, pl.BlockSpec((1, HEAD), lambda b, r: (0, 0)), st],
        out_specs=[tok, st],
        out_shape=[jax.ShapeDtypeStruct((n_b * seq, WIDTH), F32),
                   jax.ShapeDtypeStruct((n_b, N_HEADS, HEAD, HEAD), F32)],
        scratch_shapes=[pltpu.VMEM((N_HEADS, HEAD, HEAD), F32)],
        compiler_params=_params(2),
        name="retention",
    )(q, k, v, gn.reshape(1, HEAD), s0)


def _s5_prep_kernel(are_ref, aim_ref, ldt_ref, brt_ref, bit_ref, lbr_ref, lbi_ref, bbr_ref, bbi_ref):
    dt = jnp.exp(ldt_ref[...])
    ar, ai = are_ref[...], aim_ref[...]
    mag = jnp.exp(ar * dt)
    ang = ai * dt
    lr, li = mag * jnp.cos(ang), mag * jnp.sin(ang)
    den = ar * ar + ai * ai
    fr = ((lr - 1.0) * ar + li * ai) / den
    fi = (li * ar - (lr - 1.0) * ai) / den
    lbr_ref[...] = lr
    lbi_ref[...] = li
    brt, bit = brt_ref[...], bit_ref[...]
    bbr_ref[...] = fr[:, None, :] * brt - fi[:, None, :] * bit
    bbi_ref[...] = fr[:, None, :] * bit + fi[:, None, :] * brt


def _s5_prep(a_re, a_im, log_dt, b_re, b_im):
    n_g, n_p = a_re.shape
    brt, bit = jnp.swapaxes(b_re, 1, 2), jnp.swapaxes(b_im, 1, 2)
    gp = jax.ShapeDtypeStruct((n_g, n_p), F32)
    gcp = jax.ShapeDtypeStruct(brt.shape, F32)
    return pl.pallas_call(_s5_prep_kernel, out_shape=[gp, gp, gcp, gcp], name="s5_prep")(
        a_re, a_im, log_dt.reshape(n_g, 1), brt, bit)


def _block_diag(blocks):
    n, r, c = blocks.shape
    return jnp.einsum("grc,gh->grhc", blocks, jnp.eye(n, dtype=blocks.dtype)).reshape(n * r, n * c)


def _s5_kernel(n_b, n_t, u_ref, x0r_ref, x0i_ref, lbr_ref, lbi_ref, bb_ref, cc_ref, d_ref,
               y_ref, xr_ref, xi_ref, utb_ref, x_ref, carry_ref):
    i = pl.program_id(0)
    n_p = lbr_ref.shape[1]
    half_in = S5_HALF_GROUPS * GROUP_C
    half_st = S5_HALF_GROUPS * P_C

    @pl.when(i == 0)
    def _init():
        carry_ref[:, :n_p] = x0r_ref[...]
        carry_ref[:, n_p:] = x0i_ref[...]

    n_lc = WIDTH // LANES
    if n_t > 1:
        for b in range(n_b):
            for c in range(n_lc):
                utb_ref[c, pl.ds(b, n_t, stride=n_b), :] = u_ref[:, b * WIDTH + c * LANES:b * WIDTH + (c + 1) * LANES]
        u = jnp.concatenate([utb_ref[c] for c in range(n_lc)], axis=1)
    else:
        u = u_ref[...]
    ub = u.astype(BF16)
    for half in range(2):
        for part in range(2):
            c0 = part * n_p + half * half_st
            x_ref[:, c0:c0 + half_st] = jnp.dot(ub[:, half * half_in:(half + 1) * half_in],
                                                bb_ref[half * 2 + part], preferred_element_type=F32)

    if n_t == 1:
        lr, li = lbr_ref[...], lbi_ref[...]
        x0r, x0i = carry_ref[:, :n_p], carry_ref[:, n_p:]
        xr = lr * x0r - li * x0i + x_ref[:, :n_p]
        xi = lr * x0i + li * x0r + x_ref[:, n_p:]
        x_ref[:, :n_p] = xr
        x_ref[:, n_p:] = xi
        carry_ref[:, :n_p] = xr
        carry_ref[:, n_p:] = xi
    else:
        for lg in range(n_p // S5_LANE_GROUP):
            l0 = lg * S5_LANE_GROUP
            lr = jnp.broadcast_to(lbr_ref[:, l0:l0 + S5_LANE_GROUP], (n_b, S5_LANE_GROUP))
            li = jnp.broadcast_to(lbi_ref[:, l0:l0 + S5_LANE_GROUP], (n_b, S5_LANE_GROUP))

            def body(t, carry, l0=l0, lr=lr, li=li):
                xr, xi = carry
                row = pl.multiple_of(t * n_b, n_b)
                nxr = lr * xr - li * xi + x_ref[pl.ds(row, n_b), l0:l0 + S5_LANE_GROUP]
                nxi = lr * xi + li * xr + x_ref[pl.ds(row, n_b), n_p + l0:n_p + l0 + S5_LANE_GROUP]
                x_ref[pl.ds(row, n_b), l0:l0 + S5_LANE_GROUP] = nxr
                x_ref[pl.ds(row, n_b), n_p + l0:n_p + l0 + S5_LANE_GROUP] = nxi
                return nxr, nxi

            xr, xi = lax.fori_loop(
                0, n_t, body,
                (carry_ref[:, l0:l0 + S5_LANE_GROUP], carry_ref[:, n_p + l0:n_p + l0 + S5_LANE_GROUP]),
                unroll=4)
            carry_ref[:, l0:l0 + S5_LANE_GROUP] = xr
            carry_ref[:, n_p + l0:n_p + l0 + S5_LANE_GROUP] = xi

    ys = []
    for half in range(2):
        xr_b = x_ref[:, half * half_st:(half + 1) * half_st].astype(BF16)
        xi_b = x_ref[:, n_p + half * half_st:n_p + (half + 1) * half_st].astype(BF16)
        ys.append(jnp.dot(xr_b, cc_ref[half * 2], preferred_element_type=F32)
                  + jnp.dot(xi_b, cc_ref[half * 2 + 1], preferred_element_type=F32))
    yg = _gelu_tanh(jnp.concatenate(ys, axis=1) + d_ref[...] * u)
    if n_t > 1:
        for c in range(n_lc):
            utb_ref[c] = yg[:, c * LANES:(c + 1) * LANES]
        for b in range(n_b):
            for c in range(n_lc):
                y_ref[:, b * WIDTH + c * LANES:b * WIDTH + (c + 1) * LANES] = utb_ref[c, pl.ds(b, n_t, stride=n_b), :]
    else:
        y_ref[...] = yg

    @pl.when(i == pl.num_programs(0) - 1)
    def _finish():
        xr_ref[...] = carry_ref[:, :n_p]
        xi_ref[...] = carry_ref[:, n_p:]


def _s5(u, x0_re, x0_im, lb_re, lb_im, bb4, cc4, d_skip, n_b, seq):
    n_p = lb_re.shape[1]
    n_t = min(S5_STEPS, seq)
    n_steps = seq // n_t
    rows = n_t * n_b
    u_block = (n_t, n_b * WIDTH) if seq > 1 else (n_b, WIDTH)
    const2 = lambda i: (0, 0)
    const3 = lambda i: (0, 0, 0)
    return pl.pallas_call(
        functools.partial(_s5_kernel, n_b, n_t),
        grid=(n_steps,),
        in_specs=[pl.BlockSpec(u_block, lambda i: (i, 0)),
                  pl.BlockSpec((n_b, n_p), const2), pl.BlockSpec((n_b, n_p), const2),
                  pl.BlockSpec((1, n_p), const2), pl.BlockSpec((1, n_p), const2),
                  pl.BlockSpec(bb4.shape, const3), pl.BlockSpec(cc4.shape, const3),
                  pl.BlockSpec((1, WIDTH), const2)],
        out_specs=[pl.BlockSpec(u_block, lambda i: (i, 0)),
                   pl.BlockSpec((n_b, n_p), const2), pl.BlockSpec((n_b, n_p), const2)],
        out_shape=[jax.ShapeDtypeStruct(u.shape, F32),
                   jax.ShapeDtypeStruct((n_b, n_p), F32), jax.ShapeDtypeStruct((n_b, n_p), F32)],
        scratch_shapes=[pltpu.VMEM((WIDTH // LANES, rows, LANES), F32), pltpu.VMEM((rows, 2 * n_p), F32),
                        pltpu.VMEM((n_b, 2 * n_p), F32)],
        compiler_params=_params(1),
        name="s5_scan",
    )(u, x0_re, x0_im, lb_re, lb_im, bb4, cc4, d_skip.reshape(1, WIDTH))


def _out_kernel(glu, a1_ref, g1_ref, a2_ref, g2_ref, x_ref, gate_ref, w_ref, wg_ref, o_ref):
    a1 = a1_ref[...]
    if glu:
        a1 = a1 * _sigmoid(jnp.dot(a1.astype(BF16), wg_ref[...], preferred_element_type=F32))
    y = (jnp.dot((a1 * g1_ref[...]).astype(BF16), w_ref[0:WIDTH, :], preferred_element_type=F32)
         + jnp.dot((a2_ref[...] * g2_ref[...]).astype(BF16), w_ref[WIDTH:2 * WIDTH, :],
                   preferred_element_type=F32))
    o_ref[...] = x_ref[...] + gate_ref[...] * y


def _out_proj(a1, g1, a2, g2, x3, gate, w, wg, glu, a1_by_time):
    n_g, rows, d = x3.shape
    tm = min(ROW_TILE, rows)
    nt = rows // tm
    r_mod = gate.shape[1]
    tok = pl.BlockSpec((tm, WIDTH), lambda gi, i: (gi * nt + i, 0))
    a1_spec = pl.BlockSpec((tm, WIDTH), lambda gi, i: (i, gi)) if a1_by_time else tok
    const = lambda gi, i: (0, 0)
    return pl.pallas_call(
        functools.partial(_out_kernel, glu),
        grid=(n_g, nt),
        in_specs=[a1_spec, tok, tok, tok,
                  pl.BlockSpec((None, tm, d), lambda gi, i: (gi, i, 0)),
                  pl.BlockSpec((None, r_mod, d), lambda gi, i: (gi, 0, 0)),
                  pl.BlockSpec(w.shape, const), pl.BlockSpec(wg.shape, const)],
        out_specs=pl.BlockSpec((None, tm, d), lambda gi, i: (gi, i, 0)),
        out_shape=jax.ShapeDtypeStruct(x3.shape, F32),
        compiler_params=_params(2),
        name="out_proj",
    )(a1, g1, a2, g2, x3, gate, w, wg)


def _pad_cols(w, n):
    return jnp.pad(w, ((0, 0), (0, n - w.shape[1])))


def _even_layer(li, x3, mods, norm_g, p, attn_fn, delta_fn):
    shift, scale, gate = mods
    n_g, rows, d = x3.shape
    lam_init = 0.8 - 0.6 * math.exp(-0.3 * li)
    n_cols = p["w_in"].shape[1]
    n_pad = -(-n_cols // LANES) * LANES
    w_in = _pad_cols(p["w_in"], n_pad).astype(BF16)
    segs = ((0, WIDTH, "qnorm", BF16, False), (WIDTH, WIDTH, "knorm", F32, False),
            (2 * WIDTH, WIDTH, "raw", F32, False), (3 * WIDTH, WIDTH, "silu", F32, False),
            (4 * WIDTH, 3 * WIDTH, "raw", F32, False), (7 * WIDTH, WIDTH, "silu", F32, False),
            (8 * WIDTH, LANES, "raw", F32, False))
    bd = _block_diag(jnp.full((WIDTH // DQK, DQK, DQK), 1.0 / DQK, F32)).astype(BF16)
    qg = jnp.tile(p["qn_g"], WIDTH // DQK).reshape(1, WIDTH)
    kg = jnp.tile(p["kn_g"], WIDTH // DQK).reshape(1, WIDTH)
    q, k, v, za, qkv_b, zb, ab = _in_proj(x3, shift, scale, norm_g, w_in, segs, bd, qg, kg)
    lamp = jnp.zeros((SUBLANES, LANES), F32)
    for i, name in enumerate(("lam_q1", "lam_k1", "lam_q2", "lam_k2")):
        lamp = lamp.at[i, :DQK].set(p[name])
    sg = p["subln_g"].reshape(1, HEAD)
    oa = attn_fn(q, k, v, lamp=lamp, sg=sg, lam_init=lam_init)
    ob, conv_new, s_new = delta_fn(qkv_b, ab, p["conv_w"], p["a_log"], p["dt_bias"], p["gn_b"])
    w_out = p["w_out"].astype(BF16)
    x_new = _out_proj(oa, za, ob, zb, x3, gate, w_out, jnp.zeros((SUBLANES, LANES), BF16), False, False)
    return x_new, (k, v, conv_new, s_new)


def _odd_layer(x3, mods, norm_g, p, s5_mats, x0_re, x0_im, r0, by_time, seq_off=0):
    shift, scale, gate = mods
    n_g, rows, d = x3.shape
    w_in = p["w_in"].astype(BF16)
    segs = ((0, WIDTH, "raw", F32, by_time), (WIDTH, WIDTH, "silu", F32, False),
            (2 * WIDTH, WIDTH, "raw", F32, False), (3 * WIDTH, WIDTH, "raw", F32, False),
            (4 * WIDTH, WIDTH, "raw", F32, False), (5 * WIDTH, WIDTH, "silu", F32, False))
    dummy = jnp.zeros((SUBLANES, LANES), BF16)
    dummy_g = jnp.zeros((1, LANES), F32)
    u, zc, qd, kd, vd, zd = _in_proj(x3, shift, scale, norm_g, w_in, segs, dummy, dummy_g, dummy_g)
    lb_re, lb_im, bb4, cc4 = s5_mats
    n_b = n_g if by_time else rows
    seq = rows if by_time else 1
    yg, xr, xi = _s5(u, x0_re, x0_im, lb_re, lb_im, bb4, cc4, p["s5_d"], n_b, seq)
    if by_time:
        od, r_new = _retention(qd, kd, vd, p["gn_d"], r0, n_g, rows)
    else:
        od, r_new = _retention_step(qd, kd, vd, p["gn_d"], r0, seq_off)
    x_new = _out_proj(yg, zc, od, zd, x3, gate, p["w_out"].astype(BF16), p["w_glu"].astype(BF16), True, by_time)
    return x_new, (xr, xi, r_new)


def kernel(x_prompt, x_sample, c_prompt, c_sample, page_table, cache_k, cache_v, state_b_conv, state_b_ssm,
           state_c_re, state_c_im, state_d_ret, norm_g, w_ada, b_ada, w_in_e, w_out_e, qn_g, kn_g,
           lam_q1, lam_k1, lam_q2, lam_k2, subln_g, conv_w, a_log, dt_bias, gn_b, w_in_o, w_out_o,
           s5_a_re, s5_a_im, s5_b_re, s5_b_im, s5_c_re, s5_c_im, s5_d, s5_log_dt, w_glu, gn_d):
    n_bp, seq, d = x_prompt.shape
    n_bs = x_sample.shape[0]
    depth = norm_g.shape[0]
    n_pages, page = page_table.shape[1], cache_k.shape[2]
    n_g, n_p = s5_a_re.shape[1], s5_a_re.shape[2]
    conv_all = state_b_conv.reshape(-1, K_CONV - 1, 3 * WIDTH)
    ssm_all = state_b_ssm.reshape(-1, N_HEADS, HEAD, HEAD)
    ret_all = state_d_ret.reshape(-1, N_HEADS, HEAD, HEAD)

    mod = _modulation(jnp.concatenate([c_prompt, c_sample], axis=0), w_ada, b_ada)
    xp = x_prompt
    xs = x_sample.reshape(1, n_bs, d)
    outs = {name: [] for name in ("k_p", "v_p", "k_s", "v_s", "cv_p", "cv_s", "dl_p", "dl_s",
                                  "s5r_p", "s5i_p", "s5r_s", "s5i_s", "rt_p", "rt_s")}
    for li in range(depth):
        mods_p = tuple(mod[li, :n_bp, j * d:(j + 1) * d].reshape(n_bp, 1, d) for j in range(3))
        mods_s = tuple(mod[li, n_bp:, j * d:(j + 1) * d].reshape(1, n_bs, d) for j in range(3))
        if li % 2 == 0:
            e = li // 2
            p = dict(w_in=w_in_e[e], w_out=w_out_e[e], qn_g=qn_g[e], kn_g=kn_g[e], lam_q1=lam_q1[e],
                     lam_k1=lam_k1[e], lam_q2=lam_q2[e], lam_k2=lam_k2[e], subln_g=subln_g[e],
                     conv_w=conv_w[e], a_log=a_log[e], dt_bias=dt_bias[e], gn_b=gn_b[e])
            attn_p = functools.partial(_attn_prompt, n_b=n_bp, seq=seq)
            delta_p = functools.partial(
                _delta, cbuf=jnp.zeros((n_bp, K_CONV - 1, 3 * WIDTH), F32),
                s0=jnp.zeros((n_bp, N_HEADS, HEAD, HEAD), F32), n_b=n_bp, seq=seq)
            xp, (k_p, v_p, c_p, s_p) = _even_layer(li, xp, mods_p, norm_g[li], p, attn_p, delta_p)
            attn_s = functools.partial(_attn_decode, cache_k=cache_k, cache_v=cache_v, page_table=page_table,
                                       layer=e)
            delta_s = functools.partial(_delta_step, cbuf=conv_all, s0=ssm_all, seq_off=e * n_bs)
            xs, (k_s, v_s, c_s, s_s) = _even_layer(li, xs, mods_s, norm_g[li], p, attn_s, delta_s)
            outs["k_p"].append(k_p.reshape(n_bp, seq, N_HEADS, HEAD))
            outs["v_p"].append(v_p.reshape(n_bp, seq, N_HEADS, HEAD))
            outs["k_s"].append(k_s.reshape(n_bs, 1, N_HEADS, HEAD))
            outs["v_s"].append(v_s.reshape(n_bs, 1, N_HEADS, HEAD))
            outs["cv_p"].append(c_p)
            outs["cv_s"].append(c_s)
            outs["dl_p"].append(s_p)
            outs["dl_s"].append(s_s)
        else:
            o = li // 2
            p = dict(w_in=w_in_o[o], w_out=w_out_o[o], s5_d=s5_d[o], w_glu=w_glu[o], gn_d=gn_d[o])
            lb_re, lb_im, bbr, bbi = _s5_prep(s5_a_re[o], s5_a_im[o], s5_log_dt[o], s5_b_re[o], s5_b_im[o])
            hg = S5_HALF_GROUPS
            bb4 = jnp.stack([_block_diag(t[h * hg:(h + 1) * hg]) for h in range(2) for t in (bbr, bbi)]).astype(BF16)
            cre = jnp.swapaxes(s5_c_re[o], 1, 2)
            cim = -jnp.swapaxes(s5_c_im[o], 1, 2)
            cc4 = jnp.stack([_block_diag(t[h * hg:(h + 1) * hg]) for h in range(2) for t in (cre, cim)]).astype(BF16)
            s5_mats = (lb_re.reshape(1, n_g * n_p), lb_im.reshape(1, n_g * n_p), bb4, cc4)
            zeros_c = jnp.zeros((n_bp, n_g * n_p), F32)
            xp, (r_p, i_p, t_p) = _odd_layer(xp, mods_p, norm_g[li], p, s5_mats, zeros_c, zeros_c,
                                             jnp.zeros((n_bp, N_HEADS, HEAD, HEAD), F32), True)
            xs, (r_s, i_s, t_s) = _odd_layer(xs, mods_s, norm_g[li], p, s5_mats,
                                             state_c_re[o].reshape(n_bs, n_g * n_p),
                                             state_c_im[o].reshape(n_bs, n_g * n_p), ret_all, False, o * n_bs)
            outs["s5r_p"].append(r_p.reshape(n_bp, n_g, n_p))
            outs["s5i_p"].append(i_p.reshape(n_bp, n_g, n_p))
            outs["s5r_s"].append(r_s.reshape(n_bs, n_g, n_p))
            outs["s5i_s"].append(i_s.reshape(n_bs, n_g, n_p))
            outs["rt_p"].append(t_p)
            outs["rt_s"].append(t_s)
    st = lambda name: jnp.stack(outs[name])
    return (xp, xs.reshape(n_bs, 1, d), st("k_p"), st("v_p"), st("k_s"), st("v_s"), st("cv_p"), st("cv_s"),
            st("dl_p"), st("dl_s"), st("s5r_p"), st("s5i_p"), st("s5r_s"), st("s5i_s"), st("rt_p"), st("rt_s"))
```

```python
import functools
import math

import jax
import jax.numpy as jnp
from jax import lax
from jax.experimental import pallas as pl
from jax.experimental.pallas import tpu as pltpu

F32 = jnp.float32
BF16 = jnp.bfloat16
EPS = 1e-6

LANES = 128
SUBLANES = 8
VMEM_LIMIT_BYTES = 48 * 1024 * 1024

HEAD = 128
DQK = HEAD // 2
LOG2E = math.log2(math.e)
N_HEADS = 4
WIDTH = N_HEADS * HEAD
K_CONV = 4
DELTA_CHUNK = 64
GROUP_C = 16
P_C = 64
S5_HALF_GROUPS = 16

ROW_TILE = 256
ATTN_TILE = 512
ATTN_ROW_CHUNK = 256
DELTA_ROWS = 256
DELTA_SEQS = 1
DELTA_STACK = 4
RET_ROWS = 256
S5_STEPS = 64
S5_LANE_GROUP = 512
STEP_SEQS = 4


def _params(n_axes, vmem=VMEM_LIMIT_BYTES):
    return pltpu.CompilerParams(dimension_semantics=("arbitrary",) * n_axes, vmem_limit_bytes=vmem)


def _sigmoid(x):
    return 1.0 / (1.0 + jnp.exp(-x))


def _silu(x):
    return x * _sigmoid(x)


def _softplus(x):
    return jnp.maximum(x, 0.0) + jnp.log1p(jnp.exp(-jnp.abs(x)))


def _gelu_tanh(x):
    return 0.5 * x * (1.0 + jnp.tanh(math.sqrt(2.0 / math.pi) * (x + 0.044715 * (x * x * x))))


def _mm(a, b):
    return jnp.dot(a.astype(BF16), b.astype(BF16), preferred_element_type=F32)


def _mm_nt(a, b):
    return lax.dot_general(a.astype(BF16), b.astype(BF16), (((1,), (1,)), ((), ())),
                           preferred_element_type=F32)


def _mod_kernel(c_ref, w_ref, b_ref, o_ref):
    o_ref[...] = _mm(_silu(c_ref[...]), w_ref[...]) + b_ref[...]


def _modulation(c_all, w_ada, b_ada):
    depth, d, n3 = w_ada.shape
    rows = c_all.shape[0]
    tn = d
    return pl.pallas_call(
        _mod_kernel,
        grid=(depth, n3 // tn),
        in_specs=[pl.BlockSpec((rows, d), lambda l, n: (0, 0)),
                  pl.BlockSpec((None, d, tn), lambda l, n: (l, 0, n)),
                  pl.BlockSpec((None, 1, tn), lambda l, n: (l, 0, n))],
        out_specs=pl.BlockSpec((None, rows, tn), lambda l, n: (l, 0, n)),
        out_shape=jax.ShapeDtypeStruct((depth, rows, n3), F32),
        compiler_params=_params(2),
        name="modulation",
    )(c_all, w_ada, b_ada.reshape(depth, 1, n3))


def _in_kernel(segs, x_ref, shift_ref, scale_ref, g_ref, w_ref, bd_ref, qg_ref, kg_ref, *refs):
    out_refs, h_ref = iter(refs[:-1]), refs[-1]
    x = x_ref[...]
    h = x * lax.rsqrt(jnp.mean(x * x, axis=-1, keepdims=True) + EPS) * g_ref[...]
    h_ref[...] = (h * (1.0 + scale_ref[...]) + shift_ref[...]).astype(BF16)
    for c0, width, kind, layout in segs:
        acc = jnp.dot(h_ref[...], w_ref[:, c0:c0 + width], preferred_element_type=F32)
        if kind == "silu":
            acc = _silu(acc)
        elif kind in ("qnorm", "knorm"):
            sq = acc * acc
            hi = sq.astype(BF16)
            lo = (sq - hi.astype(F32)).astype(BF16)
            ms = (jnp.dot(hi, bd_ref[...], preferred_element_type=F32)
                  + jnp.dot(lo, bd_ref[...], preferred_element_type=F32))
            acc = acc * lax.rsqrt(ms + EPS) * (qg_ref if kind == "qnorm" else kg_ref)[...]
            if kind == "qnorm":
                acc = acc * (DQK ** -0.5 * LOG2E)
        o_ref = next(out_refs)
        o_ref[...] = acc.astype(o_ref.dtype)
        if layout == "heads":
            hm_ref = next(out_refs)
            n_h = width // HEAD
            for h in range(n_h):
                hm_ref[pl.ds(h, acc.shape[0], stride=n_h), :] = acc[:, h * HEAD:(h + 1) * HEAD]


def _in_proj(x3, shift, scale, g, w, segs, bd, qg, kg):
    n_g, rows, d = x3.shape
    tm = min(ROW_TILE, rows)
    nt = rows // tm
    r_mod = shift.shape[1]
    assert r_mod in (1, tm)
    out_shapes, out_specs = [], []
    row_map = lambda gi, i: (gi * nt + i, 0)
    for _, width, _, dtype, layout in segs:
        if layout == "time":
            out_shapes.append(jax.ShapeDtypeStruct((rows, n_g * width), dtype))
            out_specs.append(pl.BlockSpec((tm, width), lambda gi, i: (i, gi)))
        elif layout == "heads":
            n_h = width // HEAD
            out_shapes += [jax.ShapeDtypeStruct((n_g * rows, width), BF16),
                           jax.ShapeDtypeStruct((n_g * rows * n_h, HEAD), dtype)]
            out_specs += [pl.BlockSpec((tm, width), row_map), pl.BlockSpec((tm * n_h, HEAD), row_map)]
        else:
            out_shapes.append(jax.ShapeDtypeStruct((n_g * rows, width), dtype))
            out_specs.append(pl.BlockSpec((tm, width), row_map))
    const = lambda gi, i: (0, 0)
    return pl.pallas_call(
        functools.partial(_in_kernel, tuple((s[0], s[1], s[2], s[4]) for s in segs)),
        grid=(n_g, nt),
        in_specs=[pl.BlockSpec((None, tm, d), lambda gi, i: (gi, i, 0)),
                  pl.BlockSpec((None, r_mod, d), lambda gi, i: (gi, 0, 0)),
                  pl.BlockSpec((None, r_mod, d), lambda gi, i: (gi, 0, 0)),
                  pl.BlockSpec((1, d), const),
                  pl.BlockSpec(w.shape, const),
                  pl.BlockSpec(bd.shape, const),
                  pl.BlockSpec(qg.shape, const),
                  pl.BlockSpec(kg.shape, const)],
        out_specs=out_specs,
        out_shape=out_shapes,
        scratch_shapes=[pltpu.VMEM((tm, d), BF16)],
        compiler_params=_params(2),
        name="in_proj",
    )(x3, shift, scale, g.reshape(1, d), w, bd, qg, kg)


def _lambda_value(lamp, lam_init):
    a = jnp.sum(lamp[0:1] * lamp[1:2], axis=-1, keepdims=True)
    b = jnp.sum(lamp[2:3] * lamp[3:4], axis=-1, keepdims=True)
    return jnp.exp(a) - jnp.exp(b) + lam_init


def _sub_ln(o, gain, lam_init):
    return o * lax.rsqrt(jnp.mean(o * o, axis=-1, keepdims=True) + EPS) * gain * (1.0 - lam_init)


def _attn_kernel(lam_init, row_chunk, qt_ref, kt_ref, q_ref, k_ref, v_ref, kbias_ref, lamp_ref, sg_ref, o_ref,
                 qs_ref, m_ref, l_ref, acc_ref):
    qi, ki = qt_ref[pl.program_id(2)], kt_ref[pl.program_id(2)]
    tq, tk = q_ref.shape[0], k_ref.shape[0]

    @pl.when(ki == 0)
    def _init():
        m_ref[...] = jnp.full(m_ref.shape, -jnp.inf, F32)
        l_ref[...] = jnp.zeros(l_ref.shape, F32)
        acc_ref[...] = jnp.zeros(acc_ref.shape, F32)
        q = q_ref[...]
        lane = lax.broadcasted_iota(jnp.int32, (1, HEAD), 1)
        qs_ref[0:tq, :] = jnp.where(lane < DQK, q, jnp.zeros_like(q))
        qs_ref[tq:2 * tq, :] = jnp.where(lane >= DQK, q, jnp.zeros_like(q))

    def step(masked):
        kb = k_ref[...].astype(BF16)
        vb = v_ref[...].astype(BF16)
        kbias = kbias_ref[...]
        for r0 in range(0, 2 * tq, row_chunk):
            rs = slice(r0, r0 + row_chunk)
            n_k = min(tk, r0 % tq + row_chunk) if masked else tk
            s = _mm_nt(qs_ref[rs, :], kb[:n_k]) + kbias[:, :n_k]
            if masked:
                q_row = lax.broadcasted_iota(jnp.int32, (row_chunk, n_k), 0) + (r0 % tq)
                col = lax.broadcasted_iota(jnp.int32, (row_chunk, n_k), 1)
                s = jnp.where(col <= q_row, s, -jnp.inf)
            m_prev = m_ref[rs, :]
            m_new = jnp.maximum(m_prev, jnp.max(s, axis=-1, keepdims=True))
            alpha = jnp.exp2(m_prev - m_new)
            p = jnp.exp2(s - m_new)
            l_ref[rs, :] = alpha * l_ref[rs, :] + jnp.sum(p, axis=-1, keepdims=True)
            acc_ref[rs, :] = alpha * acc_ref[rs, :] + jnp.dot(p.astype(BF16), vb[:n_k], preferred_element_type=F32)
            m_ref[rs, :] = m_new

    @pl.when(ki < qi)
    def _off_diagonal():
        step(False)

    @pl.when(ki == qi)
    def _diagonal():
        step(True)
        lam = _lambda_value(lamp_ref[...], lam_init)
        o = acc_ref[0:tq, :] / l_ref[0:tq, :] - lam * (acc_ref[tq:2 * tq, :] / l_ref[tq:2 * tq, :])
        o_ref[...] = _sub_ln(o, sg_ref[...], lam_init).astype(o_ref.dtype)


def _alibi_slopes(n_heads):
    return jnp.exp2(-8.0 * jnp.arange(1, n_heads + 1, dtype=F32) / n_heads)


def _attn_prompt(q, k, v, lamp, sg, lam_init, n_b, seq):
    t = min(ATTN_TILE, seq)
    nq = seq // t
    row_chunk = min(ATTN_ROW_CHUNK, t)
    kbias = (LOG2E * _alibi_slopes(N_HEADS)[:, None] * jnp.arange(seq, dtype=F32)[None, :]).reshape(N_HEADS, nq, 1, t)
    pairs = [(qi, ki) for qi in range(nq) for ki in range(qi + 1)]
    qt = jnp.asarray([p[0] for p in pairs], jnp.int32)
    kt = jnp.asarray([p[1] for p in pairs], jnp.int32)
    q_map = lambda b, h, t_, qt_, kt_: (b * nq + qt_[t_], h)
    kv_map = lambda b, h, t_, qt_, kt_: (b * nq + kt_[t_], h)
    const = lambda b, h, t_, qt_, kt_: (0, 0)
    grid_spec = pltpu.PrefetchScalarGridSpec(
        num_scalar_prefetch=2,
        grid=(n_b, N_HEADS, len(pairs)),
        in_specs=[pl.BlockSpec((t, HEAD), q_map),
                  pl.BlockSpec((t, HEAD), kv_map),
                  pl.BlockSpec((t, HEAD), kv_map),
                  pl.BlockSpec((None, None, 1, t), lambda b, h, t_, qt_, kt_: (h, kt_[t_], 0, 0)),
                  pl.BlockSpec(lamp.shape, const),
                  pl.BlockSpec((1, HEAD), const)],
        out_specs=pl.BlockSpec((t, HEAD), q_map),
        scratch_shapes=[pltpu.VMEM((2 * t, HEAD), BF16), pltpu.VMEM((2 * t, 1), F32),
                        pltpu.VMEM((2 * t, 1), F32), pltpu.VMEM((2 * t, HEAD), F32)],
    )
    return pl.pallas_call(
        functools.partial(_attn_kernel, lam_init, row_chunk),
        grid_spec=grid_spec,
        out_shape=jax.ShapeDtypeStruct((n_b * seq, WIDTH), BF16),
        compiler_params=_params(3),
        name="attn_prompt",
    )(qt, kt, q, k, v, kbias, lamp, sg)


def _decode_kernel(lam_init, n_pages, pt_ref, q_ref, kn_ref, vn_ref, nsl_ref, lamp_ref, sg_ref, *refs):
    del pt_ref
    k_refs, v_refs, o_ref = refs[:n_pages], refs[n_pages:2 * n_pages], refs[2 * n_pages]
    cols_page = k_refs[0].shape[0]
    n_rows = 4 * N_HEADS
    row = lax.broadcasted_iota(jnp.int32, (n_rows, HEAD), 0)
    lane = lax.broadcasted_iota(jnp.int32, (n_rows, HEAD), 1)

    def per_row_head(x):
        out = jnp.zeros((n_rows, HEAD), F32)
        for h in range(N_HEADS):
            out = jnp.where((row % N_HEADS == h) & (row < 2 * N_HEADS), x[:, h * HEAD:(h + 1) * HEAD], out)
        return out

    qm = jnp.where(lane // DQK == row // N_HEADS, per_row_head(q_ref[...].astype(F32)), 0.0).astype(BF16)
    s = jnp.concatenate([_mm_nt(qm, k_refs[j][...]) for j in range(n_pages)], axis=1)
    n_cols = n_pages * cols_page
    past = n_cols // N_HEADS
    col = lax.broadcasted_iota(jnp.int32, (n_rows, n_cols), 1)
    row_c = lax.broadcasted_iota(jnp.int32, (n_rows, n_cols), 0)
    own_head = (col % N_HEADS == row_c % N_HEADS) & (row_c < 2 * N_HEADS)
    s = jnp.where(own_head, s + nsl_ref[...] * (past - col // N_HEADS).astype(F32), -jnp.inf)
    s_self = jnp.sum(qm.astype(F32) * per_row_head(kn_ref[...]).astype(BF16).astype(F32), axis=-1, keepdims=True)
    m = jnp.maximum(jnp.max(s, axis=-1, keepdims=True), s_self)
    p = jnp.exp2(s - m)
    p_self = jnp.exp2(s_self - m)
    denom = jnp.sum(p, axis=-1, keepdims=True) + p_self
    lam = _lambda_value(lamp_ref[...], lam_init)
    r1 = row[:, 0:1]
    coef = jnp.where(r1 < N_HEADS, 1.0, jnp.where(r1 < 2 * N_HEADS, -lam, 0.0)) / denom
    pw = p * coef
    acc = (p_self * coef) * per_row_head(vn_ref[...]).astype(BF16).astype(F32)
    for j in range(n_pages):
        acc = acc + _mm(pw[:, j * cols_page:(j + 1) * cols_page], v_refs[j][...])
    for h in range(N_HEADS):
        o = acc[h:h + 1, :] + acc[N_HEADS + h:N_HEADS + h + 1, :]
        o_ref[:, h * HEAD:(h + 1) * HEAD] = _sub_ln(o, sg_ref[...], lam_init)


def _attn_decode(q, k_new, v_new, cache_k, cache_v, page_table, lamp, sg, lam_init, layer):
    n_s, n_pages = page_table.shape
    n_l, n_phys, page = cache_k.shape[:3]
    cache_k = cache_k.reshape(n_l, n_phys, page * N_HEADS, HEAD)
    cache_v = cache_v.reshape(n_l, n_phys, page * N_HEADS, HEAD)
    nsl = -LOG2E * _alibi_slopes(N_HEADS)
    nsl8 = jnp.concatenate([nsl, nsl, jnp.zeros((2 * N_HEADS,), F32)]).reshape(4 * N_HEADS, 1)
    row_spec = pl.BlockSpec((None, 1, WIDTH), lambda b, pt: (b, 0, 0))
    const = lambda b, pt: (0, 0)

    def page_spec(j):
        return pl.BlockSpec((None, None, page * N_HEADS, HEAD), lambda b, pt: (layer, pt[b, j], 0, 0))

    grid_spec = pltpu.PrefetchScalarGridSpec(
        num_scalar_prefetch=1,
        grid=(n_s,),
        in_specs=[row_spec, row_spec, row_spec,
                  pl.BlockSpec(nsl8.shape, const), pl.BlockSpec(lamp.shape, const),
                  pl.BlockSpec((1, HEAD), const)]
                 + [page_spec(j) for j in range(n_pages)] * 2,
        out_specs=row_spec,
    )
    out = pl.pallas_call(
        functools.partial(_decode_kernel, lam_init, n_pages),
        grid_spec=grid_spec,
        out_shape=jax.ShapeDtypeStruct((n_s, 1, WIDTH), F32),
        compiler_params=_params(1),
        name="attn_decode",
    )(page_table, q.reshape(n_s, 1, WIDTH), k_new.reshape(n_s, 1, WIDTH), v_new.reshape(n_s, 1, WIDTH),
      nsl8, lamp, sg, *([cache_k] * n_pages), *([cache_v] * n_pages))
    return out.reshape(n_s, WIDTH)


INV_BASE = SUBLANES


def _unit_lower_inverse(a, block, row, col):
    in_base = row // INV_BASE == col // INV_BASE
    power = jnp.where(in_base, a, 0.0)
    inv = jnp.where(row == col, 1.0, 0.0) - power
    order = 2
    while order < INV_BASE:
        power = _mm(power, power)
        inv = inv + _mm(inv, power)
        order *= 2
    size = INV_BASE
    while size < block:
        coupling = jnp.where((row // (2 * size) == col // (2 * size)) & (row // size != col // size), a, 0.0)
        inv = inv - _mm(_mm(inv, coupling), inv)
        size *= 2
    return inv


def _mm_tn(a, b):
    return lax.dot_general(a.astype(BF16), b.astype(BF16), (((0,), (0,)), ((), ())),
                           preferred_element_type=F32)


def _delta_kernel(chunk, qkv_ref, ab_ref, cw_ref, hp_ref, gn_ref, cbuf_ref, s0_ref, tri_ref,
                  o_ref, cnew_ref, sout_ref, xs_ref, st_ref):
    r = pl.program_id(1)
    n_seq = qkv_ref.shape[0]
    tail = K_CONV - 1

    @pl.when(r == 0)
    def _init():
        st_ref[...] = s0_ref[...]
        for i in range(n_seq):
            xs_ref[i, 0:SUBLANES, :] = jnp.zeros((SUBLANES, xs_ref.shape[2]), F32)
            xs_ref[i, SUBLANES - tail:SUBLANES, :] = cbuf_ref[i]

    for i in range(n_seq):
        _delta_sequence(chunk, qkv_ref.at[i], ab_ref.at[i], cw_ref, hp_ref, gn_ref, tri_ref,
                        o_ref.at[i], cnew_ref.at[i], xs_ref.at[i], st_ref.at[i])

    @pl.when(r == pl.num_programs(1) - 1)
    def _finish():
        sout_ref[...] = st_ref[...]


def _delta_sequence(chunk, qkv_ref, ab_ref, cw_ref, hp_ref, gn_ref, tri_ref, o_ref, cnew_ref, xs_ref, st_ref):
    rows = qkv_ref.shape[0]
    n_chunks = rows // chunk
    tail = K_CONV - 1
    xs_ref[SUBLANES:SUBLANES + rows, :] = qkv_ref[...]
    w = cw_ref[...]
    y = w[0:1] * xs_ref[SUBLANES - 3:SUBLANES - 3 + rows, :]
    for j in range(1, K_CONV):
        y = y + w[j:j + 1] * xs_ref[SUBLANES - 3 + j:SUBLANES - 3 + j + rows, :]
    new_tail = xs_ref[SUBLANES + rows - tail:SUBLANES + rows, :]
    cnew_ref[...] = new_tail
    xs_ref[SUBLANES - tail:SUBLANES, :] = new_tail
    act = _silu(y)

    ab = ab_ref[...]
    hp = hp_ref[...]
    g_all = -jnp.exp(hp[0:1]) * _softplus(ab + hp[1:2])
    beta_all = _sigmoid(ab)
    gam_all = jnp.dot(tri_ref[...], g_all, preferred_element_type=F32, precision=lax.Precision.HIGHEST)

    n_st = DELTA_STACK * chunk
    ri = lax.broadcasted_iota(jnp.int32, (n_st, n_st), 0)
    ci = lax.broadcasted_iota(jnp.int32, (n_st, n_st), 1)
    same_head = ri // chunk == ci // chunk
    incl, strict, diag = same_head & (ri >= ci), same_head & (ri > ci), ri == ci
    gn = gn_ref[...]
    local = [slice(j * chunk, (j + 1) * chunk) for j in range(DELTA_STACK)]
    states = [st_ref[h] for h in range(N_HEADS)]

    for c in range(n_chunks):
        rs = slice(c * chunk, (c + 1) * chunk)
        for h0 in range(0, N_HEADS, DELTA_STACK):
            hg = range(h0, h0 + DELTA_STACK)

            def stacked(col0):
                return jnp.concatenate([act[rs, col0 + h * HEAD:col0 + (h + 1) * HEAD] for h in hg], axis=0)

            q, k, v = stacked(0), stacked(WIDTH), stacked(2 * WIDTH)
            q = q * lax.rsqrt(jnp.sum(q * q, axis=-1, keepdims=True) + EPS) * HEAD ** -0.5
            k = k * lax.rsqrt(jnp.sum(k * k, axis=-1, keepdims=True) + EPS)
            gc = jnp.concatenate([gam_all[rs, h:h + 1] for h in hg], axis=0)
            bc = jnp.concatenate([beta_all[rs, N_HEADS + h:N_HEADS + h + 1] for h in hg], axis=0)
            gr = jnp.sum(jnp.where(diag, gc, 0.0), axis=0, keepdims=True)
            decay = jnp.exp(jnp.where(incl, gc - gr, -jnp.inf))
            inv = _unit_lower_inverse(jnp.where(strict, bc * decay * _mm_nt(k, k), 0.0), chunk, ri, ci)
            eg = jnp.exp(gc)
            sol = _mm(inv, jnp.concatenate([bc * v, (bc * eg) * k], axis=1))
            qk = _mm_nt(q, k) * decay
            u = jnp.concatenate([sol[ls, :HEAD] - _mm(sol[ls, HEAD:], states[h]) for h, ls in zip(hg, local)], axis=0)
            o = eg * jnp.concatenate([_mm(q[ls], states[h]) for h, ls in zip(hg, local)], axis=0) + _mm(qk, u)
            for h, ls in zip(hg, local):
                gl = gc[ls.stop - 1:ls.stop, :]
                states[h] = jnp.exp(gl) * states[h] + _mm_tn(k[ls] * jnp.exp(gl - gc[ls]), u[ls])
            on = o * lax.rsqrt(jnp.mean(o * o, axis=-1, keepdims=True) + EPS) * gn
            for h, ls in zip(hg, local):
                o_ref[rs, h * HEAD:(h + 1) * HEAD] = on[ls].astype(o_ref.dtype)

    for h in range(N_HEADS):
        st_ref[h] = states[h]


def _chunk_tri(rows, chunk):
    r = jnp.arange(rows)
    return ((r[:, None] >= r[None, :]) & (r[:, None] // chunk == r[None, :] // chunk)).astype(F32)


def _head_params(a_log, dt_bias):
    return jnp.zeros((SUBLANES, LANES), F32).at[0, :N_HEADS].set(a_log).at[1, :N_HEADS].set(dt_bias)


def _delta(qkv, ab, conv_w, a_log, dt_bias, gn, cbuf, s0, n_b, seq):
    chunk = min(DELTA_CHUNK, seq)
    rows = min(DELTA_ROWS, seq)
    assert seq % rows == 0 and rows % chunk == 0
    nr = seq // rows
    nbb = math.gcd(DELTA_SEQS, n_b)
    width3 = 3 * WIDTH
    hp = _head_params(a_log, dt_bias)
    tri = _chunk_tri(rows, chunk)
    const = lambda b, r: (0, 0)
    tok = lambda width: pl.BlockSpec((nbb, rows, width), lambda b, r: (b, r, 0))
    tail = pl.BlockSpec((nbb, K_CONV - 1, width3), lambda b, r: (b, 0, 0))
    st = pl.BlockSpec((nbb, N_HEADS, HEAD, HEAD), lambda b, r: (b, 0, 0, 0))
    o, cnew, s_new = pl.pallas_call(
        functools.partial(_delta_kernel, chunk),
        grid=(n_b // nbb, nr),
        in_specs=[tok(width3), tok(LANES),
                  pl.BlockSpec((K_CONV, width3), const),
                  pl.BlockSpec((SUBLANES, LANES), const),
                  pl.BlockSpec((1, HEAD), const),
                  tail, st,
                  pl.BlockSpec((rows, rows), const)],
        out_specs=[tok(WIDTH), tail, st],
        out_shape=[jax.ShapeDtypeStruct((n_b, seq, WIDTH), BF16),
                   jax.ShapeDtypeStruct((n_b, K_CONV - 1, width3), F32),
                   jax.ShapeDtypeStruct((n_b, N_HEADS, HEAD, HEAD), F32)],
        scratch_shapes=[pltpu.VMEM((nbb, rows + SUBLANES, width3), F32),
                        pltpu.VMEM((nbb, N_HEADS, HEAD, HEAD), F32)],
        compiler_params=_params(2),
        name="delta_rule",
    )(qkv.reshape(n_b, seq, width3), ab.reshape(n_b, seq, LANES), conv_w, hp, gn.reshape(1, HEAD), cbuf, s0, tri)
    return o.reshape(n_b * seq, WIDTH), cnew, s_new


def _as_column(row_vec, eye):
    return jnp.sum(jnp.where(eye, row_vec, 0.0), axis=1, keepdims=True)


def _eye(n):
    return lax.broadcasted_iota(jnp.int32, (n, n), 0) == lax.broadcasted_iota(jnp.int32, (n, n), 1)


def _delta_step_kernel(x_ref, ab_ref, cw_ref, hp_ref, gn_ref, cbuf_ref, s0_ref, o_ref, cnew_ref, sout_ref):
    w = cw_ref[...]
    hp = hp_ref[...]
    eye = _eye(HEAD)
    for i in range(x_ref.shape[0]):
        x = x_ref[i]
        buf = cbuf_ref[i]
        y = w[0:1] * buf[0:1] + w[1:2] * buf[1:2] + w[2:3] * buf[2:3] + w[3:4] * x
        cnew_ref[i, 0:2, :] = buf[1:3]
        cnew_ref[i, 2:3, :] = x
        act = _silu(y)
        ab = ab_ref[i]
        decay_all = jnp.exp(-jnp.exp(hp[0:1]) * _softplus(ab + hp[1:2]))
        beta_all = _sigmoid(ab)
        for h in range(N_HEADS):
            qh = act[:, h * HEAD:(h + 1) * HEAD]
            kh = act[:, WIDTH + h * HEAD:WIDTH + (h + 1) * HEAD]
            vh = act[:, 2 * WIDTH + h * HEAD:2 * WIDTH + (h + 1) * HEAD]
            qh = qh * lax.rsqrt(jnp.sum(qh * qh, axis=-1, keepdims=True) + EPS) * HEAD ** -0.5
            kh = kh * lax.rsqrt(jnp.sum(kh * kh, axis=-1, keepdims=True) + EPS)
            a = decay_all[:, h:h + 1]
            beta = beta_all[:, N_HEADS + h:N_HEADS + h + 1]
            state = s0_ref[i, h]
            k_col = _as_column(kh, eye)
            u = beta * (vh - a * jnp.sum(state * k_col, axis=0, keepdims=True))
            state = a * state + k_col * u
            sout_ref[i, h] = state
            o = jnp.sum(state * _as_column(qh, eye), axis=0, keepdims=True)
            o_ref[i, :, h * HEAD:(h + 1) * HEAD] = (
                o * lax.rsqrt(jnp.mean(o * o, axis=-1, keepdims=True) + EPS) * gn_ref[...])


def _step_block(n_s, seq_off):
    return math.gcd(STEP_SEQS, math.gcd(n_s, seq_off))


def _delta_step(qkv, ab, conv_w, a_log, dt_bias, gn, cbuf, s0, seq_off):
    n_s = qkv.shape[0]
    nb = _step_block(n_s, seq_off)
    off = seq_off // nb
    width3 = 3 * WIDTH
    const = lambda b: (0, 0)
    row = lambda width: pl.BlockSpec((nb, 1, width), lambda b: (b, 0, 0))
    tail = pl.BlockSpec((nb, K_CONV - 1, width3), lambda b: (b, 0, 0))
    st = pl.BlockSpec((nb, N_HEADS, HEAD, HEAD), lambda b: (b, 0, 0, 0))
    tail_in = pl.BlockSpec((nb, K_CONV - 1, width3), lambda b: (b + off, 0, 0))
    st_in = pl.BlockSpec((nb, N_HEADS, HEAD, HEAD), lambda b: (b + off, 0, 0, 0))
    o, cnew, s_new = pl.pallas_call(
        _delta_step_kernel,
        grid=(n_s // nb,),
        in_specs=[row(width3), row(LANES), pl.BlockSpec((K_CONV, width3), const),
                  pl.BlockSpec((SUBLANES, LANES), const), pl.BlockSpec((1, HEAD), const), tail_in, st_in],
        out_specs=[row(WIDTH), tail, st],
        out_shape=[jax.ShapeDtypeStruct((n_s, 1, WIDTH), F32),
                   jax.ShapeDtypeStruct((n_s, K_CONV - 1, width3), F32),
                   jax.ShapeDtypeStruct((n_s, N_HEADS, HEAD, HEAD), F32)],
        compiler_params=_params(1),
        name="delta_step",
    )(qkv.reshape(n_s, 1, width3), ab.reshape(n_s, 1, LANES), conv_w, _head_params(a_log, dt_bias),
      gn.reshape(1, HEAD), cbuf, s0)
    return o.reshape(n_s, WIDTH), cnew, s_new


def _log_gamma(h):
    return math.log1p(-(2.0 ** (-5.0 - h)))


def _layer_norm(o, gain):
    oc = o - jnp.mean(o, axis=-1, keepdims=True)
    return oc * lax.rsqrt(jnp.mean(oc * oc, axis=-1, keepdims=True) + EPS) * gain


def _ret_kernel(q_ref, k_ref, v_ref, gn_ref, s0_ref, o_ref, sout_ref, st_ref):
    r = pl.program_id(1)
    rows = q_ref.shape[0]

    @pl.when(r == 0)
    def _init():
        st_ref[...] = s0_ref[...]

    ri = lax.broadcasted_iota(jnp.int32, (rows, 1), 0)
    ci = lax.broadcasted_iota(jnp.int32, (1, rows), 1)
    cnt_r = (ri + 1).astype(F32)
    cnt_c = (ci + 1).astype(F32)
    for h in range(N_HEADS):
        log_gamma = _log_gamma(h)
        cols = slice(h * HEAD, (h + 1) * HEAD)
        gc = cnt_r * log_gamma
        decay = jnp.exp(jnp.where(ri >= ci, gc - cnt_c * log_gamma, -jnp.inf))
        qh = q_ref[:, cols]
        kh = k_ref[:, cols] * HEAD ** -0.5
        vh = v_ref[:, cols]
        state = st_ref[h]
        o = jnp.exp(gc) * _mm(qh, state) + _mm(_mm_nt(qh, kh) * decay, vh)
        gl = rows * log_gamma
        st_ref[h] = math.exp(gl) * state + _mm_tn(kh * jnp.exp(gl - gc), vh)
        o_ref[:, cols] = _layer_norm(o, gn_ref[...]).astype(o_ref.dtype)

    @pl.when(r == pl.num_programs(1) - 1)
    def _finish():
        sout_ref[...] = st_ref[...]


def _ret_step_kernel(q_ref, k_ref, v_ref, gn_ref, s0_ref, o_ref, sout_ref):
    eye = _eye(HEAD)
    for i in range(q_ref.shape[0]):
        for h in range(N_HEADS):
            cols = slice(h * HEAD, (h + 1) * HEAD)
            k_col = _as_column(k_ref[i, :, cols] * HEAD ** -0.5, eye)
            state = math.exp(_log_gamma(h)) * s0_ref[i, h] + k_col * v_ref[i, :, cols]
            sout_ref[i, h] = state
            o = jnp.sum(state * _as_column(q_ref[i, :, cols], eye), axis=0, keepdims=True)
            o_ref[i, :, cols] = _layer_norm(o, gn_ref[...])


def _retention_step(q, k, v, gn, s0, seq_off):
    n_s = q.shape[0]
    nb = _step_block(n_s, seq_off)
    off = seq_off // nb
    row = pl.BlockSpec((nb, 1, WIDTH), lambda b: (b, 0, 0))
    st = pl.BlockSpec((nb, N_HEADS, HEAD, HEAD), lambda b: (b, 0, 0, 0))
    st_in = pl.BlockSpec((nb, N_HEADS, HEAD, HEAD), lambda b: (b + off, 0, 0, 0))
    r3 = lambda t: t.reshape(n_s, 1, WIDTH)
    o, s_new = pl.pallas_call(
        _ret_step_kernel,
        grid=(n_s // nb,),
        in_specs=[row, row, row, pl.BlockSpec((1, HEAD), lambda b: (0, 0)), st_in],
        out_specs=[row, st],
        out_shape=[jax.ShapeDtypeStruct((n_s, 1, WIDTH), F32),
                   jax.ShapeDtypeStruct((n_s, N_HEADS, HEAD, HEAD), F32)],
        compiler_params=_params(1),
        name="retention_step",
    )(r3(q), r3(k), r3(v), gn.reshape(1, HEAD), s0)
    return o.reshape(n_s, WIDTH), s_new


def _retention(q, k, v, gn, s0, n_b, seq):
    rows = min(RET_ROWS, seq)
    assert seq % rows == 0
    nr = seq // rows
    tok = pl.BlockSpec((rows, WIDTH), lambda b, r: (b * nr + r, 0))
    st = pl.BlockSpec((None, N_HEADS, HEAD, HEAD), lambda b, r: (b, 0, 0, 0))
    return pl.pallas_call(
        _ret_kernel,
        grid=(n_b, nr),
        in_specs=[tok, tok, tok, pl.BlockSpec((1, HEAD), lambda b, r: (0, 0)), st],
        out_specs=[pl.BlockSpec((rows, WIDTH), lambda b, r: (b * nr + r, 0)), st],
        out_shape=[jax.ShapeDtypeStruct((n_b * seq, WIDTH), BF16),
                   jax.ShapeDtypeStruct((n_b, N_HEADS, HEAD, HEAD), F32)],
        scratch_shapes=[pltpu.VMEM((N_HEADS, HEAD, HEAD), F32)],
        compiler_params=_params(2),
        name="retention",
    )(q, k, v, gn.reshape(1, HEAD), s0)


def _s5_prep_kernel(are_ref, aim_ref, ldt_ref, brt_ref, bit_ref, lbr_ref, lbi_ref, bbr_ref, bbi_ref):
    dt = jnp.exp(ldt_ref[...])
    ar, ai = are_ref[...], aim_ref[...]
    mag = jnp.exp(ar * dt)
    ang = ai * dt
    lr, li = mag * jnp.cos(ang), mag * jnp.sin(ang)
    den = ar * ar + ai * ai
    fr = ((lr - 1.0) * ar + li * ai) / den
    fi = (li * ar - (lr - 1.0) * ai) / den
    lbr_ref[...] = lr
    lbi_ref[...] = li
    brt, bit = brt_ref[...], bit_ref[...]
    bbr_ref[...] = fr[:, None, :] * brt - fi[:, None, :] * bit
    bbi_ref[...] = fr[:, None, :] * bit + fi[:, None, :] * brt


def _s5_prep(a_re, a_im, log_dt, b_re, b_im):
    n_g, n_p = a_re.shape
    brt, bit = jnp.swapaxes(b_re, 1, 2), jnp.swapaxes(b_im, 1, 2)
    gp = jax.ShapeDtypeStruct((n_g, n_p), F32)
    gcp = jax.ShapeDtypeStruct(brt.shape, F32)
    return pl.pallas_call(_s5_prep_kernel, out_shape=[gp, gp, gcp, gcp], name="s5_prep")(
        a_re, a_im, log_dt.reshape(n_g, 1), brt, bit)


def _block_diag(blocks):
    n, r, c = blocks.shape
    return jnp.einsum("grc,gh->grhc", blocks, jnp.eye(n, dtype=blocks.dtype)).reshape(n * r, n * c)


def _s5_kernel(n_b, n_t, u_ref, x0r_ref, x0i_ref, lbr_ref, lbi_ref, bb_ref, cc_ref, d_ref,
               y_ref, xr_ref, xi_ref, utb_ref, x_ref, carry_ref):
    i = pl.program_id(0)
    n_p = lbr_ref.shape[1]
    half_in = S5_HALF_GROUPS * GROUP_C
    half_st = S5_HALF_GROUPS * P_C

    @pl.when(i == 0)
    def _init():
        carry_ref[:, :n_p] = x0r_ref[...]
        carry_ref[:, n_p:] = x0i_ref[...]

    n_lc = WIDTH // LANES
    if n_t > 1:
        for b in range(n_b):
            for c in range(n_lc):
                utb_ref[c, pl.ds(b, n_t, stride=n_b), :] = u_ref[:, b * WIDTH + c * LANES:b * WIDTH + (c + 1) * LANES]
        u = jnp.concatenate([utb_ref[c] for c in range(n_lc)], axis=1)
    else:
        u = u_ref[...]
    ub = u.astype(BF16)
    for half in range(2):
        for part in range(2):
            c0 = part * n_p + half * half_st
            x_ref[:, c0:c0 + half_st] = jnp.dot(ub[:, half * half_in:(half + 1) * half_in],
                                                bb_ref[half * 2 + part], preferred_element_type=F32)

    if n_t == 1:
        lr, li = lbr_ref[...], lbi_ref[...]
        x0r, x0i = carry_ref[:, :n_p], carry_ref[:, n_p:]
        xr = lr * x0r - li * x0i + x_ref[:, :n_p]
        xi = lr * x0i + li * x0r + x_ref[:, n_p:]
        x_ref[:, :n_p] = xr
        x_ref[:, n_p:] = xi
        carry_ref[:, :n_p] = xr
        carry_ref[:, n_p:] = xi
    else:
        for lg in range(n_p // S5_LANE_GROUP):
            l0 = lg * S5_LANE_GROUP
            lr = jnp.broadcast_to(lbr_ref[:, l0:l0 + S5_LANE_GROUP], (n_b, S5_LANE_GROUP))
            li = jnp.broadcast_to(lbi_ref[:, l0:l0 + S5_LANE_GROUP], (n_b, S5_LANE_GROUP))

            def body(t, carry, l0=l0, lr=lr, li=li):
                xr, xi = carry
                row = pl.multiple_of(t * n_b, n_b)
                nxr = lr * xr - li * xi + x_ref[pl.ds(row, n_b), l0:l0 + S5_LANE_GROUP]
                nxi = lr * xi + li * xr + x_ref[pl.ds(row, n_b), n_p + l0:n_p + l0 + S5_LANE_GROUP]
                x_ref[pl.ds(row, n_b), l0:l0 + S5_LANE_GROUP] = nxr
                x_ref[pl.ds(row, n_b), n_p + l0:n_p + l0 + S5_LANE_GROUP] = nxi
                return nxr, nxi

            xr, xi = lax.fori_loop(
                0, n_t, body,
                (carry_ref[:, l0:l0 + S5_LANE_GROUP], carry_ref[:, n_p + l0:n_p + l0 + S5_LANE_GROUP]),
                unroll=4)
            carry_ref[:, l0:l0 + S5_LANE_GROUP] = xr
            carry_ref[:, n_p + l0:n_p + l0 + S5_LANE_GROUP] = xi

    ys = []
    for half in range(2):
        xr_b = x_ref[:, half * half_st:(half + 1) * half_st].astype(BF16)
        xi_b = x_ref[:, n_p + half * half_st:n_p + (half + 1) * half_st].astype(BF16)
        ys.append(jnp.dot(xr_b, cc_ref[half * 2], preferred_element_type=F32)
                  + jnp.dot(xi_b, cc_ref[half * 2 + 1], preferred_element_type=F32))
    yg = _gelu_tanh(jnp.concatenate(ys, axis=1) + d_ref[...] * u)
    if n_t > 1:
        for c in range(n_lc):
            utb_ref[c] = yg[:, c * LANES:(c + 1) * LANES]
        for b in range(n_b):
            for c in range(n_lc):
                y_ref[:, b * WIDTH + c * LANES:b * WIDTH + (c + 1) * LANES] = utb_ref[c, pl.ds(b, n_t, stride=n_b), :]
    else:
        y_ref[...] = yg

    @pl.when(i == pl.num_programs(0) - 1)
    def _finish():
        xr_ref[...] = carry_ref[:, :n_p]
        xi_ref[...] = carry_ref[:, n_p:]


def _s5(u, x0_re, x0_im, lb_re, lb_im, bb4, cc4, d_skip, n_b, seq):
    n_p = lb_re.shape[1]
    n_t = min(S5_STEPS, seq)
    n_steps = seq // n_t
    rows = n_t * n_b
    u_block = (n_t, n_b * WIDTH) if seq > 1 else (n_b, WIDTH)
    const2 = lambda i: (0, 0)
    const3 = lambda i: (0, 0, 0)
    return pl.pallas_call(
        functools.partial(_s5_kernel, n_b, n_t),
        grid=(n_steps,),
        in_specs=[pl.BlockSpec(u_block, lambda i: (i, 0)),
                  pl.BlockSpec((n_b, n_p), const2), pl.BlockSpec((n_b, n_p), const2),
                  pl.BlockSpec((1, n_p), const2), pl.BlockSpec((1, n_p), const2),
                  pl.BlockSpec(bb4.shape, const3), pl.BlockSpec(cc4.shape, const3),
                  pl.BlockSpec((1, WIDTH), const2)],
        out_specs=[pl.BlockSpec(u_block, lambda i: (i, 0)),
                   pl.BlockSpec((n_b, n_p), const2), pl.BlockSpec((n_b, n_p), const2)],
        out_shape=[jax.ShapeDtypeStruct(u.shape, F32),
                   jax.ShapeDtypeStruct((n_b, n_p), F32), jax.ShapeDtypeStruct((n_b, n_p), F32)],
        scratch_shapes=[pltpu.VMEM((WIDTH // LANES, rows, LANES), F32), pltpu.VMEM((rows, 2 * n_p), F32),
                        pltpu.VMEM((n_b, 2 * n_p), F32)],
        compiler_params=_params(1),
        name="s5_scan",
    )(u, x0_re, x0_im, lb_re, lb_im, bb4, cc4, d_skip.reshape(1, WIDTH))


def _out_kernel(glu, a1_ref, g1_ref, a2_ref, g2_ref, x_ref, gate_ref, w_ref, wg_ref, o_ref):
    a1 = a1_ref[...].astype(F32)
    if glu:
        a1 = a1 * _sigmoid(jnp.dot(a1.astype(BF16), wg_ref[...], preferred_element_type=F32))
    m1 = a1 * g1_ref[...].astype(F32)
    m2 = a2_ref[...].astype(F32) * g2_ref[...].astype(F32)
    y = (jnp.dot(m1.astype(BF16), w_ref[0:WIDTH, :], preferred_element_type=F32)
         + jnp.dot(m2.astype(BF16), w_ref[WIDTH:2 * WIDTH, :], preferred_element_type=F32))
    o_ref[...] = x_ref[...] + gate_ref[...] * y


def _out_proj(a1, g1, a2, g2, x3, gate, w, wg, glu, a1_by_time):
    n_g, rows, d = x3.shape
    tm = min(ROW_TILE, rows)
    nt = rows // tm
    r_mod = gate.shape[1]
    tok = pl.BlockSpec((tm, WIDTH), lambda gi, i: (gi * nt + i, 0))
    a1_spec = pl.BlockSpec((tm, WIDTH), lambda gi, i: (i, gi)) if a1_by_time else tok
    const = lambda gi, i: (0, 0)
    return pl.pallas_call(
        functools.partial(_out_kernel, glu),
        grid=(n_g, nt),
        in_specs=[a1_spec, tok, tok, tok,
                  pl.BlockSpec((None, tm, d), lambda gi, i: (gi, i, 0)),
                  pl.BlockSpec((None, r_mod, d), lambda gi, i: (gi, 0, 0)),
                  pl.BlockSpec(w.shape, const), pl.BlockSpec(wg.shape, const)],
        out_specs=pl.BlockSpec((None, tm, d), lambda gi, i: (gi, i, 0)),
        out_shape=jax.ShapeDtypeStruct(x3.shape, F32),
        compiler_params=_params(2),
        name="out_proj",
    )(a1, g1, a2, g2, x3, gate, w, wg)


def _pad_cols(w, n):
    return jnp.pad(w, ((0, 0), (0, n - w.shape[1])))


def _even_layer(li, x3, mods, norm_g, p, attn_fn, delta_fn, kv_heads):
    shift, scale, gate = mods
    n_g, rows, d = x3.shape
    lam_init = 0.8 - 0.6 * math.exp(-0.3 * li)
    n_cols = p["w_in"].shape[1]
    n_pad = -(-n_cols // LANES) * LANES
    w_in = _pad_cols(p["w_in"], n_pad).astype(BF16)
    kv = "heads" if kv_heads else "rows"
    segs = ((0, WIDTH, "qnorm", BF16, "rows"), (WIDTH, WIDTH, "knorm", F32, kv),
            (2 * WIDTH, WIDTH, "raw", F32, kv), (3 * WIDTH, WIDTH, "silu", BF16, "rows"),
            (4 * WIDTH, 3 * WIDTH, "raw", F32, "rows"), (7 * WIDTH, WIDTH, "silu", BF16, "rows"),
            (8 * WIDTH, LANES, "raw", F32, "rows"))
    bd = _block_diag(jnp.full((WIDTH // DQK, DQK, DQK), 1.0 / DQK, F32)).astype(BF16)
    qg = jnp.tile(p["qn_g"], WIDTH // DQK).reshape(1, WIDTH)
    kg = jnp.tile(p["kn_g"], WIDTH // DQK).reshape(1, WIDTH)
    outs = _in_proj(x3, shift, scale, norm_g, w_in, segs, bd, qg, kg)
    if kv_heads:
        q, k, k_out, v, v_out, za, qkv_b, zb, ab = outs
    else:
        q, k, v, za, qkv_b, zb, ab = outs
        k_out, v_out = k, v
    lamp = jnp.zeros((SUBLANES, LANES), F32)
    for i, name in enumerate(("lam_q1", "lam_k1", "lam_q2", "lam_k2")):
        lamp = lamp.at[i, :DQK].set(p[name])
    sg = p["subln_g"].reshape(1, HEAD)
    oa = attn_fn(q, k, v, lamp=lamp, sg=sg, lam_init=lam_init)
    ob, conv_new, s_new = delta_fn(qkv_b, ab, p["conv_w"], p["a_log"], p["dt_bias"], p["gn_b"])
    w_out = p["w_out"].astype(BF16)
    x_new = _out_proj(oa, za, ob, zb, x3, gate, w_out, jnp.zeros((SUBLANES, LANES), BF16), False, False)
    return x_new, (k_out, v_out, conv_new, s_new)


def _odd_layer(x3, mods, norm_g, p, s5_mats, x0_re, x0_im, r0, by_time, seq_off=0):
    shift, scale, gate = mods
    n_g, rows, d = x3.shape
    w_in = p["w_in"].astype(BF16)
    segs = ((0, WIDTH, "raw", F32, "time" if by_time else "rows"), (WIDTH, WIDTH, "silu", BF16, "rows"),
            (2 * WIDTH, WIDTH, "raw", F32, "rows"), (3 * WIDTH, WIDTH, "raw", F32, "rows"),
            (4 * WIDTH, WIDTH, "raw", F32, "rows"), (5 * WIDTH, WIDTH, "silu", BF16, "rows"))
    dummy = jnp.zeros((SUBLANES, LANES), BF16)
    dummy_g = jnp.zeros((1, LANES), F32)
    u, zc, qd, kd, vd, zd = _in_proj(x3, shift, scale, norm_g, w_in, segs, dummy, dummy_g, dummy_g)
    lb_re, lb_im, bb4, cc4 = s5_mats
    n_b = n_g if by_time else rows
    seq = rows if by_time else 1
    yg, xr, xi = _s5(u, x0_re, x0_im, lb_re, lb_im, bb4, cc4, p["s5_d"], n_b, seq)
    if by_time:
        od, r_new = _retention(qd, kd, vd, p["gn_d"], r0, n_g, rows)
    else:
        od, r_new = _retention_step(qd, kd, vd, p["gn_d"], r0, seq_off)
    x_new = _out_proj(yg, zc, od, zd, x3, gate, p["w_out"].astype(BF16), p["w_glu"].astype(BF16), True, by_time)
    return x_new, (xr, xi, r_new)


def kernel(x_prompt, x_sample, c_prompt, c_sample, page_table, cache_k, cache_v, state_b_conv, state_b_ssm,
           state_c_re, state_c_im, state_d_ret, norm_g, w_ada, b_ada, w_in_e, w_out_e, qn_g, kn_g,
           lam_q1, lam_k1, lam_q2, lam_k2, subln_g, conv_w, a_log, dt_bias, gn_b, w_in_o, w_out_o,
           s5_a_re, s5_a_im, s5_b_re, s5_b_im, s5_c_re, s5_c_im, s5_d, s5_log_dt, w_glu, gn_d):
    n_bp, seq, d = x_prompt.shape
    n_bs = x_sample.shape[0]
    depth = norm_g.shape[0]
    n_pages, page = page_table.shape[1], cache_k.shape[2]
    n_g, n_p = s5_a_re.shape[1], s5_a_re.shape[2]
    conv_all = state_b_conv.reshape(-1, K_CONV - 1, 3 * WIDTH)
    ssm_all = state_b_ssm.reshape(-1, N_HEADS, HEAD, HEAD)
    ret_all = state_d_ret.reshape(-1, N_HEADS, HEAD, HEAD)

    mod = _modulation(jnp.concatenate([c_prompt, c_sample], axis=0), w_ada, b_ada)
    xp = x_prompt
    xs = x_sample.reshape(1, n_bs, d)
    outs = {name: [] for name in ("k_p", "v_p", "k_s", "v_s", "cv_p", "cv_s", "dl_p", "dl_s",
                                  "s5r_p", "s5i_p", "s5r_s", "s5i_s", "rt_p", "rt_s")}
    for li in range(depth):
        mods_p = tuple(mod[li, :n_bp, j * d:(j + 1) * d].reshape(n_bp, 1, d) for j in range(3))
        mods_s = tuple(mod[li, n_bp:, j * d:(j + 1) * d].reshape(1, n_bs, d) for j in range(3))
        if li % 2 == 0:
            e = li // 2
            p = dict(w_in=w_in_e[e], w_out=w_out_e[e], qn_g=qn_g[e], kn_g=kn_g[e], lam_q1=lam_q1[e],
                     lam_k1=lam_k1[e], lam_q2=lam_q2[e], lam_k2=lam_k2[e], subln_g=subln_g[e],
                     conv_w=conv_w[e], a_log=a_log[e], dt_bias=dt_bias[e], gn_b=gn_b[e])
            attn_p = functools.partial(_attn_prompt, n_b=n_bp, seq=seq)
            delta_p = functools.partial(
                _delta, cbuf=jnp.zeros((n_bp, K_CONV - 1, 3 * WIDTH), F32),
                s0=jnp.zeros((n_bp, N_HEADS, HEAD, HEAD), F32), n_b=n_bp, seq=seq)
            xp, (k_p, v_p, c_p, s_p) = _even_layer(li, xp, mods_p, norm_g[li], p, attn_p, delta_p, True)
            attn_s = functools.partial(_attn_decode, cache_k=cache_k, cache_v=cache_v, page_table=page_table,
                                       layer=e)
            delta_s = functools.partial(_delta_step, cbuf=conv_all, s0=ssm_all, seq_off=e * n_bs)
            xs, (k_s, v_s, c_s, s_s) = _even_layer(li, xs, mods_s, norm_g[li], p, attn_s, delta_s, False)
            outs["k_p"].append(k_p.reshape(n_bp, seq, N_HEADS, HEAD))
            outs["v_p"].append(v_p.reshape(n_bp, seq, N_HEADS, HEAD))
            outs["k_s"].append(k_s.reshape(n_bs, 1, N_HEADS, HEAD))
            outs["v_s"].append(v_s.reshape(n_bs, 1, N_HEADS, HEAD))
            outs["cv_p"].append(c_p)
            outs["cv_s"].append(c_s)
            outs["dl_p"].append(s_p)
            outs["dl_s"].append(s_s)
        else:
            o = li // 2
            p = dict(w_in=w_in_o[o], w_out=w_out_o[o], s5_d=s5_d[o], w_glu=w_glu[o], gn_d=gn_d[o])
            lb_re, lb_im, bbr, bbi = _s5_prep(s5_a_re[o], s5_a_im[o], s5_log_dt[o], s5_b_re[o], s5_b_im[o])
            hg = S5_HALF_GROUPS
            bb4 = jnp.stack([_block_diag(t[h * hg:(h + 1) * hg]) for h in range(2) for t in (bbr, bbi)]).astype(BF16)
            cre = jnp.swapaxes(s5_c_re[o], 1, 2)
            cim = -jnp.swapaxes(s5_c_im[o], 1, 2)
            cc4 = jnp.stack([_block_diag(t[h * hg:(h + 1) * hg]) for h in range(2) for t in (cre, cim)]).astype(BF16)
            s5_mats = (lb_re.reshape(1, n_g * n_p), lb_im.reshape(1, n_g * n_p), bb4, cc4)
            zeros_c = jnp.zeros((n_bp, n_g * n_p), F32)
            xp, (r_p, i_p, t_p) = _odd_layer(xp, mods_p, norm_g[li], p, s5_mats, zeros_c, zeros_c,
                                             jnp.zeros((n_bp, N_HEADS, HEAD, HEAD), F32), True)
            xs, (r_s, i_s, t_s) = _odd_layer(xs, mods_s, norm_g[li], p, s5_mats,
                                             state_c_re[o].reshape(n_bs, n_g * n_p),
                                             state_c_im[o].reshape(n_bs, n_g * n_p), ret_all, False, o * n_bs)
            outs["s5r_p"].append(r_p.reshape(n_bp, n_g, n_p))
            outs["s5i_p"].append(i_p.reshape(n_bp, n_g, n_p))
            outs["s5r_s"].append(r_s.reshape(n_bs, n_g, n_p))
            outs["s5i_s"].append(i_s.reshape(n_bs, n_g, n_p))
            outs["rt_p"].append(t_p)
            outs["rt_s"].append(t_s)
    st = lambda name: jnp.stack(outs[name])
    return (xp, xs.reshape(n_bs, 1, d), st("k_p"), st("v_p"), st("k_s"), st("v_s"), st("cv_p"), st("cv_s"),
            st("dl_p"), st("dl_s"), st("s5r_p"), st("s5i_p"), st("s5r_s"), st("s5i_s"), st("rt_p"), st("rt_s"))
```

```python
import functools
import math

import jax
import jax.numpy as jnp
from jax import lax
from jax.experimental import pallas as pl
from jax.experimental.pallas import tpu as pltpu

F32 = jnp.float32
BF16 = jnp.bfloat16
EPS = 1e-6

LANES = 128
SUBLANES = 8
VMEM_LIMIT_BYTES = 48 * 1024 * 1024

HEAD = 128
DQK = HEAD // 2
LOG2E = math.log2(math.e)
N_HEADS = 4
WIDTH = N_HEADS * HEAD
K_CONV = 4
DELTA_CHUNK = 64
GROUP_C = 16
P_C = 64
S5_HALF_GROUPS = 16

ROW_TILE = 256
ATTN_TILE = 512
ATTN_ROW_CHUNK = 256
DELTA_ROWS = 256
DELTA_SEQS = 1
DELTA_STACK = 4
RET_ROWS = 256
S5_STEPS = 64
S5_LANE_GROUP = 512
STEP_SEQS = 8


def _params(n_axes, vmem=VMEM_LIMIT_BYTES):
    return pltpu.CompilerParams(dimension_semantics=("arbitrary",) * n_axes, vmem_limit_bytes=vmem)


def _sigmoid(x):
    return 1.0 / (1.0 + jnp.exp(-x))


def _silu(x):
    return x * _sigmoid(x)


def _softplus(x):
    return jnp.maximum(x, 0.0) + jnp.log1p(jnp.exp(-jnp.abs(x)))


def _gelu_tanh(x):
    return 0.5 * x * (1.0 + jnp.tanh(math.sqrt(2.0 / math.pi) * (x + 0.044715 * (x * x * x))))


def _mm(a, b):
    return jnp.dot(a.astype(BF16), b.astype(BF16), preferred_element_type=F32)


def _mm_nt(a, b):
    return lax.dot_general(a.astype(BF16), b.astype(BF16), (((1,), (1,)), ((), ())),
                           preferred_element_type=F32)


def _mod_kernel(c_ref, w_ref, b_ref, o_ref):
    o_ref[...] = _mm(_silu(c_ref[...]), w_ref[...]) + b_ref[...]


def _modulation(c_all, w_ada, b_ada):
    depth, d, n3 = w_ada.shape
    rows = c_all.shape[0]
    tn = d
    return pl.pallas_call(
        _mod_kernel,
        grid=(depth, n3 // tn),
        in_specs=[pl.BlockSpec((rows, d), lambda l, n: (0, 0)),
                  pl.BlockSpec((None, d, tn), lambda l, n: (l, 0, n)),
                  pl.BlockSpec((None, 1, tn), lambda l, n: (l, 0, n))],
        out_specs=pl.BlockSpec((None, rows, tn), lambda l, n: (l, 0, n)),
        out_shape=jax.ShapeDtypeStruct((depth, rows, n3), F32),
        compiler_params=_params(2),
        name="modulation",
    )(c_all, w_ada, b_ada.reshape(depth, 1, n3))


def _in_kernel(segs, x_ref, shift_ref, scale_ref, g_ref, w_ref, bd_ref, qg_ref, kg_ref, *refs):
    out_refs, h_ref = iter(refs[:-1]), refs[-1]
    x = x_ref[...]
    h = x * lax.rsqrt(jnp.mean(x * x, axis=-1, keepdims=True) + EPS) * g_ref[...]
    h_ref[...] = (h * (1.0 + scale_ref[...]) + shift_ref[...]).astype(BF16)
    for c0, width, kind, layout in segs:
        acc = jnp.dot(h_ref[...], w_ref[:, c0:c0 + width], preferred_element_type=F32)
        if kind == "silu":
            acc = _silu(acc)
        elif kind in ("qnorm", "knorm"):
            ms = jnp.dot((acc * acc).astype(BF16), bd_ref[...], preferred_element_type=F32)
            acc = acc * lax.rsqrt(ms + EPS) * (qg_ref if kind == "qnorm" else kg_ref)[...]
            if kind == "qnorm":
                acc = acc * (DQK ** -0.5 * LOG2E)
        o_ref = next(out_refs)
        o_ref[...] = acc.astype(o_ref.dtype)
        if layout == "heads":
            hm_ref = next(out_refs)
            n_h = width // HEAD
            for h in range(n_h):
                hm_ref[pl.ds(h, acc.shape[0], stride=n_h), :] = acc[:, h * HEAD:(h + 1) * HEAD]


def _in_proj(x3, shift, scale, g, w, segs, bd, qg, kg):
    n_g, rows, d = x3.shape
    tm = min(ROW_TILE, rows)
    nt = rows // tm
    r_mod = shift.shape[1]
    assert r_mod in (1, tm)
    out_shapes, out_specs = [], []
    row_map = lambda gi, i: (gi * nt + i, 0)
    for _, width, _, dtype, layout in segs:
        if layout == "time":
            out_shapes.append(jax.ShapeDtypeStruct((rows, n_g * width), dtype))
            out_specs.append(pl.BlockSpec((tm, width), lambda gi, i: (i, gi)))
        elif layout == "heads":
            n_h = width // HEAD
            out_shapes += [jax.ShapeDtypeStruct((n_g * rows, width), BF16),
                           jax.ShapeDtypeStruct((n_g * rows * n_h, HEAD), dtype)]
            out_specs += [pl.BlockSpec((tm, width), row_map), pl.BlockSpec((tm * n_h, HEAD), row_map)]
        else:
            out_shapes.append(jax.ShapeDtypeStruct((n_g * rows, width), dtype))
            out_specs.append(pl.BlockSpec((tm, width), row_map))
    const = lambda gi, i: (0, 0)
    return pl.pallas_call(
        functools.partial(_in_kernel, tuple((s[0], s[1], s[2], s[4]) for s in segs)),
        grid=(n_g, nt),
        in_specs=[pl.BlockSpec((None, tm, d), lambda gi, i: (gi, i, 0)),
                  pl.BlockSpec((None, r_mod, d), lambda gi, i: (gi, 0, 0)),
                  pl.BlockSpec((None, r_mod, d), lambda gi, i: (gi, 0, 0)),
                  pl.BlockSpec((1, d), const),
                  pl.BlockSpec(w.shape, const),
                  pl.BlockSpec(bd.shape, const),
                  pl.BlockSpec(qg.shape, const),
                  pl.BlockSpec(kg.shape, const)],
        out_specs=out_specs,
        out_shape=out_shapes,
        scratch_shapes=[pltpu.VMEM((tm, d), BF16)],
        compiler_params=_params(2),
        name="in_proj",
    )(x3, shift, scale, g.reshape(1, d), w, bd, qg, kg)


def _lambda_value(lamp, lam_init):
    a = jnp.sum(lamp[0:1] * lamp[1:2], axis=-1, keepdims=True)
    b = jnp.sum(lamp[2:3] * lamp[3:4], axis=-1, keepdims=True)
    return jnp.exp(a) - jnp.exp(b) + lam_init


def _sub_ln(o, gain, lam_init):
    return o * lax.rsqrt(jnp.mean(o * o, axis=-1, keepdims=True) + EPS) * gain * (1.0 - lam_init)


def _attn_kernel(lam_init, row_chunk, qt_ref, kt_ref, q_ref, k_ref, v_ref, kbias_ref, lamp_ref, sg_ref, o_ref,
                 qs_ref, m_ref, l_ref, acc_ref):
    qi, ki = qt_ref[pl.program_id(2)], kt_ref[pl.program_id(2)]
    tq, tk = q_ref.shape[0], k_ref.shape[0]

    @pl.when(ki == 0)
    def _init():
        m_ref[...] = jnp.full(m_ref.shape, -jnp.inf, F32)
        l_ref[...] = jnp.zeros(l_ref.shape, F32)
        acc_ref[...] = jnp.zeros(acc_ref.shape, F32)
        q = q_ref[...]
        lane = lax.broadcasted_iota(jnp.int32, (1, HEAD), 1)
        qs_ref[0:tq, :] = jnp.where(lane < DQK, q, jnp.zeros_like(q))
        qs_ref[tq:2 * tq, :] = jnp.where(lane >= DQK, q, jnp.zeros_like(q))

    def step(masked):
        kb = k_ref[...].astype(BF16)
        vb = v_ref[...].astype(BF16)
        kbias = kbias_ref[...]
        for r0 in range(0, 2 * tq, row_chunk):
            rs = slice(r0, r0 + row_chunk)
            n_k = min(tk, r0 % tq + row_chunk) if masked else tk
            s = _mm_nt(qs_ref[rs, :], kb[:n_k]) + kbias[:, :n_k]
            if masked:
                q_row = lax.broadcasted_iota(jnp.int32, (row_chunk, n_k), 0) + (r0 % tq)
                col = lax.broadcasted_iota(jnp.int32, (row_chunk, n_k), 1)
                s = jnp.where(col <= q_row, s, -jnp.inf)
            m_prev = m_ref[rs, :]
            m_new = jnp.maximum(m_prev, jnp.max(s, axis=-1, keepdims=True))
            alpha = jnp.exp2(m_prev - m_new)
            p = jnp.exp2(s - m_new)
            l_ref[rs, :] = alpha * l_ref[rs, :] + jnp.sum(p, axis=-1, keepdims=True)
            acc_ref[rs, :] = alpha * acc_ref[rs, :] + jnp.dot(p.astype(BF16), vb[:n_k], preferred_element_type=F32)
            m_ref[rs, :] = m_new

    @pl.when(ki < qi)
    def _off_diagonal():
        step(False)

    @pl.when(ki == qi)
    def _diagonal():
        step(True)
        lam = _lambda_value(lamp_ref[...], lam_init)
        o = acc_ref[0:tq, :] / l_ref[0:tq, :] - lam * (acc_ref[tq:2 * tq, :] / l_ref[tq:2 * tq, :])
        o_ref[...] = _sub_ln(o, sg_ref[...], lam_init).astype(o_ref.dtype)


def _alibi_slopes(n_heads):
    return jnp.exp2(-8.0 * jnp.arange(1, n_heads + 1, dtype=F32) / n_heads)


def _attn_prompt(q, k, v, lamp, sg, lam_init, n_b, seq):
    t = min(ATTN_TILE, seq)
    nq = seq // t
    row_chunk = min(ATTN_ROW_CHUNK, t)
    kbias = (LOG2E * _alibi_slopes(N_HEADS)[:, None] * jnp.arange(seq, dtype=F32)[None, :]).reshape(N_HEADS, nq, 1, t)
    pairs = [(qi, ki) for qi in range(nq) for ki in range(qi + 1)]
    qt = jnp.asarray([p[0] for p in pairs], jnp.int32)
    kt = jnp.asarray([p[1] for p in pairs], jnp.int32)
    q_map = lambda b, h, t_, qt_, kt_: (b * nq + qt_[t_], h)
    kv_map = lambda b, h, t_, qt_, kt_: (b * nq + kt_[t_], h)
    const = lambda b, h, t_, qt_, kt_: (0, 0)
    grid_spec = pltpu.PrefetchScalarGridSpec(
        num_scalar_prefetch=2,
        grid=(n_b, N_HEADS, len(pairs)),
        in_specs=[pl.BlockSpec((t, HEAD), q_map),
                  pl.BlockSpec((t, HEAD), kv_map),
                  pl.BlockSpec((t, HEAD), kv_map),
                  pl.BlockSpec((None, None, 1, t), lambda b, h, t_, qt_, kt_: (h, kt_[t_], 0, 0)),
                  pl.BlockSpec(lamp.shape, const),
                  pl.BlockSpec((1, HEAD), const)],
        out_specs=pl.BlockSpec((t, HEAD), q_map),
        scratch_shapes=[pltpu.VMEM((2 * t, HEAD), BF16), pltpu.VMEM((2 * t, 1), F32),
                        pltpu.VMEM((2 * t, 1), F32), pltpu.VMEM((2 * t, HEAD), F32)],
    )
    return pl.pallas_call(
        functools.partial(_attn_kernel, lam_init, row_chunk),
        grid_spec=grid_spec,
        out_shape=jax.ShapeDtypeStruct((n_b * seq, WIDTH), BF16),
        compiler_params=_params(3),
        name="attn_prompt",
    )(qt, kt, q, k, v, kbias, lamp, sg)


def _decode_kernel(lam_init, n_pages, pt_ref, q_ref, kn_ref, vn_ref, nsl_ref, lamp_ref, sg_ref, *refs):
    del pt_ref
    k_refs, v_refs, o_ref = refs[:n_pages], refs[n_pages:2 * n_pages], refs[2 * n_pages]
    cols_page = k_refs[0].shape[0]
    n_rows = 4 * N_HEADS
    row = lax.broadcasted_iota(jnp.int32, (n_rows, HEAD), 0)
    lane = lax.broadcasted_iota(jnp.int32, (n_rows, HEAD), 1)

    def per_row_head(x):
        out = jnp.zeros((n_rows, HEAD), F32)
        for h in range(N_HEADS):
            out = jnp.where((row % N_HEADS == h) & (row < 2 * N_HEADS), x[:, h * HEAD:(h + 1) * HEAD], out)
        return out

    qm = jnp.where(lane // DQK == row // N_HEADS, per_row_head(q_ref[...].astype(F32)), 0.0).astype(BF16)
    s = jnp.concatenate([_mm_nt(qm, k_refs[j][...]) for j in range(n_pages)], axis=1)
    n_cols = n_pages * cols_page
    past = n_cols // N_HEADS
    col = lax.broadcasted_iota(jnp.int32, (n_rows, n_cols), 1)
    row_c = lax.broadcasted_iota(jnp.int32, (n_rows, n_cols), 0)
    own_head = (col % N_HEADS == row_c % N_HEADS) & (row_c < 2 * N_HEADS)
    s = jnp.where(own_head, s + nsl_ref[...] * (past - col // N_HEADS).astype(F32), -jnp.inf)
    s_self = jnp.sum(qm.astype(F32) * per_row_head(kn_ref[...]).astype(BF16).astype(F32), axis=-1, keepdims=True)
    m = jnp.maximum(jnp.max(s, axis=-1, keepdims=True), s_self)
    p = jnp.exp2(s - m)
    p_self = jnp.exp2(s_self - m)
    denom = jnp.sum(p, axis=-1, keepdims=True) + p_self
    lam = _lambda_value(lamp_ref[...], lam_init)
    r1 = row[:, 0:1]
    coef = jnp.where(r1 < N_HEADS, 1.0, jnp.where(r1 < 2 * N_HEADS, -lam, 0.0)) / denom
    pw = p * coef
    acc = (p_self * coef) * per_row_head(vn_ref[...]).astype(BF16).astype(F32)
    for j in range(n_pages):
        acc = acc + _mm(pw[:, j * cols_page:(j + 1) * cols_page], v_refs[j][...])
    for h in range(N_HEADS):
        o = acc[h:h + 1, :] + acc[N_HEADS + h:N_HEADS + h + 1, :]
        o_ref[:, h * HEAD:(h + 1) * HEAD] = _sub_ln(o, sg_ref[...], lam_init)


def _attn_decode(q, k_new, v_new, cache_k, cache_v, page_table, lamp, sg, lam_init, layer):
    n_s, n_pages = page_table.shape
    n_l, n_phys, page = cache_k.shape[:3]
    cache_k = cache_k.reshape(n_l, n_phys, page * N_HEADS, HEAD)
    cache_v = cache_v.reshape(n_l, n_phys, page * N_HEADS, HEAD)
    nsl = -LOG2E * _alibi_slopes(N_HEADS)
    nsl8 = jnp.concatenate([nsl, nsl, jnp.zeros((2 * N_HEADS,), F32)]).reshape(4 * N_HEADS, 1)
    row_spec = pl.BlockSpec((None, 1, WIDTH), lambda b, pt: (b, 0, 0))
    const = lambda b, pt: (0, 0)

    def page_spec(j):
        return pl.BlockSpec((None, None, page * N_HEADS, HEAD), lambda b, pt: (layer, pt[b, j], 0, 0))

    grid_spec = pltpu.PrefetchScalarGridSpec(
        num_scalar_prefetch=1,
        grid=(n_s,),
        in_specs=[row_spec, row_spec, row_spec,
                  pl.BlockSpec(nsl8.shape, const), pl.BlockSpec(lamp.shape, const),
                  pl.BlockSpec((1, HEAD), const)]
                 + [page_spec(j) for j in range(n_pages)] * 2,
        out_specs=row_spec,
    )
    out = pl.pallas_call(
        functools.partial(_decode_kernel, lam_init, n_pages),
        grid_spec=grid_spec,
        out_shape=jax.ShapeDtypeStruct((n_s, 1, WIDTH), F32),
        compiler_params=_params(1),
        name="attn_decode",
    )(page_table, q.reshape(n_s, 1, WIDTH), k_new.reshape(n_s, 1, WIDTH), v_new.reshape(n_s, 1, WIDTH),
      nsl8, lamp, sg, *([cache_k] * n_pages), *([cache_v] * n_pages))
    return out.reshape(n_s, WIDTH)


INV_BASE = SUBLANES


def _unit_lower_inverse(a, block, row, col):
    in_base = row // INV_BASE == col // INV_BASE
    power = jnp.where(in_base, a, 0.0)
    inv = jnp.where(row == col, 1.0, 0.0) - power
    order = 2
    while order < INV_BASE:
        power = _mm(power, power)
        inv = inv + _mm(inv, power)
        order *= 2
    size = INV_BASE
    while size < block:
        coupling = jnp.where((row // (2 * size) == col // (2 * size)) & (row // size != col // size), a, 0.0)
        inv = inv - _mm(_mm(inv, coupling), inv)
        size *= 2
    return inv


def _mm_tn(a, b):
    return lax.dot_general(a.astype(BF16), b.astype(BF16), (((0,), (0,)), ((), ())),
                           preferred_element_type=F32)


def _delta_kernel(chunk, qkv_ref, ab_ref, cw_ref, hp_ref, gn_ref, cbuf_ref, s0_ref, tri_ref,
                  o_ref, cnew_ref, sout_ref, xs_ref, st_ref):
    r = pl.program_id(1)
    n_seq = qkv_ref.shape[0]
    tail = K_CONV - 1

    @pl.when(r == 0)
    def _init():
        st_ref[...] = s0_ref[...]
        for i in range(n_seq):
            xs_ref[i, 0:SUBLANES, :] = jnp.zeros((SUBLANES, xs_ref.shape[2]), F32)
            xs_ref[i, SUBLANES - tail:SUBLANES, :] = cbuf_ref[i]

    for i in range(n_seq):
        _delta_sequence(chunk, qkv_ref.at[i], ab_ref.at[i], cw_ref, hp_ref, gn_ref, tri_ref,
                        o_ref.at[i], cnew_ref.at[i], xs_ref.at[i], st_ref.at[i])

    @pl.when(r == pl.num_programs(1) - 1)
    def _finish():
        sout_ref[...] = st_ref[...]


def _delta_sequence(chunk, qkv_ref, ab_ref, cw_ref, hp_ref, gn_ref, tri_ref, o_ref, cnew_ref, xs_ref, st_ref):
    rows = qkv_ref.shape[0]
    n_chunks = rows // chunk
    tail = K_CONV - 1
    xs_ref[SUBLANES:SUBLANES + rows, :] = qkv_ref[...]
    w = cw_ref[...]
    y = w[0:1] * xs_ref[SUBLANES - 3:SUBLANES - 3 + rows, :]
    for j in range(1, K_CONV):
        y = y + w[j:j + 1] * xs_ref[SUBLANES - 3 + j:SUBLANES - 3 + j + rows, :]
    new_tail = xs_ref[SUBLANES + rows - tail:SUBLANES + rows, :]
    cnew_ref[...] = new_tail
    xs_ref[SUBLANES - tail:SUBLANES, :] = new_tail
    act = _silu(y)

    ab = ab_ref[...]
    hp = hp_ref[...]
    g_all = -jnp.exp(hp[0:1]) * _softplus(ab + hp[1:2])
    beta_all = _sigmoid(ab)
    gam_all = jnp.dot(tri_ref[...], g_all, preferred_element_type=F32, precision=lax.Precision.HIGHEST)

    n_st = DELTA_STACK * chunk
    ri = lax.broadcasted_iota(jnp.int32, (n_st, n_st), 0)
    ci = lax.broadcasted_iota(jnp.int32, (n_st, n_st), 1)
    same_head = ri // chunk == ci // chunk
    incl, strict, diag = same_head & (ri >= ci), same_head & (ri > ci), ri == ci
    gn = gn_ref[...]
    local = [slice(j * chunk, (j + 1) * chunk) for j in range(DELTA_STACK)]
    states = [st_ref[h] for h in range(N_HEADS)]

    for c in range(n_chunks):
        rs = slice(c * chunk, (c + 1) * chunk)
        for h0 in range(0, N_HEADS, DELTA_STACK):
            hg = range(h0, h0 + DELTA_STACK)

            def stacked(col0):
                return jnp.concatenate([act[rs, col0 + h * HEAD:col0 + (h + 1) * HEAD] for h in hg], axis=0)

            q, k, v = stacked(0), stacked(WIDTH), stacked(2 * WIDTH)
            q = q * lax.rsqrt(jnp.sum(q * q, axis=-1, keepdims=True) + EPS) * HEAD ** -0.5
            k = k * lax.rsqrt(jnp.sum(k * k, axis=-1, keepdims=True) + EPS)
            gc = jnp.concatenate([gam_all[rs, h:h + 1] for h in hg], axis=0)
            bc = jnp.concatenate([beta_all[rs, N_HEADS + h:N_HEADS + h + 1] for h in hg], axis=0)
            gr = jnp.sum(jnp.where(diag, gc, 0.0), axis=0, keepdims=True)
            decay = jnp.exp(jnp.where(incl, gc - gr, -jnp.inf))
            inv = _unit_lower_inverse(jnp.where(strict, bc * decay * _mm_nt(k, k), 0.0), chunk, ri, ci)
            eg = jnp.exp(gc)
            sol = _mm(inv, jnp.concatenate([bc * v, (bc * eg) * k], axis=1))
            qk = _mm_nt(q, k) * decay
            u = jnp.concatenate([sol[ls, :HEAD] - _mm(sol[ls, HEAD:], states[h]) for h, ls in zip(hg, local)], axis=0)
            o = eg * jnp.concatenate([_mm(q[ls], states[h]) for h, ls in zip(hg, local)], axis=0) + _mm(qk, u)
            for h, ls in zip(hg, local):
                gl = gc[ls.stop - 1:ls.stop, :]
                states[h] = jnp.exp(gl) * states[h] + _mm_tn(k[ls] * jnp.exp(gl - gc[ls]), u[ls])
            on = o * lax.rsqrt(jnp.mean(o * o, axis=-1, keepdims=True) + EPS) * gn
            for h, ls in zip(hg, local):
                o_ref[rs, h * HEAD:(h + 1) * HEAD] = on[ls].astype(o_ref.dtype)

    for h in range(N_HEADS):
        st_ref[h] = states[h]


def _chunk_tri(rows, chunk):
    r = jnp.arange(rows)
    return ((r[:, None] >= r[None, :]) & (r[:, None] // chunk == r[None, :] // chunk)).astype(F32)


def _head_params(a_log, dt_bias):
    return jnp.zeros((SUBLANES, LANES), F32).at[0, :N_HEADS].set(a_log).at[1, :N_HEADS].set(dt_bias)


def _delta(qkv, ab, conv_w, a_log, dt_bias, gn, cbuf, s0, n_b, seq):
    chunk = min(DELTA_CHUNK, seq)
    rows = min(DELTA_ROWS, seq)
    assert seq % rows == 0 and rows % chunk == 0
    nr = seq // rows
    nbb = math.gcd(DELTA_SEQS, n_b)
    width3 = 3 * WIDTH
    hp = _head_params(a_log, dt_bias)
    tri = _chunk_tri(rows, chunk)
    const = lambda b, r: (0, 0)
    tok = lambda width: pl.BlockSpec((nbb, rows, width), lambda b, r: (b, r, 0))
    tail = pl.BlockSpec((nbb, K_CONV - 1, width3), lambda b, r: (b, 0, 0))
    st = pl.BlockSpec((nbb, N_HEADS, HEAD, HEAD), lambda b, r: (b, 0, 0, 0))
    o, cnew, s_new = pl.pallas_call(
        functools.partial(_delta_kernel, chunk),
        grid=(n_b // nbb, nr),
        in_specs=[tok(width3), tok(LANES),
                  pl.BlockSpec((K_CONV, width3), const),
                  pl.BlockSpec((SUBLANES, LANES), const),
                  pl.BlockSpec((1, HEAD), const),
                  tail, st,
                  pl.BlockSpec((rows, rows), const)],
        out_specs=[tok(WIDTH), tail, st],
        out_shape=[jax.ShapeDtypeStruct((n_b, seq, WIDTH), BF16),
                   jax.ShapeDtypeStruct((n_b, K_CONV - 1, width3), F32),
                   jax.ShapeDtypeStruct((n_b, N_HEADS, HEAD, HEAD), F32)],
        scratch_shapes=[pltpu.VMEM((nbb, rows + SUBLANES, width3), F32),
                        pltpu.VMEM((nbb, N_HEADS, HEAD, HEAD), F32)],
        compiler_params=_params(2),
        name="delta_rule",
    )(qkv.reshape(n_b, seq, width3), ab.reshape(n_b, seq, LANES), conv_w, hp, gn.reshape(1, HEAD), cbuf, s0, tri)
    return o.reshape(n_b * seq, WIDTH), cnew, s_new


def _eye(n):
    return lax.broadcasted_iota(jnp.int32, (n, n), 0) == lax.broadcasted_iota(jnp.int32, (n, n), 1)


def _delta_step_kernel(x_ref, ab_ref, cw_ref, hp_ref, gn_ref, cbuf_ref, s0_ref, o_ref, cnew_ref, sout_ref):
    w = cw_ref[...]
    hp = hp_ref[...]
    x = x_ref[...]
    buf = cbuf_ref[...]
    y = w[0:1] * buf[:, 0:1] + w[1:2] * buf[:, 1:2] + w[2:3] * buf[:, 2:3] + w[3:4] * x
    cnew_ref[:, 0:2, :] = buf[:, 1:3]
    cnew_ref[:, 2:3, :] = x
    act = _silu(y)
    ab = ab_ref[...]
    decay_all = jnp.exp(-jnp.exp(hp[0:1]) * _softplus(ab + hp[1:2]))
    beta_all = _sigmoid(ab)

    def per_head(t, col0, width):
        return jnp.stack([t[:, :, col0 + h * width:col0 + (h + 1) * width] for h in range(N_HEADS)], axis=1)

    q, k, v = per_head(act, 0, HEAD), per_head(act, WIDTH, HEAD), per_head(act, 2 * WIDTH, HEAD)
    q = q * lax.rsqrt(jnp.sum(q * q, axis=-1, keepdims=True) + EPS) * HEAD ** -0.5
    k = k * lax.rsqrt(jnp.sum(k * k, axis=-1, keepdims=True) + EPS)
    a = per_head(decay_all, 0, 1)
    beta = per_head(beta_all, N_HEADS, 1)
    eye = _eye(HEAD)
    k_col = jnp.sum(jnp.where(eye, k, 0.0), axis=-1, keepdims=True)
    q_col = jnp.sum(jnp.where(eye, q, 0.0), axis=-1, keepdims=True)
    state = s0_ref[...]
    u = beta * (v - a * jnp.sum(state * k_col, axis=2, keepdims=True))
    state = a * state + k_col * u
    sout_ref[...] = state
    o = jnp.sum(state * q_col, axis=2, keepdims=True)
    o = o * lax.rsqrt(jnp.mean(o * o, axis=-1, keepdims=True) + EPS) * gn_ref[...]
    for h in range(N_HEADS):
        o_ref[:, :, h * HEAD:(h + 1) * HEAD] = o[:, h]


def _step_block(n_s, seq_off):
    return math.gcd(STEP_SEQS, math.gcd(n_s, seq_off))


def _delta_step(qkv, ab, conv_w, a_log, dt_bias, gn, cbuf, s0, seq_off):
    n_s = qkv.shape[0]
    nb = _step_block(n_s, seq_off)
    off = seq_off // nb
    width3 = 3 * WIDTH
    const = lambda b: (0, 0)
    row = lambda width: pl.BlockSpec((nb, 1, width), lambda b: (b, 0, 0))
    tail = pl.BlockSpec((nb, K_CONV - 1, width3), lambda b: (b, 0, 0))
    st = pl.BlockSpec((nb, N_HEADS, HEAD, HEAD), lambda b: (b, 0, 0, 0))
    tail_in = pl.BlockSpec((nb, K_CONV - 1, width3), lambda b: (b + off, 0, 0))
    st_in = pl.BlockSpec((nb, N_HEADS, HEAD, HEAD), lambda b: (b + off, 0, 0, 0))
    o, cnew, s_new = pl.pallas_call(
        _delta_step_kernel,
        grid=(n_s // nb,),
        in_specs=[row(width3), row(LANES), pl.BlockSpec((K_CONV, width3), const),
                  pl.BlockSpec((SUBLANES, LANES), const), pl.BlockSpec((1, HEAD), const), tail_in, st_in],
        out_specs=[row(WIDTH), tail, st],
        out_shape=[jax.ShapeDtypeStruct((n_s, 1, WIDTH), F32),
                   jax.ShapeDtypeStruct((n_s, K_CONV - 1, width3), F32),
                   jax.ShapeDtypeStruct((n_s, N_HEADS, HEAD, HEAD), F32)],
        compiler_params=_params(1),
        name="delta_step",
    )(qkv.reshape(n_s, 1, width3), ab.reshape(n_s, 1, LANES), conv_w, _head_params(a_log, dt_bias),
      gn.reshape(1, HEAD), cbuf, s0)
    return o.reshape(n_s, WIDTH), cnew, s_new


def _log_gamma(h):
    return math.log1p(-(2.0 ** (-5.0 - h)))


def _layer_norm(o, gain):
    oc = o - jnp.mean(o, axis=-1, keepdims=True)
    return oc * lax.rsqrt(jnp.mean(oc * oc, axis=-1, keepdims=True) + EPS) * gain


def _ret_kernel(q_ref, k_ref, v_ref, gn_ref, s0_ref, o_ref, sout_ref, st_ref):
    r = pl.program_id(1)
    rows = q_ref.shape[0]

    @pl.when(r == 0)
    def _init():
        st_ref[...] = s0_ref[...]

    ri = lax.broadcasted_iota(jnp.int32, (rows, 1), 0)
    ci = lax.broadcasted_iota(jnp.int32, (1, rows), 1)
    cnt_r = (ri + 1).astype(F32)
    cnt_c = (ci + 1).astype(F32)
    for h in range(N_HEADS):
        log_gamma = _log_gamma(h)
        cols = slice(h * HEAD, (h + 1) * HEAD)
        gc = cnt_r * log_gamma
        decay = jnp.exp(jnp.where(ri >= ci, gc - cnt_c * log_gamma, -jnp.inf))
        qh = q_ref[:, cols]
        kh = k_ref[:, cols] * HEAD ** -0.5
        vh = v_ref[:, cols]
        state = st_ref[h]
        o = jnp.exp(gc) * _mm(qh, state) + _mm(_mm_nt(qh, kh) * decay, vh)
        gl = rows * log_gamma
        st_ref[h] = math.exp(gl) * state + _mm_tn(kh * jnp.exp(gl - gc), vh)
        o_ref[:, cols] = _layer_norm(o, gn_ref[...]).astype(o_ref.dtype)

    @pl.when(r == pl.num_programs(1) - 1)
    def _finish():
        sout_ref[...] = st_ref[...]


def _ret_step_kernel(q_ref, k_ref, v_ref, gn_ref, s0_ref, o_ref, sout_ref):
    def per_head(ref):
        return jnp.stack([ref[:, :, h * HEAD:(h + 1) * HEAD] for h in range(N_HEADS)], axis=1)

    q, k, v = per_head(q_ref), per_head(k_ref) * HEAD ** -0.5, per_head(v_ref)
    eye = _eye(HEAD)
    k_col = jnp.sum(jnp.where(eye, k, 0.0), axis=-1, keepdims=True)
    q_col = jnp.sum(jnp.where(eye, q, 0.0), axis=-1, keepdims=True)
    s0 = s0_ref[...]
    state = jnp.stack([math.exp(_log_gamma(h)) * s0[:, h] for h in range(N_HEADS)], axis=1) + k_col * v
    sout_ref[...] = state
    o = _layer_norm(jnp.sum(state * q_col, axis=2, keepdims=True), gn_ref[...])
    for h in range(N_HEADS):
        o_ref[:, :, h * HEAD:(h + 1) * HEAD] = o[:, h]


def _retention_step(q, k, v, gn, s0, seq_off):
    n_s = q.shape[0]
    nb = _step_block(n_s, seq_off)
    off = seq_off // nb
    row = pl.BlockSpec((nb, 1, WIDTH), lambda b: (b, 0, 0))
    st = pl.BlockSpec((nb, N_HEADS, HEAD, HEAD), lambda b: (b, 0, 0, 0))
    st_in = pl.BlockSpec((nb, N_HEADS, HEAD, HEAD), lambda b: (b + off, 0, 0, 0))
    r3 = lambda t: t.reshape(n_s, 1, WIDTH)
    o, s_new = pl.pallas_call(
        _ret_step_kernel,
        grid=(n_s // nb,),
        in_specs=[row, row, row, pl.BlockSpec((1, HEAD), lambda b: (0, 0)), st_in],
        out_specs=[row, st],
        out_shape=[jax.ShapeDtypeStruct((n_s, 1, WIDTH), F32),
                   jax.ShapeDtypeStruct((n_s, N_HEADS, HEAD, HEAD), F32)],
        compiler_params=_params(1),
        name="retention_step",
    )(r3(q), r3(k), r3(v), gn.reshape(1, HEAD), s0)
    return o.reshape(n_s, WIDTH), s_new


def _retention(q, k, v, gn, s0, n_b, seq):
    rows = min(RET_ROWS, seq)
    assert seq % rows == 0
    nr = seq // rows
    tok = pl.BlockSpec((rows, WIDTH), lambda b, r: (b * nr + r, 0))
    st = pl.BlockSpec((None, N_HEADS, HEAD, HEAD), lambda b, r: (b, 0, 0, 0))
    return pl.pallas_call(
        _ret_kernel,
        grid=(n_b, nr),
        in_specs=[tok, tok, tok, pl.BlockSpec((1, HEAD), lambda b, r: (0, 0)), st],
        out_specs=[pl.BlockSpec((rows, WIDTH), lambda b, r: (b * nr + r, 0)), st],
        out_shape=[jax.ShapeDtypeStruct((n_b * seq, WIDTH), BF16),
                   jax.ShapeDtypeStruct((n_b, N_HEADS, HEAD, HEAD), F32)],
        scratch_shapes=[pltpu.VMEM((N_HEADS, HEAD, HEAD), F32)],
        compiler_params=_params(2),
        name="retention",
    )(q, k, v, gn.reshape(1, HEAD), s0)


def _s5_prep_kernel(are_ref, aim_ref, ldt_ref, brt_ref, bit_ref, lbr_ref, lbi_ref, bbr_ref, bbi_ref):
    dt = jnp.exp(ldt_ref[...])
    ar, ai = are_ref[...], aim_ref[...]
    mag = jnp.exp(ar * dt)
    ang = ai * dt
    lr, li = mag * jnp.cos(ang), mag * jnp.sin(ang)
    den = ar * ar + ai * ai
    fr = ((lr - 1.0) * ar + li * ai) / den
    fi = (li * ar - (lr - 1.0) * ai) / den
    lbr_ref[...] = lr
    lbi_ref[...] = li
    brt, bit = brt_ref[...], bit_ref[...]
    bbr_ref[...] = fr[:, None, :] * brt - fi[:, None, :] * bit
    bbi_ref[...] = fr[:, None, :] * bit + fi[:, None, :] * brt


def _s5_prep(a_re, a_im, log_dt, b_re, b_im):
    n_g, n_p = a_re.shape
    brt, bit = jnp.swapaxes(b_re, 1, 2), jnp.swapaxes(b_im, 1, 2)
    gp = jax.ShapeDtypeStruct((n_g, n_p), F32)
    gcp = jax.ShapeDtypeStruct(brt.shape, F32)
    return pl.pallas_call(_s5_prep_kernel, out_shape=[gp, gp, gcp, gcp], name="s5_prep")(
        a_re, a_im, log_dt.reshape(n_g, 1), brt, bit)


def _block_diag(blocks):
    n, r, c = blocks.shape
    return jnp.einsum("grc,gh->grhc", blocks, jnp.eye(n, dtype=blocks.dtype)).reshape(n * r, n * c)


def _s5_kernel(n_b, n_t, u_ref, x0r_ref, x0i_ref, lbr_ref, lbi_ref, bb_ref, cc_ref, d_ref,
               y_ref, xr_ref, xi_ref, utb_ref, x_ref, carry_ref):
    i = pl.program_id(0)
    n_p = lbr_ref.shape[1]
    half_in = S5_HALF_GROUPS * GROUP_C
    half_st = S5_HALF_GROUPS * P_C

    @pl.when(i == 0)
    def _init():
        carry_ref[:, :n_p] = x0r_ref[...]
        carry_ref[:, n_p:] = x0i_ref[...]

    n_lc = WIDTH // LANES
    if n_t > 1:
        for b in range(n_b):
            for c in range(n_lc):
                utb_ref[c, pl.ds(b, n_t, stride=n_b), :] = u_ref[:, b * WIDTH + c * LANES:b * WIDTH + (c + 1) * LANES]
        u = jnp.concatenate([utb_ref[c] for c in range(n_lc)], axis=1)
    else:
        u = u_ref[...]
    ub = u.astype(BF16)
    for half in range(2):
        for part in range(2):
            c0 = part * n_p + half * half_st
            x_ref[:, c0:c0 + half_st] = jnp.dot(ub[:, half * half_in:(half + 1) * half_in],
                                                bb_ref[half * 2 + part], preferred_element_type=F32)

    if n_t == 1:
        lr, li = lbr_ref[...], lbi_ref[...]
        x0r, x0i = carry_ref[:, :n_p], carry_ref[:, n_p:]
        xr = lr * x0r - li * x0i + x_ref[:, :n_p]
        xi = lr * x0i + li * x0r + x_ref[:, n_p:]
        x_ref[:, :n_p] = xr
        x_ref[:, n_p:] = xi
        carry_ref[:, :n_p] = xr
        carry_ref[:, n_p:] = xi
    else:
        for lg in range(n_p // S5_LANE_GROUP):
            l0 = lg * S5_LANE_GROUP
            lr = jnp.broadcast_to(lbr_ref[:, l0:l0 + S5_LANE_GROUP], (n_b, S5_LANE_GROUP))
            li = jnp.broadcast_to(lbi_ref[:, l0:l0 + S5_LANE_GROUP], (n_b, S5_LANE_GROUP))

            def body(t, carry, l0=l0, lr=lr, li=li):
                xr, xi = carry
                row = pl.multiple_of(t * n_b, n_b)
                nxr = lr * xr - li * xi + x_ref[pl.ds(row, n_b), l0:l0 + S5_LANE_GROUP]
                nxi = lr * xi + li * xr + x_ref[pl.ds(row, n_b), n_p + l0:n_p + l0 + S5_LANE_GROUP]
                x_ref[pl.ds(row, n_b), l0:l0 + S5_LANE_GROUP] = nxr
                x_ref[pl.ds(row, n_b), n_p + l0:n_p + l0 + S5_LANE_GROUP] = nxi
                return nxr, nxi

            xr, xi = lax.fori_loop(
                0, n_t, body,
                (carry_ref[:, l0:l0 + S5_LANE_GROUP], carry_ref[:, n_p + l0:n_p + l0 + S5_LANE_GROUP]),
                unroll=4)
            carry_ref[:, l0:l0 + S5_LANE_GROUP] = xr
            carry_ref[:, n_p + l0:n_p + l0 + S5_LANE_GROUP] = xi

    ys = []
    for half in range(2):
        xr_b = x_ref[:, half * half_st:(half + 1) * half_st].astype(BF16)
        xi_b = x_ref[:, n_p + half * half_st:n_p + (half + 1) * half_st].astype(BF16)
        ys.append(jnp.dot(xr_b, cc_ref[half * 2], preferred_element_type=F32)
                  + jnp.dot(xi_b, cc_ref[half * 2 + 1], preferred_element_type=F32))
    yg = _gelu_tanh(jnp.concatenate(ys, axis=1) + d_ref[...] * u)
    if n_t > 1:
        for c in range(n_lc):
            utb_ref[c] = yg[:, c * LANES:(c + 1) * LANES]
        for b in range(n_b):
            for c in range(n_lc):
                y_ref[:, b * WIDTH + c * LANES:b * WIDTH + (c + 1) * LANES] = utb_ref[c, pl.ds(b, n_t, stride=n_b), :]
    else:
        y_ref[...] = yg

    @pl.when(i == pl.num_programs(0) - 1)
    def _finish():
        xr_ref[...] = carry_ref[:, :n_p]
        xi_ref[...] = carry_ref[:, n_p:]


def _s5(u, x0_re, x0_im, lb_re, lb_im, bb4, cc4, d_skip, n_b, seq):
    n_p = lb_re.shape[1]
    n_t = min(S5_STEPS, seq)
    n_steps = seq // n_t
    rows = n_t * n_b
    u_block = (n_t, n_b * WIDTH) if seq > 1 else (n_b, WIDTH)
    const2 = lambda i: (0, 0)
    const3 = lambda i: (0, 0, 0)
    return pl.pallas_call(
        functools.partial(_s5_kernel, n_b, n_t),
        grid=(n_steps,),
        in_specs=[pl.BlockSpec(u_block, lambda i: (i, 0)),
                  pl.BlockSpec((n_b, n_p), const2), pl.BlockSpec((n_b, n_p), const2),
                  pl.BlockSpec((1, n_p), const2), pl.BlockSpec((1, n_p), const2),
                  pl.BlockSpec(bb4.shape, const3), pl.BlockSpec(cc4.shape, const3),
                  pl.BlockSpec((1, WIDTH), const2)],
        out_specs=[pl.BlockSpec(u_block, lambda i: (i, 0)),
                   pl.BlockSpec((n_b, n_p), const2), pl.BlockSpec((n_b, n_p), const2)],
        out_shape=[jax.ShapeDtypeStruct(u.shape, F32),
                   jax.ShapeDtypeStruct((n_b, n_p), F32), jax.ShapeDtypeStruct((n_b, n_p), F32)],
        scratch_shapes=[pltpu.VMEM((WIDTH // LANES, rows, LANES), F32), pltpu.VMEM((rows, 2 * n_p), F32),
                        pltpu.VMEM((n_b, 2 * n_p), F32)],
        compiler_params=_params(1),
        name="s5_scan",
    )(u, x0_re, x0_im, lb_re, lb_im, bb4, cc4, d_skip.reshape(1, WIDTH))


def _out_kernel(glu, a1_ref, g1_ref, a2_ref, g2_ref, x_ref, gate_ref, w_ref, wg_ref, o_ref):
    a1 = a1_ref[...].astype(F32)
    if glu:
        a1 = a1 * _sigmoid(jnp.dot(a1.astype(BF16), wg_ref[...], preferred_element_type=F32))
    m1 = a1 * g1_ref[...].astype(F32)
    m2 = a2_ref[...].astype(F32) * g2_ref[...].astype(F32)
    y = (jnp.dot(m1.astype(BF16), w_ref[0:WIDTH, :], preferred_element_type=F32)
         + jnp.dot(m2.astype(BF16), w_ref[WIDTH:2 * WIDTH, :], preferred_element_type=F32))
    o_ref[...] = x_ref[...] + gate_ref[...] * y


def _out_proj(a1, g1, a2, g2, x3, gate, w, wg, glu, a1_by_time):
    n_g, rows, d = x3.shape
    tm = min(ROW_TILE, rows)
    nt = rows // tm
    r_mod = gate.shape[1]
    tok = pl.BlockSpec((tm, WIDTH), lambda gi, i: (gi * nt + i, 0))
    a1_spec = pl.BlockSpec((tm, WIDTH), lambda gi, i: (i, gi)) if a1_by_time else tok
    const = lambda gi, i: (0, 0)
    return pl.pallas_call(
        functools.partial(_out_kernel, glu),
        grid=(n_g, nt),
        in_specs=[a1_spec, tok, tok, tok,
                  pl.BlockSpec((None, tm, d), lambda gi, i: (gi, i, 0)),
                  pl.BlockSpec((None, r_mod, d), lambda gi, i: (gi, 0, 0)),
                  pl.BlockSpec(w.shape, const), pl.BlockSpec(wg.shape, const)],
        out_specs=pl.BlockSpec((None, tm, d), lambda gi, i: (gi, i, 0)),
        out_shape=jax.ShapeDtypeStruct(x3.shape, F32),
        compiler_params=_params(2),
        name="out_proj",
    )(a1, g1, a2, g2, x3, gate, w, wg)


def _pad_cols(w, n):
    return jnp.pad(w, ((0, 0), (0, n - w.shape[1])))


def _even_layer(li, x3, mods, norm_g, p, attn_fn, delta_fn, kv_heads):
    shift, scale, gate = mods
    n_g, rows, d = x3.shape
    lam_init = 0.8 - 0.6 * math.exp(-0.3 * li)
    n_cols = p["w_in"].shape[1]
    n_pad = -(-n_cols // LANES) * LANES
    w_in = _pad_cols(p["w_in"], n_pad).astype(BF16)
    kv = "heads" if kv_heads else "rows"
    segs = ((0, WIDTH, "qnorm", BF16, "rows"), (WIDTH, WIDTH, "knorm", F32, kv),
            (2 * WIDTH, WIDTH, "raw", F32, kv), (3 * WIDTH, WIDTH, "silu", BF16, "rows"),
            (4 * WIDTH, 3 * WIDTH, "raw", F32, "rows"), (7 * WIDTH, WIDTH, "silu", BF16, "rows"),
            (8 * WIDTH, LANES, "raw", F32, "rows"))
    bd = _block_diag(jnp.full((WIDTH // DQK, DQK, DQK), 1.0 / DQK, F32)).astype(BF16)
    qg = jnp.tile(p["qn_g"], WIDTH // DQK).reshape(1, WIDTH)
    kg = jnp.tile(p["kn_g"], WIDTH // DQK).reshape(1, WIDTH)
    outs = _in_proj(x3, shift, scale, norm_g, w_in, segs, bd, qg, kg)
    if kv_heads:
        q, k, k_out, v, v_out, za, qkv_b, zb, ab = outs
    else:
        q, k, v, za, qkv_b, zb, ab = outs
        k_out, v_out = k, v
    lamp = jnp.zeros((SUBLANES, LANES), F32)
    for i, name in enumerate(("lam_q1", "lam_k1", "lam_q2", "lam_k2")):
        lamp = lamp.at[i, :DQK].set(p[name])
    sg = p["subln_g"].reshape(1, HEAD)
    oa = attn_fn(q, k, v, lamp=lamp, sg=sg, lam_init=lam_init)
    ob, conv_new, s_new = delta_fn(qkv_b, ab, p["conv_w"], p["a_log"], p["dt_bias"], p["gn_b"])
    w_out = p["w_out"].astype(BF16)
    x_new = _out_proj(oa, za, ob, zb, x3, gate, w_out, jnp.zeros((SUBLANES, LANES), BF16), False, False)
    return x_new, (k_out, v_out, conv_new, s_new)


def _odd_layer(x3, mods, norm_g, p, s5_mats, x0_re, x0_im, r0, by_time, seq_off=0):
    shift, scale, gate = mods
    n_g, rows, d = x3.shape
    w_in = p["w_in"].astype(BF16)
    segs = ((0, WIDTH, "raw", F32, "time" if by_time else "rows"), (WIDTH, WIDTH, "silu", BF16, "rows"),
            (2 * WIDTH, WIDTH, "raw", F32, "rows"), (3 * WIDTH, WIDTH, "raw", F32, "rows"),
            (4 * WIDTH, WIDTH, "raw", F32, "rows"), (5 * WIDTH, WIDTH, "silu", BF16, "rows"))
    dummy = jnp.zeros((SUBLANES, LANES), BF16)
    dummy_g = jnp.zeros((1, LANES), F32)
    u, zc, qd, kd, vd, zd = _in_proj(x3, shift, scale, norm_g, w_in, segs, dummy, dummy_g, dummy_g)
    lb_re, lb_im, bb4, cc4 = s5_mats
    n_b = n_g if by_time else rows
    seq = rows if by_time else 1
    yg, xr, xi = _s5(u, x0_re, x0_im, lb_re, lb_im, bb4, cc4, p["s5_d"], n_b, seq)
    if by_time:
        od, r_new = _retention(qd, kd, vd, p["gn_d"], r0, n_g, rows)
    else:
        od, r_new = _retention_step(qd, kd, vd, p["gn_d"], r0, seq_off)
    x_new = _out_proj(yg, zc, od, zd, x3, gate, p["w_out"].astype(BF16), p["w_glu"].astype(BF16), True, by_time)
    return x_new, (xr, xi, r_new)


def kernel(x_prompt, x_sample, c_prompt, c_sample, page_table, cache_k, cache_v, state_b_conv, state_b_ssm,
           state_c_re, state_c_im, state_d_ret, norm_g, w_ada, b_ada, w_in_e, w_out_e, qn_g, kn_g,
           lam_q1, lam_k1, lam_q2, lam_k2, subln_g, conv_w, a_log, dt_bias, gn_b, w_in_o, w_out_o,
           s5_a_re, s5_a_im, s5_b_re, s5_b_im, s5_c_re, s5_c_im, s5_d, s5_log_dt, w_glu, gn_d):
    n_bp, seq, d = x_prompt.shape
    n_bs = x_sample.shape[0]
    depth = norm_g.shape[0]
    n_pages, page = page_table.shape[1], cache_k.shape[2]
    n_g, n_p = s5_a_re.shape[1], s5_a_re.shape[2]
    conv_all = state_b_conv.reshape(-1, K_CONV - 1, 3 * WIDTH)
    ssm_all = state_b_ssm.reshape(-1, N_HEADS, HEAD, HEAD)
    ret_all = state_d_ret.reshape(-1, N_HEADS, HEAD, HEAD)

    mod = _modulation(jnp.concatenate([c_prompt, c_sample], axis=0), w_ada, b_ada)
    xp = x_prompt
    xs = x_sample.reshape(1, n_bs, d)
    outs = {name: [] for name in ("k_p", "v_p", "k_s", "v_s", "cv_p", "cv_s", "dl_p", "dl_s",
                                  "s5r_p", "s5i_p", "s5r_s", "s5i_s", "rt_p", "rt_s")}
    for li in range(depth):
        mods_p = tuple(mod[li, :n_bp, j * d:(j + 1) * d].reshape(n_bp, 1, d) for j in range(3))
        mods_s = tuple(mod[li, n_bp:, j * d:(j + 1) * d].reshape(1, n_bs, d) for j in range(3))
        if li % 2 == 0:
            e = li // 2
            p = dict(w_in=w_in_e[e], w_out=w_out_e[e], qn_g=qn_g[e], kn_g=kn_g[e], lam_q1=lam_q1[e],
                     lam_k1=lam_k1[e], lam_q2=lam_q2[e], lam_k2=lam_k2[e], subln_g=subln_g[e],
                     conv_w=conv_w[e], a_log=a_log[e], dt_bias=dt_bias[e], gn_b=gn_b[e])
            attn_p = functools.partial(_attn_prompt, n_b=n_bp, seq=seq)
            delta_p = functools.partial(
                _delta, cbuf=jnp.zeros((n_bp, K_CONV - 1, 3 * WIDTH), F32),
                s0=jnp.zeros((n_bp, N_HEADS, HEAD, HEAD), F32), n_b=n_bp, seq=seq)
            xp, (k_p, v_p, c_p, s_p) = _even_layer(li, xp, mods_p, norm_g[li], p, attn_p, delta_p, True)
            attn_s = functools.partial(_attn_decode, cache_k=cache_k, cache_v=cache_v, page_table=page_table,
                                       layer=e)
            delta_s = functools.partial(_delta_step, cbuf=conv_all, s0=ssm_all, seq_off=e * n_bs)
            xs, (k_s, v_s, c_s, s_s) = _even_layer(li, xs, mods_s, norm_g[li], p, attn_s, delta_s, False)
            outs["k_p"].append(k_p.reshape(n_bp, seq, N_HEADS, HEAD))
            outs["v_p"].append(v_p.reshape(n_bp, seq, N_HEADS, HEAD))
            outs["k_s"].append(k_s.reshape(n_bs, 1, N_HEADS, HEAD))
            outs["v_s"].append(v_s.reshape(n_bs, 1, N_HEADS, HEAD))
            outs["cv_p"].append(c_p)
            outs["cv_s"].append(c_s)
            outs["dl_p"].append(s_p)
            outs["dl_s"].append(s_s)
        else:
            o = li // 2
            p = dict(w_in=w_in_o[o], w_out=w_out_o[o], s5_d=s5_d[o], w_glu=w_glu[o], gn_d=gn_d[o])
            lb_re, lb_im, bbr, bbi = _s5_prep(s5_a_re[o], s5_a_im[o], s5_log_dt[o], s5_b_re[o], s5_b_im[o])
            hg = S5_HALF_GROUPS
            bb4 = jnp.stack([_block_diag(t[h * hg:(h + 1) * hg]) for h in range(2) for t in (bbr, bbi)]).astype(BF16)
            cre = jnp.swapaxes(s5_c_re[o], 1, 2)
            cim = -jnp.swapaxes(s5_c_im[o], 1, 2)
            cc4 = jnp.stack([_block_diag(t[h * hg:(h + 1) * hg]) for h in range(2) for t in (cre, cim)]).astype(BF16)
            s5_mats = (lb_re.reshape(1, n_g * n_p), lb_im.reshape(1, n_g * n_p), bb4, cc4)
            zeros_c = jnp.zeros((n_bp, n_g * n_p), F32)
            xp, (r_p, i_p, t_p) = _odd_layer(xp, mods_p, norm_g[li], p, s5_mats, zeros_c, zeros_c,
                                             jnp.zeros((n_bp, N_HEADS, HEAD, HEAD), F32), True)
            xs, (r_s, i_s, t_s) = _odd_layer(xs, mods_s, norm_g[li], p, s5_mats,
                                             state_c_re[o].reshape(n_bs, n_g * n_p),
                                             state_c_im[o].reshape(n_bs, n_g * n_p), ret_all, False, o * n_bs)
            outs["s5r_p"].append(r_p.reshape(n_bp, n_g, n_p))
            outs["s5i_p"].append(i_p.reshape(n_bp, n_g, n_p))
            outs["s5r_s"].append(r_s.reshape(n_bs, n_g, n_p))
            outs["s5i_s"].append(i_s.reshape(n_bs, n_g, n_p))
            outs["rt_p"].append(t_p)
            outs["rt_s"].append(t_s)
    st = lambda name: jnp.stack(outs[name])
    return (xp, xs.reshape(n_bs, 1, d), st("k_p"), st("v_p"), st("k_s"), st("v_s"), st("cv_p"), st("cv_s"),
            st("dl_p"), st("dl_s"), st("s5r_p"), st("s5i_p"), st("s5r_s"), st("s5i_s"), st("rt_p"), st("rt_s"))
```

```python
import functools
import math

import jax
import jax.numpy as jnp
from jax import lax
from jax.experimental import pallas as pl
from jax.experimental.pallas import tpu as pltpu

F32 = jnp.float32
BF16 = jnp.bfloat16
EPS = 1e-6

LANES = 128
SUBLANES = 8
VMEM_LIMIT_BYTES = 48 * 1024 * 1024

HEAD = 128
DQK = HEAD // 2
LOG2E = math.log2(math.e)
N_HEADS = 4
WIDTH = N_HEADS * HEAD
K_CONV = 4
DELTA_CHUNK = 64
GROUP_C = 16
P_C = 64
S5_HALF_GROUPS = 16

ROW_TILE = 256
ATTN_TILE = 512
ATTN_ROW_CHUNK = 256
DELTA_ROWS = 256
DELTA_SEQS = 1
RET_ROWS = 256
S5_STEPS = 64
S5_LANE_GROUP = 512
STEP_SEQS = 8


def _params(n_axes, vmem=VMEM_LIMIT_BYTES):
    return pltpu.CompilerParams(dimension_semantics=("arbitrary",) * n_axes, vmem_limit_bytes=vmem)


def _sigmoid(x):
    return 1.0 / (1.0 + jnp.exp(-x))


def _silu(x):
    return x * _sigmoid(x)


def _softplus(x):
    return jnp.maximum(x, 0.0) + jnp.log1p(jnp.exp(-jnp.abs(x)))


def _gelu_tanh(x):
    return 0.5 * x * (1.0 + jnp.tanh(math.sqrt(2.0 / math.pi) * (x + 0.044715 * (x * x * x))))


def _mm(a, b):
    if a.ndim == 3:
        return lax.dot_general(a.astype(BF16), b.astype(BF16), (((2,), (1,)), ((0,), (0,))),
                               preferred_element_type=F32)
    return jnp.dot(a.astype(BF16), b.astype(BF16), preferred_element_type=F32)


def _mm_nt(a, b):
    if a.ndim == 3:
        return lax.dot_general(a.astype(BF16), b.astype(BF16), (((2,), (2,)), ((0,), (0,))),
                               preferred_element_type=F32)
    return lax.dot_general(a.astype(BF16), b.astype(BF16), (((1,), (1,)), ((), ())),
                           preferred_element_type=F32)


def _mod_kernel(c_ref, w_ref, b_ref, o_ref):
    o_ref[...] = _mm(_silu(c_ref[...]), w_ref[...]) + b_ref[...]


def _modulation(c_all, w_ada, b_ada):
    depth, d, n3 = w_ada.shape
    rows = c_all.shape[0]
    tn = d
    return pl.pallas_call(
        _mod_kernel,
        grid=(depth, n3 // tn),
        in_specs=[pl.BlockSpec((rows, d), lambda l, n: (0, 0)),
                  pl.BlockSpec((None, d, tn), lambda l, n: (l, 0, n)),
                  pl.BlockSpec((None, 1, tn), lambda l, n: (l, 0, n))],
        out_specs=pl.BlockSpec((None, rows, tn), lambda l, n: (l, 0, n)),
        out_shape=jax.ShapeDtypeStruct((depth, rows, n3), F32),
        compiler_params=_params(2),
        name="modulation",
    )(c_all, w_ada, b_ada.reshape(depth, 1, n3))


def _in_kernel(segs, x_ref, shift_ref, scale_ref, g_ref, w_ref, bd_ref, qg_ref, kg_ref, *refs):
    out_refs, h_ref = iter(refs[:-1]), refs[-1]
    x = x_ref[...]
    h = x * lax.rsqrt(jnp.mean(x * x, axis=-1, keepdims=True) + EPS) * g_ref[...]
    h_ref[...] = (h * (1.0 + scale_ref[...]) + shift_ref[...]).astype(BF16)
    for c0, width, kind, layout in segs:
        acc = jnp.dot(h_ref[...], w_ref[:, c0:c0 + width], preferred_element_type=F32)
        if kind == "silu":
            acc = _silu(acc)
        elif kind in ("qnorm", "knorm"):
            ms = jnp.dot((acc * acc).astype(BF16), bd_ref[...], preferred_element_type=F32)
            acc = acc * lax.rsqrt(ms + EPS) * (qg_ref if kind == "qnorm" else kg_ref)[...]
            if kind == "qnorm":
                acc = acc * (DQK ** -0.5 * LOG2E)
        o_ref = next(out_refs)
        o_ref[...] = acc.astype(o_ref.dtype)
        if layout == "heads":
            hm_ref = next(out_refs)
            n_h = width // HEAD
            for h in range(n_h):
                hm_ref[pl.ds(h, acc.shape[0], stride=n_h), :] = acc[:, h * HEAD:(h + 1) * HEAD]


def _in_proj(x3, shift, scale, g, w, segs, bd, qg, kg):
    n_g, rows, d = x3.shape
    tm = min(ROW_TILE, rows)
    nt = rows // tm
    r_mod = shift.shape[1]
    assert r_mod in (1, tm)
    out_shapes, out_specs = [], []
    row_map = lambda gi, i: (gi * nt + i, 0)
    for _, width, _, dtype, layout in segs:
        if layout == "time":
            out_shapes.append(jax.ShapeDtypeStruct((rows, n_g * width), dtype))
            out_specs.append(pl.BlockSpec((tm, width), lambda gi, i: (i, gi)))
        elif layout == "heads":
            n_h = width // HEAD
            out_shapes += [jax.ShapeDtypeStruct((n_g * rows, width), BF16),
                           jax.ShapeDtypeStruct((n_g * rows * n_h, HEAD), dtype)]
            out_specs += [pl.BlockSpec((tm, width), row_map), pl.BlockSpec((tm * n_h, HEAD), row_map)]
        else:
            out_shapes.append(jax.ShapeDtypeStruct((n_g * rows, width), dtype))
            out_specs.append(pl.BlockSpec((tm, width), row_map))
    const = lambda gi, i: (0, 0)
    return pl.pallas_call(
        functools.partial(_in_kernel, tuple((s[0], s[1], s[2], s[4]) for s in segs)),
        grid=(n_g, nt),
        in_specs=[pl.BlockSpec((None, tm, d), lambda gi, i: (gi, i, 0)),
                  pl.BlockSpec((None, r_mod, d), lambda gi, i: (gi, 0, 0)),
                  pl.BlockSpec((None, r_mod, d), lambda gi, i: (gi, 0, 0)),
                  pl.BlockSpec((1, d), const),
                  pl.BlockSpec(w.shape, const),
                  pl.BlockSpec(bd.shape, const),
                  pl.BlockSpec(qg.shape, const),
                  pl.BlockSpec(kg.shape, const)],
        out_specs=out_specs,
        out_shape=out_shapes,
        scratch_shapes=[pltpu.VMEM((tm, d), BF16)],
        compiler_params=_params(2),
        name="in_proj",
    )(x3, shift, scale, g.reshape(1, d), w, bd, qg, kg)


def _lambda_value(lamp, lam_init):
    a = jnp.sum(lamp[0:1] * lamp[1:2], axis=-1, keepdims=True)
    b = jnp.sum(lamp[2:3] * lamp[3:4], axis=-1, keepdims=True)
    return jnp.exp(a) - jnp.exp(b) + lam_init


def _sub_ln(o, gain, lam_init):
    return o * lax.rsqrt(jnp.mean(o * o, axis=-1, keepdims=True) + EPS) * gain * (1.0 - lam_init)


def _attn_kernel(lam_init, row_chunk, qt_ref, kt_ref, q_ref, k_ref, v_ref, kbias_ref, lamp_ref, sg_ref, o_ref,
                 qs_ref, m_ref, l_ref, acc_ref):
    qi, ki = qt_ref[pl.program_id(2)], kt_ref[pl.program_id(2)]
    tq, tk = q_ref.shape[0], k_ref.shape[0]

    @pl.when(ki == 0)
    def _init():
        m_ref[...] = jnp.full(m_ref.shape, -jnp.inf, F32)
        l_ref[...] = jnp.zeros(l_ref.shape, F32)
        acc_ref[...] = jnp.zeros(acc_ref.shape, F32)
        q = q_ref[...]
        lane = lax.broadcasted_iota(jnp.int32, (1, HEAD), 1)
        qs_ref[0:tq, :] = jnp.where(lane < DQK, q, jnp.zeros_like(q))
        qs_ref[tq:2 * tq, :] = jnp.where(lane >= DQK, q, jnp.zeros_like(q))

    def step(masked):
        kb = k_ref[...].astype(BF16)
        vb = v_ref[...].astype(BF16)
        kbias = kbias_ref[...]
        for r0 in range(0, 2 * tq, row_chunk):
            rs = slice(r0, r0 + row_chunk)
            n_k = min(tk, r0 % tq + row_chunk) if masked else tk
            s = _mm_nt(qs_ref[rs, :], kb[:n_k]) + kbias[:, :n_k]
            if masked:
                q_row = lax.broadcasted_iota(jnp.int32, (row_chunk, n_k), 0) + (r0 % tq)
                col = lax.broadcasted_iota(jnp.int32, (row_chunk, n_k), 1)
                s = jnp.where(col <= q_row, s, -jnp.inf)
            m_prev = m_ref[rs, :]
            m_new = jnp.maximum(m_prev, jnp.max(s, axis=-1, keepdims=True))
            alpha = jnp.exp2(m_prev - m_new)
            p = jnp.exp2(s - m_new)
            l_ref[rs, :] = alpha * l_ref[rs, :] + jnp.sum(p, axis=-1, keepdims=True)
            acc_ref[rs, :] = alpha * acc_ref[rs, :] + jnp.dot(p.astype(BF16), vb[:n_k], preferred_element_type=F32)
            m_ref[rs, :] = m_new

    @pl.when(ki < qi)
    def _off_diagonal():
        step(False)

    @pl.when(ki == qi)
    def _diagonal():
        step(True)
        lam = _lambda_value(lamp_ref[...], lam_init)
        o = acc_ref[0:tq, :] / l_ref[0:tq, :] - lam * (acc_ref[tq:2 * tq, :] / l_ref[tq:2 * tq, :])
        o_ref[...] = _sub_ln(o, sg_ref[...], lam_init).astype(o_ref.dtype)


def _alibi_slopes(n_heads):
    return jnp.exp2(-8.0 * jnp.arange(1, n_heads + 1, dtype=F32) / n_heads)


def _attn_prompt(q, k, v, lamp, sg, lam_init, n_b, seq):
    t = min(ATTN_TILE, seq)
    nq = seq // t
    row_chunk = min(ATTN_ROW_CHUNK, t)
    kbias = (LOG2E * _alibi_slopes(N_HEADS)[:, None] * jnp.arange(seq, dtype=F32)[None, :]).reshape(N_HEADS, nq, 1, t)
    pairs = [(qi, ki) for qi in range(nq) for ki in range(qi + 1)]
    qt = jnp.asarray([p[0] for p in pairs], jnp.int32)
    kt = jnp.asarray([p[1] for p in pairs], jnp.int32)
    q_map = lambda b, h, t_, qt_, kt_: (b * nq + qt_[t_], h)
    kv_map = lambda b, h, t_, qt_, kt_: (b * nq + kt_[t_], h)
    const = lambda b, h, t_, qt_, kt_: (0, 0)
    grid_spec = pltpu.PrefetchScalarGridSpec(
        num_scalar_prefetch=2,
        grid=(n_b, N_HEADS, len(pairs)),
        in_specs=[pl.BlockSpec((t, HEAD), q_map),
                  pl.BlockSpec((t, HEAD), kv_map),
                  pl.BlockSpec((t, HEAD), kv_map),
                  pl.BlockSpec((None, None, 1, t), lambda b, h, t_, qt_, kt_: (h, kt_[t_], 0, 0)),
                  pl.BlockSpec(lamp.shape, const),
                  pl.BlockSpec((1, HEAD), const)],
        out_specs=pl.BlockSpec((t, HEAD), q_map),
        scratch_shapes=[pltpu.VMEM((2 * t, HEAD), BF16), pltpu.VMEM((2 * t, 1), F32),
                        pltpu.VMEM((2 * t, 1), F32), pltpu.VMEM((2 * t, HEAD), F32)],
    )
    return pl.pallas_call(
        functools.partial(_attn_kernel, lam_init, row_chunk),
        grid_spec=grid_spec,
        out_shape=jax.ShapeDtypeStruct((n_b * seq, WIDTH), BF16),
        compiler_params=_params(3),
        name="attn_prompt",
    )(qt, kt, q, k, v, kbias, lamp, sg)


def _decode_kernel(lam_init, n_pages, pt_ref, q_ref, kn_ref, vn_ref, nsl_ref, lamp_ref, sg_ref, *refs):
    del pt_ref
    k_refs, v_refs, o_ref = refs[:n_pages], refs[n_pages:2 * n_pages], refs[2 * n_pages]
    cols_page = k_refs[0].shape[0]
    n_rows = 4 * N_HEADS
    row = lax.broadcasted_iota(jnp.int32, (n_rows, HEAD), 0)
    lane = lax.broadcasted_iota(jnp.int32, (n_rows, HEAD), 1)

    def per_row_head(x):
        out = jnp.zeros((n_rows, HEAD), F32)
        for h in range(N_HEADS):
            out = jnp.where((row % N_HEADS == h) & (row < 2 * N_HEADS), x[:, h * HEAD:(h + 1) * HEAD], out)
        return out

    qm = jnp.where(lane // DQK == row // N_HEADS, per_row_head(q_ref[...].astype(F32)), 0.0).astype(BF16)
    s = jnp.concatenate([_mm_nt(qm, k_refs[j][...]) for j in range(n_pages)], axis=1)
    n_cols = n_pages * cols_page
    past = n_cols // N_HEADS
    col = lax.broadcasted_iota(jnp.int32, (n_rows, n_cols), 1)
    row_c = lax.broadcasted_iota(jnp.int32, (n_rows, n_cols), 0)
    own_head = (col % N_HEADS == row_c % N_HEADS) & (row_c < 2 * N_HEADS)
    s = jnp.where(own_head, s + nsl_ref[...] * (past - col // N_HEADS).astype(F32), -jnp.inf)
    s_self = jnp.sum(qm.astype(F32) * per_row_head(kn_ref[...]).astype(BF16).astype(F32), axis=-1, keepdims=True)
    m = jnp.maximum(jnp.max(s, axis=-1, keepdims=True), s_self)
    p = jnp.exp2(s - m)
    p_self = jnp.exp2(s_self - m)
    denom = jnp.sum(p, axis=-1, keepdims=True) + p_self
    lam = _lambda_value(lamp_ref[...], lam_init)
    r1 = row[:, 0:1]
    coef = jnp.where(r1 < N_HEADS, 1.0, jnp.where(r1 < 2 * N_HEADS, -lam, 0.0)) / denom
    pw = p * coef
    acc = (p_self * coef) * per_row_head(vn_ref[...]).astype(BF16).astype(F32)
    for j in range(n_pages):
        acc = acc + _mm(pw[:, j * cols_page:(j + 1) * cols_page], v_refs[j][...])
    for h in range(N_HEADS):
        o = acc[h:h + 1, :] + acc[N_HEADS + h:N_HEADS + h + 1, :]
        o_ref[:, h * HEAD:(h + 1) * HEAD] = _sub_ln(o, sg_ref[...], lam_init)


def _attn_decode(q, k_new, v_new, cache_k, cache_v, page_table, lamp, sg, lam_init, layer):
    n_s, n_pages = page_table.shape
    n_l, n_phys, page = cache_k.shape[:3]
    cache_k = cache_k.reshape(n_l, n_phys, page * N_HEADS, HEAD)
    cache_v = cache_v.reshape(n_l, n_phys, page * N_HEADS, HEAD)
    nsl = -LOG2E * _alibi_slopes(N_HEADS)
    nsl8 = jnp.concatenate([nsl, nsl, jnp.zeros((2 * N_HEADS,), F32)]).reshape(4 * N_HEADS, 1)
    row_spec = pl.BlockSpec((None, 1, WIDTH), lambda b, pt: (b, 0, 0))
    const = lambda b, pt: (0, 0)

    def page_spec(j):
        return pl.BlockSpec((None, None, page * N_HEADS, HEAD), lambda b, pt: (layer, pt[b, j], 0, 0))

    grid_spec = pltpu.PrefetchScalarGridSpec(
        num_scalar_prefetch=1,
        grid=(n_s,),
        in_specs=[row_spec, row_spec, row_spec,
                  pl.BlockSpec(nsl8.shape, const), pl.BlockSpec(lamp.shape, const),
                  pl.BlockSpec((1, HEAD), const)]
                 + [page_spec(j) for j in range(n_pages)] * 2,
        out_specs=row_spec,
    )
    out = pl.pallas_call(
        functools.partial(_decode_kernel, lam_init, n_pages),
        grid_spec=grid_spec,
        out_shape=jax.ShapeDtypeStruct((n_s, 1, WIDTH), F32),
        compiler_params=_params(1),
        name="attn_decode",
    )(page_table, q.reshape(n_s, 1, WIDTH), k_new.reshape(n_s, 1, WIDTH), v_new.reshape(n_s, 1, WIDTH),
      nsl8, lamp, sg, *([cache_k] * n_pages), *([cache_v] * n_pages))
    return out.reshape(n_s, WIDTH)


INV_BASE = SUBLANES


def _unit_lower_inverse(a, block, row, col):
    in_base = row // INV_BASE == col // INV_BASE
    power = jnp.where(in_base, a, 0.0)
    inv = jnp.where(row == col, 1.0, 0.0) - power
    order = 2
    while order < INV_BASE:
        power = _mm(power, power)
        inv = inv + _mm(inv, power)
        order *= 2
    size = INV_BASE
    while size < block:
        coupling = jnp.where((row // (2 * size) == col // (2 * size)) & (row // size != col // size), a, 0.0)
        inv = inv - _mm(_mm(inv, coupling), inv)
        size *= 2
    return inv


def _mm_tn(a, b):
    return lax.dot_general(a.astype(BF16), b.astype(BF16), (((0,), (0,)), ((), ())),
                           preferred_element_type=F32)


def _delta_kernel(chunk, qkv_ref, ab_ref, cw_ref, hp_ref, gn_ref, cbuf_ref, s0_ref, tri_ref,
                  o_ref, cnew_ref, sout_ref, xs_ref, st_ref):
    r = pl.program_id(1)
    n_seq = qkv_ref.shape[0]
    tail = K_CONV - 1

    @pl.when(r == 0)
    def _init():
        st_ref[...] = s0_ref[...]
        for i in range(n_seq):
            xs_ref[i, 0:SUBLANES, :] = jnp.zeros((SUBLANES, xs_ref.shape[2]), F32)
            xs_ref[i, SUBLANES - tail:SUBLANES, :] = cbuf_ref[i]

    for i in range(n_seq):
        _delta_sequence(chunk, qkv_ref.at[i], ab_ref.at[i], cw_ref, hp_ref, gn_ref, tri_ref,
                        o_ref.at[i], cnew_ref.at[i], xs_ref.at[i], st_ref.at[i])

    @pl.when(r == pl.num_programs(1) - 1)
    def _finish():
        sout_ref[...] = st_ref[...]


def _delta_sequence(chunk, qkv_ref, ab_ref, cw_ref, hp_ref, gn_ref, tri_ref, o_ref, cnew_ref, xs_ref, st_ref):
    rows = qkv_ref.shape[0]
    n_chunks = rows // chunk
    tail = K_CONV - 1
    xs_ref[SUBLANES:SUBLANES + rows, :] = qkv_ref[...]
    w = cw_ref[...]
    y = w[0:1] * xs_ref[SUBLANES - 3:SUBLANES - 3 + rows, :]
    for j in range(1, K_CONV):
        y = y + w[j:j + 1] * xs_ref[SUBLANES - 3 + j:SUBLANES - 3 + j + rows, :]
    new_tail = xs_ref[SUBLANES + rows - tail:SUBLANES + rows, :]
    cnew_ref[...] = new_tail
    xs_ref[SUBLANES - tail:SUBLANES, :] = new_tail
    act = _silu(y)

    ab = ab_ref[...]
    hp = hp_ref[...]
    g_all = -jnp.exp(hp[0:1]) * _softplus(ab + hp[1:2])
    beta_all = _sigmoid(ab)
    gam_all = jnp.dot(tri_ref[...], g_all, preferred_element_type=F32, precision=lax.Precision.HIGHEST)

    n_st = N_HEADS * chunk
    ri = lax.broadcasted_iota(jnp.int32, (n_st, n_st), 0)
    ci = lax.broadcasted_iota(jnp.int32, (n_st, n_st), 1)
    same_head = ri // chunk == ci // chunk
    incl, strict, diag = same_head & (ri >= ci), same_head & (ri > ci), ri == ci
    gn = gn_ref[...]
    local = [slice(h * chunk, (h + 1) * chunk) for h in range(N_HEADS)]
    states = [st_ref[h] for h in range(N_HEADS)]

    def stacked(col0):
        return jnp.stack([jnp.concatenate([act[c * chunk:(c + 1) * chunk, col0 + h * HEAD:col0 + (h + 1) * HEAD]
                                           for h in range(N_HEADS)], axis=0) for c in range(n_chunks)], axis=0)

    def stacked_column(t, lane0):
        return jnp.stack([jnp.concatenate([t[c * chunk:(c + 1) * chunk, lane0 + h:lane0 + h + 1]
                                           for h in range(N_HEADS)], axis=0) for c in range(n_chunks)], axis=0)

    q, k, v = stacked(0), stacked(WIDTH), stacked(2 * WIDTH)
    q = q * lax.rsqrt(jnp.sum(q * q, axis=-1, keepdims=True) + EPS) * HEAD ** -0.5
    k = k * lax.rsqrt(jnp.sum(k * k, axis=-1, keepdims=True) + EPS)
    gc = stacked_column(gam_all, 0)
    bc = stacked_column(beta_all, N_HEADS)
    gr = jnp.sum(jnp.where(diag, gc, 0.0), axis=1, keepdims=True)
    decay = jnp.exp(jnp.where(incl, gc - gr, -jnp.inf))
    inv = _unit_lower_inverse(jnp.where(strict, bc * decay * _mm_nt(k, k), 0.0), chunk, ri, ci)
    eg = jnp.exp(gc)
    sol_all = _mm(inv, jnp.concatenate([bc * v, (bc * eg) * k], axis=2))
    qk_all = _mm_nt(q, k) * decay

    for c in range(n_chunks):
        sol, qk, q_c, k_c, gc_c = sol_all[c], qk_all[c], q[c], k[c], gc[c]
        u = jnp.concatenate([sol[ls, :HEAD] - _mm(sol[ls, HEAD:], states[h]) for h, ls in enumerate(local)], axis=0)
        o = eg[c] * jnp.concatenate([_mm(q_c[ls], states[h]) for h, ls in enumerate(local)], axis=0) + _mm(qk, u)
        for h, ls in enumerate(local):
            gl = gc_c[ls.stop - 1:ls.stop, :]
            states[h] = jnp.exp(gl) * states[h] + _mm_tn(k_c[ls] * jnp.exp(gl - gc_c[ls]), u[ls])
        on = o * lax.rsqrt(jnp.mean(o * o, axis=-1, keepdims=True) + EPS) * gn
        for h, ls in enumerate(local):
            o_ref[c * chunk:(c + 1) * chunk, h * HEAD:(h + 1) * HEAD] = on[ls].astype(o_ref.dtype)

    for h in range(N_HEADS):
        st_ref[h] = states[h]


def _chunk_tri(rows, chunk):
    r = jnp.arange(rows)
    return ((r[:, None] >= r[None, :]) & (r[:, None] // chunk == r[None, :] // chunk)).astype(F32)


def _head_params(a_log, dt_bias):
    return jnp.zeros((SUBLANES, LANES), F32).at[0, :N_HEADS].set(a_log).at[1, :N_HEADS].set(dt_bias)


def _delta(qkv, ab, conv_w, a_log, dt_bias, gn, cbuf, s0, n_b, seq):
    chunk = min(DELTA_CHUNK, seq)
    rows = min(DELTA_ROWS, seq)
    assert seq % rows == 0 and rows % chunk == 0
    nr = seq // rows
    nbb = math.gcd(DELTA_SEQS, n_b)
    width3 = 3 * WIDTH
    hp = _head_params(a_log, dt_bias)
    tri = _chunk_tri(rows, chunk)
    const = lambda b, r: (0, 0)
    tok = lambda width: pl.BlockSpec((nbb, rows, width), lambda b, r: (b, r, 0))
    tail = pl.BlockSpec((nbb, K_CONV - 1, width3), lambda b, r: (b, 0, 0))
    st = pl.BlockSpec((nbb, N_HEADS, HEAD, HEAD), lambda b, r: (b, 0, 0, 0))
    o, cnew, s_new = pl.pallas_call(
        functools.partial(_delta_kernel, chunk),
        grid=(n_b // nbb, nr),
        in_specs=[tok(width3), tok(LANES),
                  pl.BlockSpec((K_CONV, width3), const),
                  pl.BlockSpec((SUBLANES, LANES), const),
                  pl.BlockSpec((1, HEAD), const),
                  tail, st,
                  pl.BlockSpec((rows, rows), const)],
        out_specs=[tok(WIDTH), tail, st],
        out_shape=[jax.ShapeDtypeStruct((n_b, seq, WIDTH), BF16),
                   jax.ShapeDtypeStruct((n_b, K_CONV - 1, width3), F32),
                   jax.ShapeDtypeStruct((n_b, N_HEADS, HEAD, HEAD), F32)],
        scratch_shapes=[pltpu.VMEM((nbb, rows + SUBLANES, width3), F32),
                        pltpu.VMEM((nbb, N_HEADS, HEAD, HEAD), F32)],
        compiler_params=_params(2),
        name="delta_rule",
    )(qkv.reshape(n_b, seq, width3), ab.reshape(n_b, seq, LANES), conv_w, hp, gn.reshape(1, HEAD), cbuf, s0, tri)
    return o.reshape(n_b * seq, WIDTH), cnew, s_new


def _eye(n):
    return lax.broadcasted_iota(jnp.int32, (n, n), 0) == lax.broadcasted_iota(jnp.int32, (n, n), 1)


def _delta_step_kernel(x_ref, ab_ref, cw_ref, hp_ref, gn_ref, cbuf_ref, s0_ref, o_ref, cnew_ref, sout_ref):
    w = cw_ref[...]
    hp = hp_ref[...]
    x = x_ref[...]
    buf = cbuf_ref[...]
    y = w[0:1] * buf[:, 0:1] + w[1:2] * buf[:, 1:2] + w[2:3] * buf[:, 2:3] + w[3:4] * x
    cnew_ref[:, 0:2, :] = buf[:, 1:3]
    cnew_ref[:, 2:3, :] = x
    act = _silu(y)
    ab = ab_ref[...]
    decay_all = jnp.exp(-jnp.exp(hp[0:1]) * _softplus(ab + hp[1:2]))
    beta_all = _sigmoid(ab)

    def per_head(t, col0, width):
        return jnp.stack([t[:, :, col0 + h * width:col0 + (h + 1) * width] for h in range(N_HEADS)], axis=1)

    q, k, v = per_head(act, 0, HEAD), per_head(act, WIDTH, HEAD), per_head(act, 2 * WIDTH, HEAD)
    q = q * lax.rsqrt(jnp.sum(q * q, axis=-1, keepdims=True) + EPS) * HEAD ** -0.5
    k = k * lax.rsqrt(jnp.sum(k * k, axis=-1, keepdims=True) + EPS)
    a = per_head(decay_all, 0, 1)
    beta = per_head(beta_all, N_HEADS, 1)
    eye = _eye(HEAD)
    k_col = jnp.sum(jnp.where(eye, k, 0.0), axis=-1, keepdims=True)
    q_col = jnp.sum(jnp.where(eye, q, 0.0), axis=-1, keepdims=True)
    state = s0_ref[...]
    u = beta * (v - a * jnp.sum(state * k_col, axis=2, keepdims=True))
    state = a * state + k_col * u
    sout_ref[...] = state
    o = jnp.sum(state * q_col, axis=2, keepdims=True)
    o = o * lax.rsqrt(jnp.mean(o * o, axis=-1, keepdims=True) + EPS) * gn_ref[...]
    for h in range(N_HEADS):
        o_ref[:, :, h * HEAD:(h + 1) * HEAD] = o[:, h]


def _step_block(n_s, seq_off):
    return math.gcd(STEP_SEQS, math.gcd(n_s, seq_off))


def _delta_step(qkv, ab, conv_w, a_log, dt_bias, gn, cbuf, s0, seq_off):
    n_s = qkv.shape[0]
    nb = _step_block(n_s, seq_off)
    off = seq_off // nb
    width3 = 3 * WIDTH
    const = lambda b: (0, 0)
    row = lambda width: pl.BlockSpec((nb, 1, width), lambda b: (b, 0, 0))
    tail = pl.BlockSpec((nb, K_CONV - 1, width3), lambda b: (b, 0, 0))
    st = pl.BlockSpec((nb, N_HEADS, HEAD, HEAD), lambda b: (b, 0, 0, 0))
    tail_in = pl.BlockSpec((nb, K_CONV - 1, width3), lambda b: (b + off, 0, 0))
    st_in = pl.BlockSpec((nb, N_HEADS, HEAD, HEAD), lambda b: (b + off, 0, 0, 0))
    o, cnew, s_new = pl.pallas_call(
        _delta_step_kernel,
        grid=(n_s // nb,),
        in_specs=[row(width3), row(LANES), pl.BlockSpec((K_CONV, width3), const),
                  pl.BlockSpec((SUBLANES, LANES), const), pl.BlockSpec((1, HEAD), const), tail_in, st_in],
        out_specs=[row(WIDTH), tail, st],
        out_shape=[jax.ShapeDtypeStruct((n_s, 1, WIDTH), F32),
                   jax.ShapeDtypeStruct((n_s, K_CONV - 1, width3), F32),
                   jax.ShapeDtypeStruct((n_s, N_HEADS, HEAD, HEAD), F32)],
        compiler_params=_params(1),
        name="delta_step",
    )(qkv.reshape(n_s, 1, width3), ab.reshape(n_s, 1, LANES), conv_w, _head_params(a_log, dt_bias),
      gn.reshape(1, HEAD), cbuf, s0)
    return o.reshape(n_s, WIDTH), cnew, s_new


def _log_gamma(h):
    return math.log1p(-(2.0 ** (-5.0 - h)))


def _layer_norm(o, gain):
    oc = o - jnp.mean(o, axis=-1, keepdims=True)
    return oc * lax.rsqrt(jnp.mean(oc * oc, axis=-1, keepdims=True) + EPS) * gain


def _ret_kernel(q_ref, k_ref, v_ref, gn_ref, s0_ref, o_ref, sout_ref, st_ref):
    r = pl.program_id(1)
    rows = q_ref.shape[0]

    @pl.when(r == 0)
    def _init():
        st_ref[...] = s0_ref[...]

    ri = lax.broadcasted_iota(jnp.int32, (rows, 1), 0)
    ci = lax.broadcasted_iota(jnp.int32, (1, rows), 1)
    cnt_r = (ri + 1).astype(F32)
    cnt_c = (ci + 1).astype(F32)
    for h in range(N_HEADS):
        log_gamma = _log_gamma(h)
        cols = slice(h * HEAD, (h + 1) * HEAD)
        gc = cnt_r * log_gamma
        decay = jnp.exp(jnp.where(ri >= ci, gc - cnt_c * log_gamma, -jnp.inf))
        qh = q_ref[:, cols]
        kh = k_ref[:, cols] * HEAD ** -0.5
        vh = v_ref[:, cols]
        state = st_ref[h]
        o = jnp.exp(gc) * _mm(qh, state) + _mm(_mm_nt(qh, kh) * decay, vh)
        gl = rows * log_gamma
        st_ref[h] = math.exp(gl) * state + _mm_tn(kh * jnp.exp(gl - gc), vh)
        o_ref[:, cols] = _layer_norm(o, gn_ref[...]).astype(o_ref.dtype)

    @pl.when(r == pl.num_programs(1) - 1)
    def _finish():
        sout_ref[...] = st_ref[...]


def _ret_step_kernel(q_ref, k_ref, v_ref, gn_ref, s0_ref, o_ref, sout_ref):
    def per_head(ref):
        return jnp.stack([ref[:, :, h * HEAD:(h + 1) * HEAD] for h in range(N_HEADS)], axis=1)

    q, k, v = per_head(q_ref), per_head(k_ref) * HEAD ** -0.5, per_head(v_ref)
    eye = _eye(HEAD)
    k_col = jnp.sum(jnp.where(eye, k, 0.0), axis=-1, keepdims=True)
    q_col = jnp.sum(jnp.where(eye, q, 0.0), axis=-1, keepdims=True)
    s0 = s0_ref[...]
    state = jnp.stack([math.exp(_log_gamma(h)) * s0[:, h] for h in range(N_HEADS)], axis=1) + k_col * v
    sout_ref[...] = state
    o = _layer_norm(jnp.sum(state * q_col, axis=2, keepdims=True), gn_ref[...])
    for h in range(N_HEADS):
        o_ref[:, :, h * HEAD:(h + 1) * HEAD] = o[:, h]


def _retention_step(q, k, v, gn, s0, seq_off):
    n_s = q.shape[0]
    nb = _step_block(n_s, seq_off)
    off = seq_off // nb
    row = pl.BlockSpec((nb, 1, WIDTH), lambda b: (b, 0, 0))
    st = pl.BlockSpec((nb, N_HEADS, HEAD, HEAD), lambda b: (b, 0, 0, 0))
    st_in = pl.BlockSpec((nb, N_HEADS, HEAD, HEAD), lambda b: (b + off, 0, 0, 0))
    r3 = lambda t: t.reshape(n_s, 1, WIDTH)
    o, s_new = pl.pallas_call(
        _ret_step_kernel,
        grid=(n_s // nb,),
        in_specs=[row, row, row, pl.BlockSpec((1, HEAD), lambda b: (0, 0)), st_in],
        out_specs=[row, st],
        out_shape=[jax.ShapeDtypeStruct((n_s, 1, WIDTH), F32),
                   jax.ShapeDtypeStruct((n_s, N_HEADS, HEAD, HEAD), F32)],
        compiler_params=_params(1),
        name="retention_step",
    )(r3(q), r3(k), r3(v), gn.reshape(1, HEAD), s0)
    return o.reshape(n_s, WIDTH), s_new


def _retention(q, k, v, gn, s0, n_b, seq):
    rows = min(RET_ROWS, seq)
    assert seq % rows == 0
    nr = seq // rows
    tok = pl.BlockSpec((rows, WIDTH), lambda b, r: (b * nr + r, 0))
    st = pl.BlockSpec((None, N_HEADS, HEAD, HEAD), lambda b, r: (b, 0, 0, 0))
    return pl.pallas_call(
        _ret_kernel,
        grid=(n_b, nr),
        in_specs=[tok, tok, tok, pl.BlockSpec((1, HEAD), lambda b, r: (0, 0)), st],
        out_specs=[pl.BlockSpec((rows, WIDTH), lambda b, r: (b * nr + r, 0)), st],
        out_shape=[jax.ShapeDtypeStruct((n_b * seq, WIDTH), BF16),
                   jax.ShapeDtypeStruct((n_b, N_HEADS, HEAD, HEAD), F32)],
        scratch_shapes=[pltpu.VMEM((N_HEADS, HEAD, HEAD), F32)],
        compiler_params=_params(2),
        name="retention",
    )(q, k, v, gn.reshape(1, HEAD), s0)


def _s5_prep_kernel(are_ref, aim_ref, ldt_ref, brt_ref, bit_ref, lbr_ref, lbi_ref, bbr_ref, bbi_ref):
    dt = jnp.exp(ldt_ref[...])
    ar, ai = are_ref[...], aim_ref[...]
    mag = jnp.exp(ar * dt)
    ang = ai * dt
    lr, li = mag * jnp.cos(ang), mag * jnp.sin(ang)
    den = ar * ar + ai * ai
    fr = ((lr - 1.0) * ar + li * ai) / den
    fi = (li * ar - (lr - 1.0) * ai) / den
    lbr_ref[...] = lr
    lbi_ref[...] = li
    brt, bit = brt_ref[...], bit_ref[...]
    bbr_ref[...] = fr[:, None, :] * brt - fi[:, None, :] * bit
    bbi_ref[...] = fr[:, None, :] * bit + fi[:, None, :] * brt


def _s5_prep(a_re, a_im, log_dt, b_re, b_im):
    n_g, n_p = a_re.shape
    brt, bit = jnp.swapaxes(b_re, 1, 2), jnp.swapaxes(b_im, 1, 2)
    gp = jax.ShapeDtypeStruct((n_g, n_p), F32)
    gcp = jax.ShapeDtypeStruct(brt.shape, F32)
    return pl.pallas_call(_s5_prep_kernel, out_shape=[gp, gp, gcp, gcp], name="s5_prep")(
        a_re, a_im, log_dt.reshape(n_g, 1), brt, bit)


def _block_diag(blocks):
    n, r, c = blocks.shape
    return jnp.einsum("grc,gh->grhc", blocks, jnp.eye(n, dtype=blocks.dtype)).reshape(n * r, n * c)


def _s5_kernel(n_b, n_t, u_ref, x0r_ref, x0i_ref, lbr_ref, lbi_ref, bb_ref, cc_ref, d_ref,
               y_ref, xr_ref, xi_ref, utb_ref, x_ref, carry_ref):
    i = pl.program_id(0)
    n_p = lbr_ref.shape[1]
    half_in = S5_HALF_GROUPS * GROUP_C
    half_st = S5_HALF_GROUPS * P_C

    @pl.when(i == 0)
    def _init():
        carry_ref[:, :n_p] = x0r_ref[...]
        carry_ref[:, n_p:] = x0i_ref[...]

    n_lc = WIDTH // LANES
    if n_t > 1:
        for b in range(n_b):
            for c in range(n_lc):
                utb_ref[c, pl.ds(b, n_t, stride=n_b), :] = u_ref[:, b * WIDTH + c * LANES:b * WIDTH + (c + 1) * LANES]
        u = jnp.concatenate([utb_ref[c] for c in range(n_lc)], axis=1)
    else:
        u = u_ref[...]
    ub = u.astype(BF16)
    for half in range(2):
        for part in range(2):
            c0 = part * n_p + half * half_st
            x_ref[:, c0:c0 + half_st] = jnp.dot(ub[:, half * half_in:(half + 1) * half_in],
                                                bb_ref[half * 2 + part], preferred_element_type=F32)

    if n_t == 1:
        lr, li = lbr_ref[...], lbi_ref[...]
        x0r, x0i = carry_ref[:, :n_p], carry_ref[:, n_p:]
        xr = lr * x0r - li * x0i + x_ref[:, :n_p]
        xi = lr * x0i + li * x0r + x_ref[:, n_p:]
        x_ref[:, :n_p] = xr
        x_ref[:, n_p:] = xi
        carry_ref[:, :n_p] = xr
        carry_ref[:, n_p:] = xi
    else:
        for lg in range(n_p // S5_LANE_GROUP):
            l0 = lg * S5_LANE_GROUP
            lr = jnp.broadcast_to(lbr_ref[:, l0:l0 + S5_LANE_GROUP], (n_b, S5_LANE_GROUP))
            li = jnp.broadcast_to(lbi_ref[:, l0:l0 + S5_LANE_GROUP], (n_b, S5_LANE_GROUP))

            def body(t, carry, l0=l0, lr=lr, li=li):
                xr, xi = carry
                row = pl.multiple_of(t * n_b, n_b)
                nxr = lr * xr - li * xi + x_ref[pl.ds(row, n_b), l0:l0 + S5_LANE_GROUP]
                nxi = lr * xi + li * xr + x_ref[pl.ds(row, n_b), n_p + l0:n_p + l0 + S5_LANE_GROUP]
                x_ref[pl.ds(row, n_b), l0:l0 + S5_LANE_GROUP] = nxr
                x_ref[pl.ds(row, n_b), n_p + l0:n_p + l0 + S5_LANE_GROUP] = nxi
                return nxr, nxi

            xr, xi = lax.fori_loop(
                0, n_t, body,
                (carry_ref[:, l0:l0 + S5_LANE_GROUP], carry_ref[:, n_p + l0:n_p + l0 + S5_LANE_GROUP]),
                unroll=4)
            carry_ref[:, l0:l0 + S5_LANE_GROUP] = xr
            carry_ref[:, n_p + l0:n_p + l0 + S5_LANE_GROUP] = xi

    ys = []
    for half in range(2):
        xr_b = x_ref[:, half * half_st:(half + 1) * half_st].astype(BF16)
        xi_b = x_ref[:, n_p + half * half_st:n_p + (half + 1) * half_st].astype(BF16)
        ys.append(jnp.dot(xr_b, cc_ref[half * 2], preferred_element_type=F32)
                  + jnp.dot(xi_b, cc_ref[half * 2 + 1], preferred_element_type=F32))
    yg = _gelu_tanh(jnp.concatenate(ys, axis=1) + d_ref[...] * u)
    if n_t > 1:
        for c in range(n_lc):
            utb_ref[c] = yg[:, c * LANES:(c + 1) * LANES]
        for b in range(n_b):
            for c in range(n_lc):
                y_ref[:, b * WIDTH + c * LANES:b * WIDTH + (c + 1) * LANES] = utb_ref[c, pl.ds(b, n_t, stride=n_b), :]
    else:
        y_ref[...] = yg

    @pl.when(i == pl.num_programs(0) - 1)
    def _finish():
        xr_ref[...] = carry_ref[:, :n_p]
        xi_ref[...] = carry_ref[:, n_p:]


def _s5(u, x0_re, x0_im, lb_re, lb_im, bb4, cc4, d_skip, n_b, seq):
    n_p = lb_re.shape[1]
    n_t = min(S5_STEPS, seq)
    n_steps = seq // n_t
    rows = n_t * n_b
    u_block = (n_t, n_b * WIDTH) if seq > 1 else (n_b, WIDTH)
    const2 = lambda i: (0, 0)
    const3 = lambda i: (0, 0, 0)
    return pl.pallas_call(
        functools.partial(_s5_kernel, n_b, n_t),
        grid=(n_steps,),
        in_specs=[pl.BlockSpec(u_block, lambda i: (i, 0)),
                  pl.BlockSpec((n_b, n_p), const2), pl.BlockSpec((n_b, n_p), const2),
                  pl.BlockSpec((1, n_p), const2), pl.BlockSpec((1, n_p), const2),
                  pl.BlockSpec(bb4.shape, const3), pl.BlockSpec(cc4.shape, const3),
                  pl.BlockSpec((1, WIDTH), const2)],
        out_specs=[pl.BlockSpec(u_block, lambda i: (i, 0)),
                   pl.BlockSpec((n_b, n_p), const2), pl.BlockSpec((n_b, n_p), const2)],
        out_shape=[jax.ShapeDtypeStruct(u.shape, F32),
                   jax.ShapeDtypeStruct((n_b, n_p), F32), jax.ShapeDtypeStruct((n_b, n_p), F32)],
        scratch_shapes=[pltpu.VMEM((WIDTH // LANES, rows, LANES), F32), pltpu.VMEM((rows, 2 * n_p), F32),
                        pltpu.VMEM((n_b, 2 * n_p), F32)],
        compiler_params=_params(1),
        name="s5_scan",
    )(u, x0_re, x0_im, lb_re, lb_im, bb4, cc4, d_skip.reshape(1, WIDTH))


def _out_kernel(glu, a1_ref, g1_ref, a2_ref, g2_ref, x_ref, gate_ref, w_ref, wg_ref, o_ref):
    a1 = a1_ref[...].astype(F32)
    if glu:
        a1 = a1 * _sigmoid(jnp.dot(a1.astype(BF16), wg_ref[...], preferred_element_type=F32))
    m1 = a1 * g1_ref[...].astype(F32)
    m2 = a2_ref[...].astype(F32) * g2_ref[...].astype(F32)
    y = (jnp.dot(m1.astype(BF16), w_ref[0:WIDTH, :], preferred_element_type=F32)
         + jnp.dot(m2.astype(BF16), w_ref[WIDTH:2 * WIDTH, :], preferred_element_type=F32))
    o_ref[...] = x_ref[...] + gate_ref[...] * y


def _out_proj(a1, g1, a2, g2, x3, gate, w, wg, glu, a1_by_time):
    n_g, rows, d = x3.shape
    tm = min(ROW_TILE, rows)
    nt = rows // tm
    r_mod = gate.shape[1]
    tok = pl.BlockSpec((tm, WIDTH), lambda gi, i: (gi * nt + i, 0))
    a1_spec = pl.BlockSpec((tm, WIDTH), lambda gi, i: (i, gi)) if a1_by_time else tok
    const = lambda gi, i: (0, 0)
    return pl.pallas_call(
        functools.partial(_out_kernel, glu),
        grid=(n_g, nt),
        in_specs=[a1_spec, tok, tok, tok,
                  pl.BlockSpec((None, tm, d), lambda gi, i: (gi, i, 0)),
                  pl.BlockSpec((None, r_mod, d), lambda gi, i: (gi, 0, 0)),
                  pl.BlockSpec(w.shape, const), pl.BlockSpec(wg.shape, const)],
        out_specs=pl.BlockSpec((None, tm, d), lambda gi, i: (gi, i, 0)),
        out_shape=jax.ShapeDtypeStruct(x3.shape, F32),
        compiler_params=_params(2),
        name="out_proj",
    )(a1, g1, a2, g2, x3, gate, w, wg)


def _pad_cols(w, n):
    return jnp.pad(w, ((0, 0), (0, n - w.shape[1])))


def _even_layer(li, x3, mods, norm_g, p, attn_fn, delta_fn, kv_heads):
    shift, scale, gate = mods
    n_g, rows, d = x3.shape
    lam_init = 0.8 - 0.6 * math.exp(-0.3 * li)
    n_cols = p["w_in"].shape[1]
    n_pad = -(-n_cols // LANES) * LANES
    w_in = _pad_cols(p["w_in"], n_pad).astype(BF16)
    kv = "heads" if kv_heads else "rows"
    segs = ((0, WIDTH, "qnorm", BF16, "rows"), (WIDTH, WIDTH, "knorm", F32, kv),
            (2 * WIDTH, WIDTH, "raw", F32, kv), (3 * WIDTH, WIDTH, "silu", BF16, "rows"),
            (4 * WIDTH, 3 * WIDTH, "raw", F32, "rows"), (7 * WIDTH, WIDTH, "silu", BF16, "rows"),
            (8 * WIDTH, LANES, "raw", F32, "rows"))
    bd = _block_diag(jnp.full((WIDTH // DQK, DQK, DQK), 1.0 / DQK, F32)).astype(BF16)
    qg = jnp.tile(p["qn_g"], WIDTH // DQK).reshape(1, WIDTH)
    kg = jnp.tile(p["kn_g"], WIDTH // DQK).reshape(1, WIDTH)
    outs = _in_proj(x3, shift, scale, norm_g, w_in, segs, bd, qg, kg)
    if kv_heads:
        q, k, k_out, v, v_out, za, qkv_b, zb, ab = outs
    else:
        q, k, v, za, qkv_b, zb, ab = outs
        k_out, v_out = k, v
    lamp = jnp.zeros((SUBLANES, LANES), F32)
    for i, name in enumerate(("lam_q1", "lam_k1", "lam_q2", "lam_k2")):
        lamp = lamp.at[i, :DQK].set(p[name])
    sg = p["subln_g"].reshape(1, HEAD)
    oa = attn_fn(q, k, v, lamp=lamp, sg=sg, lam_init=lam_init)
    ob, conv_new, s_new = delta_fn(qkv_b, ab, p["conv_w"], p["a_log"], p["dt_bias"], p["gn_b"])
    w_out = p["w_out"].astype(BF16)
    x_new = _out_proj(oa, za, ob, zb, x3, gate, w_out, jnp.zeros((SUBLANES, LANES), BF16), False, False)
    return x_new, (k_out, v_out, conv_new, s_new)


def _odd_layer(x3, mods, norm_g, p, s5_mats, x0_re, x0_im, r0, by_time, seq_off=0):
    shift, scale, gate = mods
    n_g, rows, d = x3.shape
    w_in = p["w_in"].astype(BF16)
    segs = ((0, WIDTH, "raw", F32, "time" if by_time else "rows"), (WIDTH, WIDTH, "silu", BF16, "rows"),
            (2 * WIDTH, WIDTH, "raw", F32, "rows"), (3 * WIDTH, WIDTH, "raw", F32, "rows"),
            (4 * WIDTH, WIDTH, "raw", F32, "rows"), (5 * WIDTH, WIDTH, "silu", BF16, "rows"))
    dummy = jnp.zeros((SUBLANES, LANES), BF16)
    dummy_g = jnp.zeros((1, LANES), F32)
    u, zc, qd, kd, vd, zd = _in_proj(x3, shift, scale, norm_g, w_in, segs, dummy, dummy_g, dummy_g)
    lb_re, lb_im, bb4, cc4 = s5_mats
    n_b = n_g if by_time else rows
    seq = rows if by_time else 1
    yg, xr, xi = _s5(u, x0_re, x0_im, lb_re, lb_im, bb4, cc4, p["s5_d"], n_b, seq)
    if by_time:
        od, r_new = _retention(qd, kd, vd, p["gn_d"], r0, n_g, rows)
    else:
        od, r_new = _retention_step(qd, kd, vd, p["gn_d"], r0, seq_off)
    x_new = _out_proj(yg, zc, od, zd, x3, gate, p["w_out"].astype(BF16), p["w_glu"].astype(BF16), True, by_time)
    return x_new, (xr, xi, r_new)


def kernel(x_prompt, x_sample, c_prompt, c_sample, page_table, cache_k, cache_v, state_b_conv, state_b_ssm,
           state_c_re, state_c_im, state_d_ret, norm_g, w_ada, b_ada, w_in_e, w_out_e, qn_g, kn_g,
           lam_q1, lam_k1, lam_q2, lam_k2, subln_g, conv_w, a_log, dt_bias, gn_b, w_in_o, w_out_o,
           s5_a_re, s5_a_im, s5_b_re, s5_b_im, s5_c_re, s5_c_im, s5_d, s5_log_dt, w_glu, gn_d):
    n_bp, seq, d = x_prompt.shape
    n_bs = x_sample.shape[0]
    depth = norm_g.shape[0]
    n_pages, page = page_table.shape[1], cache_k.shape[2]
    n_g, n_p = s5_a_re.shape[1], s5_a_re.shape[2]
    conv_all = state_b_conv.reshape(-1, K_CONV - 1, 3 * WIDTH)
    ssm_all = state_b_ssm.reshape(-1, N_HEADS, HEAD, HEAD)
    ret_all = state_d_ret.reshape(-1, N_HEADS, HEAD, HEAD)

    mod = _modulation(jnp.concatenate([c_prompt, c_sample], axis=0), w_ada, b_ada)
    xp = x_prompt
    xs = x_sample.reshape(1, n_bs, d)
    outs = {name: [] for name in ("k_p", "v_p", "k_s", "v_s", "cv_p", "cv_s", "dl_p", "dl_s",
                                  "s5r_p", "s5i_p", "s5r_s", "s5i_s", "rt_p", "rt_s")}
    for li in range(depth):
        mods_p = tuple(mod[li, :n_bp, j * d:(j + 1) * d].reshape(n_bp, 1, d) for j in range(3))
        mods_s = tuple(mod[li, n_bp:, j * d:(j + 1) * d].reshape(1, n_bs, d) for j in range(3))
        if li % 2 == 0:
            e = li // 2
            p = dict(w_in=w_in_e[e], w_out=w_out_e[e], qn_g=qn_g[e], kn_g=kn_g[e], lam_q1=lam_q1[e],
                     lam_k1=lam_k1[e], lam_q2=lam_q2[e], lam_k2=lam_k2[e], subln_g=subln_g[e],
                     conv_w=conv_w[e], a_log=a_log[e], dt_bias=dt_bias[e], gn_b=gn_b[e])
            attn_p = functools.partial(_attn_prompt, n_b=n_bp, seq=seq)
            delta_p = functools.partial(
                _delta, cbuf=jnp.zeros((n_bp, K_CONV - 1, 3 * WIDTH), F32),
                s0=jnp.zeros((n_bp, N_HEADS, HEAD, HEAD), F32), n_b=n_bp, seq=seq)
            xp, (k_p, v_p, c_p, s_p) = _even_layer(li, xp, mods_p, norm_g[li], p, attn_p, delta_p, True)
            attn_s = functools.partial(_attn_decode, cache_k=cache_k, cache_v=cache_v, page_table=page_table,
                                       layer=e)
            delta_s = functools.partial(_delta_step, cbuf=conv_all, s0=ssm_all, seq_off=e * n_bs)
            xs, (k_s, v_s, c_s, s_s) = _even_layer(li, xs, mods_s, norm_g[li], p, attn_s, delta_s, False)
            outs["k_p"].append(k_p.reshape(n_bp, seq, N_HEADS, HEAD))
            outs["v_p"].append(v_p.reshape(n_bp, seq, N_HEADS, HEAD))
            outs["k_s"].append(k_s.reshape(n_bs, 1, N_HEADS, HEAD))
            outs["v_s"].append(v_s.reshape(n_bs, 1, N_HEADS, HEAD))
            outs["cv_p"].append(c_p)
            outs["cv_s"].append(c_s)
            outs["dl_p"].append(s_p)
            outs["dl_s"].append(s_s)
        else:
            o = li // 2
            p = dict(w_in=w_in_o[o], w_out=w_out_o[o], s5_d=s5_d[o], w_glu=w_glu[o], gn_d=gn_d[o])
            lb_re, lb_im, bbr, bbi = _s5_prep(s5_a_re[o], s5_a_im[o], s5_log_dt[o], s5_b_re[o], s5_b_im[o])
            hg = S5_HALF_GROUPS
            bb4 = jnp.stack([_block_diag(t[h * hg:(h + 1) * hg]) for h in range(2) for t in (bbr, bbi)]).astype(BF16)
            cre = jnp.swapaxes(s5_c_re[o], 1, 2)
            cim = -jnp.swapaxes(s5_c_im[o], 1, 2)
            cc4 = jnp.stack([_block_diag(t[h * hg:(h + 1) * hg]) for h in range(2) for t in (cre, cim)]).astype(BF16)
            s5_mats = (lb_re.reshape(1, n_g * n_p), lb_im.reshape(1, n_g * n_p), bb4, cc4)
            zeros_c = jnp.zeros((n_bp, n_g * n_p), F32)
            xp, (r_p, i_p, t_p) = _odd_layer(xp, mods_p, norm_g[li], p, s5_mats, zeros_c, zeros_c,
                                             jnp.zeros((n_bp, N_HEADS, HEAD, HEAD), F32), True)
            xs, (r_s, i_s, t_s) = _odd_layer(xs, mods_s, norm_g[li], p, s5_mats,
                                             state_c_re[o].reshape(n_bs, n_g * n_p),
                                             state_c_im[o].reshape(n_bs, n_g * n_p), ret_all, False, o * n_bs)
            outs["s5r_p"].append(r_p.reshape(n_bp, n_g, n_p))
            outs["s5i_p"].append(i_p.reshape(n_bp, n_g, n_p))
            outs["s5r_s"].append(r_s.reshape(n_bs, n_g, n_p))
            outs["s5i_s"].append(i_s.reshape(n_bs, n_g, n_p))
            outs["rt_p"].append(t_p)
            outs["rt_s"].append(t_s)
    st = lambda name: jnp.stack(outs[name])
    return (xp, xs.reshape(n_bs, 1, d), st("k_p"), st("v_p"), st("k_s"), st("v_s"), st("cv_p"), st("cv_s"),
            st("dl_p"), st("dl_s"), st("s5r_p"), st("s5i_p"), st("s5r_s"), st("s5i_s"), st("rt_p"), st("rt_s"))
```

```python
import functools
import math

import jax
import jax.numpy as jnp
from jax import lax
from jax.experimental import pallas as pl
from jax.experimental.pallas import tpu as pltpu

F32 = jnp.float32
BF16 = jnp.bfloat16
EPS = 1e-6

LANES = 128
SUBLANES = 8
VMEM_LIMIT_BYTES = 48 * 1024 * 1024

HEAD = 128
DQK = HEAD // 2
LOG2E = math.log2(math.e)
N_HEADS = 4
WIDTH = N_HEADS * HEAD
K_CONV = 4
DELTA_CHUNK = 64
GROUP_C = 16
P_C = 64
S5_HALF_GROUPS = 16

ROW_TILE = 256
OUT_ROW_TILE = 512
ATTN_TILE = 512
ATTN_ROW_CHUNK = 256
DELTA_ROWS = 256
DELTA_SEQS = 1
RET_ROWS = 256
S5_STEPS = 64
S5_LANE_GROUP = 1024
STEP_SEQS = 8


def _params(n_axes, vmem=VMEM_LIMIT_BYTES):
    return pltpu.CompilerParams(dimension_semantics=("arbitrary",) * n_axes, vmem_limit_bytes=vmem)


def _sigmoid(x):
    return 1.0 / (1.0 + jnp.exp(-x))


def _silu(x):
    return x * _sigmoid(x)


def _softplus(x):
    return jnp.maximum(x, 0.0) + jnp.log1p(jnp.exp(-jnp.abs(x)))


def _gelu_tanh(x):
    return 0.5 * x * (1.0 + jnp.tanh(math.sqrt(2.0 / math.pi) * (x + 0.044715 * (x * x * x))))


def _mm(a, b):
    if a.ndim == 3:
        return lax.dot_general(a.astype(BF16), b.astype(BF16), (((2,), (1,)), ((0,), (0,))),
                               preferred_element_type=F32)
    return jnp.dot(a.astype(BF16), b.astype(BF16), preferred_element_type=F32)


def _mm_nt(a, b):
    if a.ndim == 3:
        return lax.dot_general(a.astype(BF16), b.astype(BF16), (((2,), (2,)), ((0,), (0,))),
                               preferred_element_type=F32)
    return lax.dot_general(a.astype(BF16), b.astype(BF16), (((1,), (1,)), ((), ())),
                           preferred_element_type=F32)


def _mod_kernel(c_ref, w_ref, b_ref, o_ref):
    o_ref[...] = _mm(_silu(c_ref[...]), w_ref[...]) + b_ref[...]


def _modulation(c_all, w_ada, b_ada):
    depth, d, n3 = w_ada.shape
    rows = c_all.shape[0]
    tn = d
    return pl.pallas_call(
        _mod_kernel,
        grid=(depth, n3 // tn),
        in_specs=[pl.BlockSpec((rows, d), lambda l, n: (0, 0)),
                  pl.BlockSpec((None, d, tn), lambda l, n: (l, 0, n)),
                  pl.BlockSpec((None, 1, tn), lambda l, n: (l, 0, n))],
        out_specs=pl.BlockSpec((None, rows, tn), lambda l, n: (l, 0, n)),
        out_shape=jax.ShapeDtypeStruct((depth, rows, n3), F32),
        compiler_params=_params(2),
        name="modulation",
    )(c_all, w_ada, b_ada.reshape(depth, 1, n3))


def _in_kernel(segs, x_ref, shift_ref, scale_ref, g_ref, w_ref, bd_ref, qg_ref, kg_ref, *refs):
    out_refs, h_ref = iter(refs[:-1]), refs[-1]
    x = x_ref[...]
    h = x * lax.rsqrt(jnp.mean(x * x, axis=-1, keepdims=True) + EPS) * g_ref[...]
    h_ref[...] = (h * (1.0 + scale_ref[...]) + shift_ref[...]).astype(BF16)
    for c0, width, kind, layout in segs:
        acc = jnp.dot(h_ref[...], w_ref[:, c0:c0 + width], preferred_element_type=F32)
        if kind == "silu":
            acc = _silu(acc)
        elif kind in ("qnorm", "knorm"):
            ms = jnp.dot((acc * acc).astype(BF16), bd_ref[...], preferred_element_type=F32)
            acc = acc * lax.rsqrt(ms + EPS) * (qg_ref if kind == "qnorm" else kg_ref)[...]
            if kind == "qnorm":
                acc = acc * (DQK ** -0.5 * LOG2E)
        o_ref = next(out_refs)
        o_ref[...] = acc.astype(o_ref.dtype)
        if layout == "heads":
            hm_ref = next(out_refs)
            n_h = width // HEAD
            for h in range(n_h):
                hm_ref[pl.ds(h, acc.shape[0], stride=n_h), :] = acc[:, h * HEAD:(h + 1) * HEAD]


def _in_proj(x3, shift, scale, g, w, segs, bd, qg, kg):
    n_g, rows, d = x3.shape
    tm = min(ROW_TILE, rows)
    nt = rows // tm
    r_mod = shift.shape[1]
    assert r_mod in (1, tm)
    out_shapes, out_specs = [], []
    row_map = lambda gi, i: (gi * nt + i, 0)
    for _, width, _, dtype, layout in segs:
        if layout == "time":
            out_shapes.append(jax.ShapeDtypeStruct((rows, n_g * width), dtype))
            out_specs.append(pl.BlockSpec((tm, width), lambda gi, i: (i, gi)))
        elif layout == "heads":
            n_h = width // HEAD
            out_shapes += [jax.ShapeDtypeStruct((n_g * rows, width), BF16),
                           jax.ShapeDtypeStruct((n_g * rows * n_h, HEAD), dtype)]
            out_specs += [pl.BlockSpec((tm, width), row_map), pl.BlockSpec((tm * n_h, HEAD), row_map)]
        else:
            out_shapes.append(jax.ShapeDtypeStruct((n_g * rows, width), dtype))
            out_specs.append(pl.BlockSpec((tm, width), row_map))
    const = lambda gi, i: (0, 0)
    return pl.pallas_call(
        functools.partial(_in_kernel, tuple((s[0], s[1], s[2], s[4]) for s in segs)),
        grid=(n_g, nt),
        in_specs=[pl.BlockSpec((None, tm, d), lambda gi, i: (gi, i, 0)),
                  pl.BlockSpec((None, r_mod, d), lambda gi, i: (gi, 0, 0)),
                  pl.BlockSpec((None, r_mod, d), lambda gi, i: (gi, 0, 0)),
                  pl.BlockSpec((1, d), const),
                  pl.BlockSpec(w.shape, const),
                  pl.BlockSpec(bd.shape, const),
                  pl.BlockSpec(qg.shape, const),
                  pl.BlockSpec(kg.shape, const)],
        out_specs=out_specs,
        out_shape=out_shapes,
        scratch_shapes=[pltpu.VMEM((tm, d), BF16)],
        compiler_params=_params(2),
        name="in_proj",
    )(x3, shift, scale, g.reshape(1, d), w, bd, qg, kg)


def _lambda_value(lamp, lam_init):
    a = jnp.sum(lamp[0:1] * lamp[1:2], axis=-1, keepdims=True)
    b = jnp.sum(lamp[2:3] * lamp[3:4], axis=-1, keepdims=True)
    return jnp.exp(a) - jnp.exp(b) + lam_init


def _sub_ln(o, gain, lam_init):
    return o * lax.rsqrt(jnp.mean(o * o, axis=-1, keepdims=True) + EPS) * gain * (1.0 - lam_init)


def _attn_kernel(lam_init, row_chunk, qt_ref, kt_ref, q_ref, k_ref, v_ref, kbias_ref, lamp_ref, sg_ref, o_ref,
                 qs_ref, m_ref, acc_ref):
    qi, ki = qt_ref[pl.program_id(2)], kt_ref[pl.program_id(2)]
    tq, tk = q_ref.shape[0], k_ref.shape[0]

    @pl.when(ki == 0)
    def _init():
        m_ref[...] = jnp.full(m_ref.shape, -jnp.inf, F32)
        acc_ref[...] = jnp.zeros(acc_ref.shape, F32)
        q = q_ref[...]
        lane = lax.broadcasted_iota(jnp.int32, (1, HEAD), 1)
        qs_ref[0:tq, :] = jnp.where(lane < DQK, q, jnp.zeros_like(q))
        qs_ref[tq:2 * tq, :] = jnp.where(lane >= DQK, q, jnp.zeros_like(q))

    def step(masked):
        kb = k_ref[...].astype(BF16)
        ones_lane = lax.broadcasted_iota(jnp.int32, (tk, HEAD), 1) == 0
        vb = jnp.concatenate([v_ref[...].astype(BF16), jnp.where(ones_lane, 1.0, 0.0).astype(BF16)], axis=1)
        kbias = kbias_ref[...]
        for r0 in range(0, 2 * tq, row_chunk):
            rs = slice(r0, r0 + row_chunk)
            n_k = min(tk, r0 % tq + row_chunk) if masked else tk
            s = _mm_nt(qs_ref[rs, :], kb[:n_k]) + kbias[:, :n_k]
            if masked:
                q_row = lax.broadcasted_iota(jnp.int32, (row_chunk, n_k), 0) + (r0 % tq)
                col = lax.broadcasted_iota(jnp.int32, (row_chunk, n_k), 1)
                s = jnp.where(col <= q_row, s, -jnp.inf)
            m_prev = m_ref[rs, :]
            m_new = jnp.maximum(m_prev, jnp.max(s, axis=-1, keepdims=True))
            alpha = jnp.exp2(m_prev - m_new)
            p = jnp.exp2(s - m_new)
            acc_ref[rs, :] = alpha * acc_ref[rs, :] + jnp.dot(p.astype(BF16), vb[:n_k], preferred_element_type=F32)
            m_ref[rs, :] = m_new

    @pl.when(ki < qi)
    def _off_diagonal():
        step(False)

    @pl.when(ki == qi)
    def _diagonal():
        step(True)
        lam = _lambda_value(lamp_ref[...], lam_init)
        o = (acc_ref[0:tq, 0:HEAD] / acc_ref[0:tq, HEAD:HEAD + 1]
             - lam * (acc_ref[tq:2 * tq, 0:HEAD] / acc_ref[tq:2 * tq, HEAD:HEAD + 1]))
        o_ref[...] = _sub_ln(o, sg_ref[...], lam_init).astype(o_ref.dtype)


def _alibi_slopes(n_heads):
    return jnp.exp2(-8.0 * jnp.arange(1, n_heads + 1, dtype=F32) / n_heads)


def _attn_prompt(q, k, v, lamp, sg, lam_init, n_b, seq):
    t = min(ATTN_TILE, seq)
    nq = seq // t
    row_chunk = min(ATTN_ROW_CHUNK, t)
    kbias = (LOG2E * _alibi_slopes(N_HEADS)[:, None] * jnp.arange(seq, dtype=F32)[None, :]).reshape(N_HEADS, nq, 1, t)
    pairs = [(qi, ki) for qi in range(nq) for ki in range(qi + 1)]
    qt = jnp.asarray([p[0] for p in pairs], jnp.int32)
    kt = jnp.asarray([p[1] for p in pairs], jnp.int32)
    q_map = lambda b, h, t_, qt_, kt_: (b * nq + qt_[t_], h)
    kv_map = lambda b, h, t_, qt_, kt_: (b * nq + kt_[t_], h)
    const = lambda b, h, t_, qt_, kt_: (0, 0)
    grid_spec = pltpu.PrefetchScalarGridSpec(
        num_scalar_prefetch=2,
        grid=(n_b, N_HEADS, len(pairs)),
        in_specs=[pl.BlockSpec((t, HEAD), q_map),
                  pl.BlockSpec((t, HEAD), kv_map),
                  pl.BlockSpec((t, HEAD), kv_map),
                  pl.BlockSpec((None, None, 1, t), lambda b, h, t_, qt_, kt_: (h, kt_[t_], 0, 0)),
                  pl.BlockSpec(lamp.shape, const),
                  pl.BlockSpec((1, HEAD), const)],
        out_specs=pl.BlockSpec((t, HEAD), q_map),
        scratch_shapes=[pltpu.VMEM((2 * t, HEAD), BF16), pltpu.VMEM((2 * t, 1), F32),
                        pltpu.VMEM((2 * t, 2 * HEAD), F32)],
    )
    return pl.pallas_call(
        functools.partial(_attn_kernel, lam_init, row_chunk),
        grid_spec=grid_spec,
        out_shape=jax.ShapeDtypeStruct((n_b * seq, WIDTH), BF16),
        compiler_params=_params(3),
        name="attn_prompt",
    )(qt, kt, q, k, v, kbias, lamp, sg)


def _decode_kernel(lam_init, n_pages, pt_ref, q_ref, kn_ref, vn_ref, nsl_ref, lamp_ref, sg_ref, *refs):
    del pt_ref
    k_refs, v_refs, o_ref = refs[:n_pages], refs[n_pages:2 * n_pages], refs[2 * n_pages]
    cols_page = k_refs[0].shape[0]
    n_rows = 4 * N_HEADS
    row = lax.broadcasted_iota(jnp.int32, (n_rows, HEAD), 0)
    lane = lax.broadcasted_iota(jnp.int32, (n_rows, HEAD), 1)

    def per_row_head(x):
        out = jnp.zeros((n_rows, HEAD), F32)
        for h in range(N_HEADS):
            out = jnp.where((row % N_HEADS == h) & (row < 2 * N_HEADS), x[:, h * HEAD:(h + 1) * HEAD], out)
        return out

    qm = jnp.where(lane // DQK == row // N_HEADS, per_row_head(q_ref[...].astype(F32)), 0.0).astype(BF16)
    s = jnp.concatenate([_mm_nt(qm, k_refs[j][...]) for j in range(n_pages)], axis=1)
    n_cols = n_pages * cols_page
    past = n_cols // N_HEADS
    col = lax.broadcasted_iota(jnp.int32, (n_rows, n_cols), 1)
    row_c = lax.broadcasted_iota(jnp.int32, (n_rows, n_cols), 0)
    own_head = (col % N_HEADS == row_c % N_HEADS) & (row_c < 2 * N_HEADS)
    s = jnp.where(own_head, s + nsl_ref[...] * (past - col // N_HEADS).astype(F32), -jnp.inf)
    s_self = jnp.sum(qm.astype(F32) * per_row_head(kn_ref[...]).astype(BF16).astype(F32), axis=-1, keepdims=True)
    m = jnp.maximum(jnp.max(s, axis=-1, keepdims=True), s_self)
    p = jnp.exp2(s - m)
    p_self = jnp.exp2(s_self - m)
    denom = jnp.sum(p, axis=-1, keepdims=True) + p_self
    lam = _lambda_value(lamp_ref[...], lam_init)
    r1 = row[:, 0:1]
    coef = jnp.where(r1 < N_HEADS, 1.0, jnp.where(r1 < 2 * N_HEADS, -lam, 0.0)) / denom
    pw = p * coef
    acc = (p_self * coef) * per_row_head(vn_ref[...]).astype(BF16).astype(F32)
    for j in range(n_pages):
        acc = acc + _mm(pw[:, j * cols_page:(j + 1) * cols_page], v_refs[j][...])
    for h in range(N_HEADS):
        o = acc[h:h + 1, :] + acc[N_HEADS + h:N_HEADS + h + 1, :]
        o_ref[:, h * HEAD:(h + 1) * HEAD] = _sub_ln(o, sg_ref[...], lam_init)


def _attn_decode(q, k_new, v_new, cache_k, cache_v, page_table, lamp, sg, lam_init, layer):
    n_s, n_pages = page_table.shape
    n_l, n_phys, page = cache_k.shape[:3]
    cache_k = cache_k.reshape(n_l, n_phys, page * N_HEADS, HEAD)
    cache_v = cache_v.reshape(n_l, n_phys, page * N_HEADS, HEAD)
    nsl = -LOG2E * _alibi_slopes(N_HEADS)
    nsl8 = jnp.concatenate([nsl, nsl, jnp.zeros((2 * N_HEADS,), F32)]).reshape(4 * N_HEADS, 1)
    row_spec = pl.BlockSpec((None, 1, WIDTH), lambda b, pt: (b, 0, 0))
    const = lambda b, pt: (0, 0)

    def page_spec(j):
        return pl.BlockSpec((None, None, page * N_HEADS, HEAD), lambda b, pt: (layer, pt[b, j], 0, 0))

    grid_spec = pltpu.PrefetchScalarGridSpec(
        num_scalar_prefetch=1,
        grid=(n_s,),
        in_specs=[row_spec, row_spec, row_spec,
                  pl.BlockSpec(nsl8.shape, const), pl.BlockSpec(lamp.shape, const),
                  pl.BlockSpec((1, HEAD), const)]
                 + [page_spec(j) for j in range(n_pages)] * 2,
        out_specs=row_spec,
    )
    out = pl.pallas_call(
        functools.partial(_decode_kernel, lam_init, n_pages),
        grid_spec=grid_spec,
        out_shape=jax.ShapeDtypeStruct((n_s, 1, WIDTH), F32),
        compiler_params=_params(1),
        name="attn_decode",
    )(page_table, q.reshape(n_s, 1, WIDTH), k_new.reshape(n_s, 1, WIDTH), v_new.reshape(n_s, 1, WIDTH),
      nsl8, lamp, sg, *([cache_k] * n_pages), *([cache_v] * n_pages))
    return out.reshape(n_s, WIDTH)


INV_BASE = SUBLANES


def _unit_lower_inverse(a, block, row, col):
    in_base = row // INV_BASE == col // INV_BASE
    power = jnp.where(in_base, a, 0.0)
    inv = jnp.where(row == col, 1.0, 0.0) - power
    order = 2
    while order < INV_BASE:
        power = _mm(power, power)
        inv = inv + _mm(inv, power)
        order *= 2
    size = INV_BASE
    while size < block:
        coupling = jnp.where((row // (2 * size) == col // (2 * size)) & (row // size != col // size), a, 0.0)
        inv = inv - _mm(_mm(inv, coupling), inv)
        size *= 2
    return inv


def _mm_tn(a, b):
    return lax.dot_general(a.astype(BF16), b.astype(BF16), (((0,), (0,)), ((), ())),
                           preferred_element_type=F32)


def _delta_kernel(chunk, qkv_ref, ab_ref, cw_ref, hp_ref, gn_ref, cbuf_ref, s0_ref, tri_ref,
                  o_ref, cnew_ref, sout_ref, xs_ref, st_ref):
    r = pl.program_id(1)
    n_seq = qkv_ref.shape[0]
    tail = K_CONV - 1

    @pl.when(r == 0)
    def _init():
        st_ref[...] = s0_ref[...]
        for i in range(n_seq):
            xs_ref[i, 0:SUBLANES, :] = jnp.zeros((SUBLANES, xs_ref.shape[2]), F32)
            xs_ref[i, SUBLANES - tail:SUBLANES, :] = cbuf_ref[i]

    for i in range(n_seq):
        _delta_sequence(chunk, qkv_ref.at[i], ab_ref.at[i], cw_ref, hp_ref, gn_ref, tri_ref,
                        o_ref.at[i], cnew_ref.at[i], xs_ref.at[i], st_ref.at[i])

    @pl.when(r == pl.num_programs(1) - 1)
    def _finish():
        sout_ref[...] = st_ref[...]


def _delta_sequence(chunk, qkv_ref, ab_ref, cw_ref, hp_ref, gn_ref, tri_ref, o_ref, cnew_ref, xs_ref, st_ref):
    rows = qkv_ref.shape[0]
    n_chunks = rows // chunk
    tail = K_CONV - 1
    xs_ref[SUBLANES:SUBLANES + rows, :] = qkv_ref[...]
    w = cw_ref[...]
    y = w[0:1] * xs_ref[SUBLANES - 3:SUBLANES - 3 + rows, :]
    for j in range(1, K_CONV):
        y = y + w[j:j + 1] * xs_ref[SUBLANES - 3 + j:SUBLANES - 3 + j + rows, :]
    new_tail = xs_ref[SUBLANES + rows - tail:SUBLANES + rows, :]
    cnew_ref[...] = new_tail
    xs_ref[SUBLANES - tail:SUBLANES, :] = new_tail
    act = _silu(y)

    ab = ab_ref[...]
    hp = hp_ref[...]
    g_all = -jnp.exp(hp[0:1]) * _softplus(ab + hp[1:2])
    beta_all = _sigmoid(ab)
    gam_all = jnp.dot(tri_ref[...], g_all, preferred_element_type=F32, precision=lax.Precision.HIGHEST)

    n_st = N_HEADS * chunk
    ri = lax.broadcasted_iota(jnp.int32, (n_st, n_st), 0)
    ci = lax.broadcasted_iota(jnp.int32, (n_st, n_st), 1)
    same_head = ri // chunk == ci // chunk
    incl, strict, diag = same_head & (ri >= ci), same_head & (ri > ci), ri == ci
    gn = gn_ref[...]
    local = [slice(h * chunk, (h + 1) * chunk) for h in range(N_HEADS)]
    states = [st_ref[h] for h in range(N_HEADS)]

    def stacked(col0):
        return jnp.stack([jnp.concatenate([act[c * chunk:(c + 1) * chunk, col0 + h * HEAD:col0 + (h + 1) * HEAD]
                                           for h in range(N_HEADS)], axis=0) for c in range(n_chunks)], axis=0)

    def stacked_column(t, lane0):
        return jnp.stack([jnp.concatenate([t[c * chunk:(c + 1) * chunk, lane0 + h:lane0 + h + 1]
                                           for h in range(N_HEADS)], axis=0) for c in range(n_chunks)], axis=0)

    q, k, v = stacked(0), stacked(WIDTH), stacked(2 * WIDTH)
    q = q * lax.rsqrt(jnp.sum(q * q, axis=-1, keepdims=True) + EPS) * HEAD ** -0.5
    k = k * lax.rsqrt(jnp.sum(k * k, axis=-1, keepdims=True) + EPS)
    gc = stacked_column(gam_all, 0)
    bc = stacked_column(beta_all, N_HEADS)
    gr = jnp.sum(jnp.where(diag, gc, 0.0), axis=1, keepdims=True)
    decay = jnp.exp(jnp.where(incl, gc - gr, -jnp.inf))
    inv = _unit_lower_inverse(jnp.where(strict, bc * decay * _mm_nt(k, k), 0.0), chunk, ri, ci)
    eg = jnp.exp(gc)
    sol_all = _mm(inv, jnp.concatenate([bc * v, (bc * eg) * k], axis=2))
    qk_all = _mm_nt(q, k) * decay

    for c in range(n_chunks):
        sol, qk, q_c, k_c, gc_c = sol_all[c], qk_all[c], q[c], k[c], gc[c]
        u = jnp.concatenate([sol[ls, :HEAD] - _mm(sol[ls, HEAD:], states[h]) for h, ls in enumerate(local)], axis=0)
        o = eg[c] * jnp.concatenate([_mm(q_c[ls], states[h]) for h, ls in enumerate(local)], axis=0) + _mm(qk, u)
        for h, ls in enumerate(local):
            gl = gc_c[ls.stop - 1:ls.stop, :]
            states[h] = jnp.exp(gl) * states[h] + _mm_tn(k_c[ls] * jnp.exp(gl - gc_c[ls]), u[ls])
        on = o * lax.rsqrt(jnp.mean(o * o, axis=-1, keepdims=True) + EPS) * gn
        for h, ls in enumerate(local):
            o_ref[c * chunk:(c + 1) * chunk, h * HEAD:(h + 1) * HEAD] = on[ls].astype(o_ref.dtype)

    for h in range(N_HEADS):
        st_ref[h] = states[h]


def _chunk_tri(rows, chunk):
    r = jnp.arange(rows)
    return ((r[:, None] >= r[None, :]) & (r[:, None] // chunk == r[None, :] // chunk)).astype(F32)


def _head_params(a_log, dt_bias):
    return jnp.zeros((SUBLANES, LANES), F32).at[0, :N_HEADS].set(a_log).at[1, :N_HEADS].set(dt_bias)


def _delta(qkv, ab, conv_w, a_log, dt_bias, gn, cbuf, s0, n_b, seq):
    chunk = min(DELTA_CHUNK, seq)
    rows = min(DELTA_ROWS, seq)
    assert seq % rows == 0 and rows % chunk == 0
    nr = seq // rows
    nbb = math.gcd(DELTA_SEQS, n_b)
    width3 = 3 * WIDTH
    hp = _head_params(a_log, dt_bias)
    tri = _chunk_tri(rows, chunk)
    const = lambda b, r: (0, 0)
    tok = lambda width: pl.BlockSpec((nbb, rows, width), lambda b, r: (b, r, 0))
    tail = pl.BlockSpec((nbb, K_CONV - 1, width3), lambda b, r: (b, 0, 0))
    st = pl.BlockSpec((nbb, N_HEADS, HEAD, HEAD), lambda b, r: (b, 0, 0, 0))
    o, cnew, s_new = pl.pallas_call(
        functools.partial(_delta_kernel, chunk),
        grid=(n_b // nbb, nr),
        in_specs=[tok(width3), tok(LANES),
                  pl.BlockSpec((K_CONV, width3), const),
                  pl.BlockSpec((SUBLANES, LANES), const),
                  pl.BlockSpec((1, HEAD), const),
                  tail, st,
                  pl.BlockSpec((rows, rows), const)],
        out_specs=[tok(WIDTH), tail, st],
        out_shape=[jax.ShapeDtypeStruct((n_b, seq, WIDTH), BF16),
                   jax.ShapeDtypeStruct((n_b, K_CONV - 1, width3), F32),
                   jax.ShapeDtypeStruct((n_b, N_HEADS, HEAD, HEAD), F32)],
        scratch_shapes=[pltpu.VMEM((nbb, rows + SUBLANES, width3), F32),
                        pltpu.VMEM((nbb, N_HEADS, HEAD, HEAD), F32)],
        compiler_params=_params(2),
        name="delta_rule",
    )(qkv.reshape(n_b, seq, width3), ab.reshape(n_b, seq, LANES), conv_w, hp, gn.reshape(1, HEAD), cbuf, s0, tri)
    return o.reshape(n_b * seq, WIDTH), cnew, s_new


def _eye(n):
    return lax.broadcasted_iota(jnp.int32, (n, n), 0) == lax.broadcasted_iota(jnp.int32, (n, n), 1)


def _delta_step_kernel(x_ref, ab_ref, cw_ref, hp_ref, gn_ref, cbuf_ref, s0_ref, o_ref, cnew_ref, sout_ref):
    w = cw_ref[...]
    hp = hp_ref[...]
    x = x_ref[...]
    buf = cbuf_ref[...]
    y = w[0:1] * buf[:, 0:1] + w[1:2] * buf[:, 1:2] + w[2:3] * buf[:, 2:3] + w[3:4] * x
    cnew_ref[:, 0:2, :] = buf[:, 1:3]
    cnew_ref[:, 2:3, :] = x
    act = _silu(y)
    ab = ab_ref[...]
    decay_all = jnp.exp(-jnp.exp(hp[0:1]) * _softplus(ab + hp[1:2]))
    beta_all = _sigmoid(ab)

    def per_head(t, col0, width):
        return jnp.stack([t[:, :, col0 + h * width:col0 + (h + 1) * width] for h in range(N_HEADS)], axis=1)

    q, k, v = per_head(act, 0, HEAD), per_head(act, WIDTH, HEAD), per_head(act, 2 * WIDTH, HEAD)
    q = q * lax.rsqrt(jnp.sum(q * q, axis=-1, keepdims=True) + EPS) * HEAD ** -0.5
    k = k * lax.rsqrt(jnp.sum(k * k, axis=-1, keepdims=True) + EPS)
    a = per_head(decay_all, 0, 1)
    beta = per_head(beta_all, N_HEADS, 1)
    eye = _eye(HEAD)
    k_col = jnp.sum(jnp.where(eye, k, 0.0), axis=-1, keepdims=True)
    q_col = jnp.sum(jnp.where(eye, q, 0.0), axis=-1, keepdims=True)
    state = s0_ref[...]
    u = beta * (v - a * jnp.sum(state * k_col, axis=2, keepdims=True))
    state = a * state + k_col * u
    sout_ref[...] = state
    o = jnp.sum(state * q_col, axis=2, keepdims=True)
    o = o * lax.rsqrt(jnp.mean(o * o, axis=-1, keepdims=True) + EPS) * gn_ref[...]
    for h in range(N_HEADS):
        o_ref[:, :, h * HEAD:(h + 1) * HEAD] = o[:, h]


def _step_block(n_s, seq_off):
    return math.gcd(STEP_SEQS, math.gcd(n_s, seq_off))


def _delta_step(qkv, ab, conv_w, a_log, dt_bias, gn, cbuf, s0, seq_off):
    n_s = qkv.shape[0]
    nb = _step_block(n_s, seq_off)
    off = seq_off // nb
    width3 = 3 * WIDTH
    const = lambda b: (0, 0)
    row = lambda width: pl.BlockSpec((nb, 1, width), lambda b: (b, 0, 0))
    tail = pl.BlockSpec((nb, K_CONV - 1, width3), lambda b: (b, 0, 0))
    st = pl.BlockSpec((nb, N_HEADS, HEAD, HEAD), lambda b: (b, 0, 0, 0))
    tail_in = pl.BlockSpec((nb, K_CONV - 1, width3), lambda b: (b + off, 0, 0))
    st_in = pl.BlockSpec((nb, N_HEADS, HEAD, HEAD), lambda b: (b + off, 0, 0, 0))
    o, cnew, s_new = pl.pallas_call(
        _delta_step_kernel,
        grid=(n_s // nb,),
        in_specs=[row(width3), row(LANES), pl.BlockSpec((K_CONV, width3), const),
                  pl.BlockSpec((SUBLANES, LANES), const), pl.BlockSpec((1, HEAD), const), tail_in, st_in],
        out_specs=[row(WIDTH), tail, st],
        out_shape=[jax.ShapeDtypeStruct((n_s, 1, WIDTH), F32),
                   jax.ShapeDtypeStruct((n_s, K_CONV - 1, width3), F32),
                   jax.ShapeDtypeStruct((n_s, N_HEADS, HEAD, HEAD), F32)],
        compiler_params=_params(1),
        name="delta_step",
    )(qkv.reshape(n_s, 1, width3), ab.reshape(n_s, 1, LANES), conv_w, _head_params(a_log, dt_bias),
      gn.reshape(1, HEAD), cbuf, s0)
    return o.reshape(n_s, WIDTH), cnew, s_new


def _log_gamma(h):
    return math.log1p(-(2.0 ** (-5.0 - h)))


def _layer_norm(o, gain):
    oc = o - jnp.mean(o, axis=-1, keepdims=True)
    return oc * lax.rsqrt(jnp.mean(oc * oc, axis=-1, keepdims=True) + EPS) * gain


def _ret_kernel(q_ref, k_ref, v_ref, gn_ref, s0_ref, o_ref, sout_ref, st_ref):
    r = pl.program_id(1)
    rows = q_ref.shape[0]

    @pl.when(r == 0)
    def _init():
        st_ref[...] = s0_ref[...]

    ri = lax.broadcasted_iota(jnp.int32, (rows, 1), 0)
    ci = lax.broadcasted_iota(jnp.int32, (1, rows), 1)
    cnt_r = (ri + 1).astype(F32)
    cnt_c = (ci + 1).astype(F32)
    for h in range(N_HEADS):
        log_gamma = _log_gamma(h)
        cols = slice(h * HEAD, (h + 1) * HEAD)
        gc = cnt_r * log_gamma
        decay = jnp.exp(jnp.where(ri >= ci, gc - cnt_c * log_gamma, -jnp.inf))
        qh = q_ref[:, cols]
        kh = k_ref[:, cols] * HEAD ** -0.5
        vh = v_ref[:, cols]
        state = st_ref[h]
        o = jnp.exp(gc) * _mm(qh, state) + _mm(_mm_nt(qh, kh) * decay, vh)
        gl = rows * log_gamma
        st_ref[h] = math.exp(gl) * state + _mm_tn(kh * jnp.exp(gl - gc), vh)
        o_ref[:, cols] = _layer_norm(o, gn_ref[...]).astype(o_ref.dtype)

    @pl.when(r == pl.num_programs(1) - 1)
    def _finish():
        sout_ref[...] = st_ref[...]


def _ret_step_kernel(q_ref, k_ref, v_ref, gn_ref, s0_ref, o_ref, sout_ref):
    def per_head(ref):
        return jnp.stack([ref[:, :, h * HEAD:(h + 1) * HEAD] for h in range(N_HEADS)], axis=1)

    q, k, v = per_head(q_ref), per_head(k_ref) * HEAD ** -0.5, per_head(v_ref)
    eye = _eye(HEAD)
    k_col = jnp.sum(jnp.where(eye, k, 0.0), axis=-1, keepdims=True)
    q_col = jnp.sum(jnp.where(eye, q, 0.0), axis=-1, keepdims=True)
    s0 = s0_ref[...]
    state = jnp.stack([math.exp(_log_gamma(h)) * s0[:, h] for h in range(N_HEADS)], axis=1) + k_col * v
    sout_ref[...] = state
    o = _layer_norm(jnp.sum(state * q_col, axis=2, keepdims=True), gn_ref[...])
    for h in range(N_HEADS):
        o_ref[:, :, h * HEAD:(h + 1) * HEAD] = o[:, h]


def _retention_step(q, k, v, gn, s0, seq_off):
    n_s = q.shape[0]
    nb = _step_block(n_s, seq_off)
    off = seq_off // nb
    row = pl.BlockSpec((nb, 1, WIDTH), lambda b: (b, 0, 0))
    st = pl.BlockSpec((nb, N_HEADS, HEAD, HEAD), lambda b: (b, 0, 0, 0))
    st_in = pl.BlockSpec((nb, N_HEADS, HEAD, HEAD), lambda b: (b + off, 0, 0, 0))
    r3 = lambda t: t.reshape(n_s, 1, WIDTH)
    o, s_new = pl.pallas_call(
        _ret_step_kernel,
        grid=(n_s // nb,),
        in_specs=[row, row, row, pl.BlockSpec((1, HEAD), lambda b: (0, 0)), st_in],
        out_specs=[row, st],
        out_shape=[jax.ShapeDtypeStruct((n_s, 1, WIDTH), F32),
                   jax.ShapeDtypeStruct((n_s, N_HEADS, HEAD, HEAD), F32)],
        compiler_params=_params(1),
        name="retention_step",
    )(r3(q), r3(k), r3(v), gn.reshape(1, HEAD), s0)
    return o.reshape(n_s, WIDTH), s_new


def _retention(q, k, v, gn, s0, n_b, seq):
    rows = min(RET_ROWS, seq)
    assert seq % rows == 0
    nr = seq // rows
    tok = pl.BlockSpec((rows, WIDTH), lambda b, r: (b * nr + r, 0))
    st = pl.BlockSpec((None, N_HEADS, HEAD, HEAD), lambda b, r: (b, 0, 0, 0))
    return pl.pallas_call(
        _ret_kernel,
        grid=(n_b, nr),
        in_specs=[tok, tok, tok, pl.BlockSpec((1, HEAD), lambda b, r: (0, 0)), st],
        out_specs=[pl.BlockSpec((rows, WIDTH), lambda b, r: (b * nr + r, 0)), st],
        out_shape=[jax.ShapeDtypeStruct((n_b * seq, WIDTH), BF16),
                   jax.ShapeDtypeStruct((n_b, N_HEADS, HEAD, HEAD), F32)],
        scratch_shapes=[pltpu.VMEM((N_HEADS, HEAD, HEAD), F32)],
        compiler_params=_params(2),
        name="retention",
    )(q, k, v, gn.reshape(1, HEAD), s0)


def _s5_prep_kernel(are_ref, aim_ref, ldt_ref, brt_ref, bit_ref, lbr_ref, lbi_ref, bbr_ref, bbi_ref):
    dt = jnp.exp(ldt_ref[...])
    ar, ai = are_ref[...], aim_ref[...]
    mag = jnp.exp(ar * dt)
    ang = ai * dt
    lr, li = mag * jnp.cos(ang), mag * jnp.sin(ang)
    den = ar * ar + ai * ai
    fr = ((lr - 1.0) * ar + li * ai) / den
    fi = (li * ar - (lr - 1.0) * ai) / den
    lbr_ref[...] = lr
    lbi_ref[...] = li
    brt, bit = brt_ref[...], bit_ref[...]
    bbr_ref[...] = fr[:, None, :] * brt - fi[:, None, :] * bit
    bbi_ref[...] = fr[:, None, :] * bit + fi[:, None, :] * brt


def _s5_prep(a_re, a_im, log_dt, b_re, b_im):
    n_g, n_p = a_re.shape
    brt, bit = jnp.swapaxes(b_re, 1, 2), jnp.swapaxes(b_im, 1, 2)
    gp = jax.ShapeDtypeStruct((n_g, n_p), F32)
    gcp = jax.ShapeDtypeStruct(brt.shape, F32)
    return pl.pallas_call(_s5_prep_kernel, out_shape=[gp, gp, gcp, gcp], name="s5_prep")(
        a_re, a_im, log_dt.reshape(n_g, 1), brt, bit)


def _block_diag(blocks):
    n, r, c = blocks.shape
    return jnp.einsum("grc,gh->grhc", blocks, jnp.eye(n, dtype=blocks.dtype)).reshape(n * r, n * c)


def _s5_kernel(n_b, n_t, u_ref, x0r_ref, x0i_ref, lbr_ref, lbi_ref, bb_ref, cc_ref, d_ref,
               y_ref, xr_ref, xi_ref, utb_ref, x_ref, carry_ref):
    i = pl.program_id(0)
    n_p = lbr_ref.shape[1]
    half_in = S5_HALF_GROUPS * GROUP_C
    half_st = S5_HALF_GROUPS * P_C

    @pl.when(i == 0)
    def _init():
        carry_ref[:, :n_p] = x0r_ref[...]
        carry_ref[:, n_p:] = x0i_ref[...]

    n_lc = WIDTH // LANES
    if n_t > 1:
        for b in range(n_b):
            for c in range(n_lc):
                utb_ref[c, pl.ds(b, n_t, stride=n_b), :] = u_ref[:, b * WIDTH + c * LANES:b * WIDTH + (c + 1) * LANES]
        u = jnp.concatenate([utb_ref[c] for c in range(n_lc)], axis=1)
    else:
        u = u_ref[...]
    ub = u.astype(BF16)
    for half in range(2):
        for part in range(2):
            c0 = part * n_p + half * half_st
            x_ref[:, c0:c0 + half_st] = jnp.dot(ub[:, half * half_in:(half + 1) * half_in],
                                                bb_ref[half * 2 + part], preferred_element_type=F32)

    if n_t == 1:
        lr, li = lbr_ref[...], lbi_ref[...]
        x0r, x0i = carry_ref[:, :n_p], carry_ref[:, n_p:]
        xr = lr * x0r - li * x0i + x_ref[:, :n_p]
        xi = lr * x0i + li * x0r + x_ref[:, n_p:]
        x_ref[:, :n_p] = xr
        x_ref[:, n_p:] = xi
        carry_ref[:, :n_p] = xr
        carry_ref[:, n_p:] = xi
    else:
        for lg in range(n_p // S5_LANE_GROUP):
            l0 = lg * S5_LANE_GROUP
            lr = jnp.broadcast_to(lbr_ref[:, l0:l0 + S5_LANE_GROUP], (n_b, S5_LANE_GROUP))
            li = jnp.broadcast_to(lbi_ref[:, l0:l0 + S5_LANE_GROUP], (n_b, S5_LANE_GROUP))

            def body(t, carry, l0=l0, lr=lr, li=li):
                xr, xi = carry
                row = pl.multiple_of(t * n_b, n_b)
                nxr = lr * xr - li * xi + x_ref[pl.ds(row, n_b), l0:l0 + S5_LANE_GROUP]
                nxi = lr * xi + li * xr + x_ref[pl.ds(row, n_b), n_p + l0:n_p + l0 + S5_LANE_GROUP]
                x_ref[pl.ds(row, n_b), l0:l0 + S5_LANE_GROUP] = nxr
                x_ref[pl.ds(row, n_b), n_p + l0:n_p + l0 + S5_LANE_GROUP] = nxi
                return nxr, nxi

            xr, xi = lax.fori_loop(
                0, n_t, body,
                (carry_ref[:, l0:l0 + S5_LANE_GROUP], carry_ref[:, n_p + l0:n_p + l0 + S5_LANE_GROUP]),
                unroll=4)
            carry_ref[:, l0:l0 + S5_LANE_GROUP] = xr
            carry_ref[:, n_p + l0:n_p + l0 + S5_LANE_GROUP] = xi

    ys = []
    for half in range(2):
        xr_b = x_ref[:, half * half_st:(half + 1) * half_st].astype(BF16)
        xi_b = x_ref[:, n_p + half * half_st:n_p + (half + 1) * half_st].astype(BF16)
        ys.append(jnp.dot(xr_b, cc_ref[half * 2], preferred_element_type=F32)
                  + jnp.dot(xi_b, cc_ref[half * 2 + 1], preferred_element_type=F32))
    yg = _gelu_tanh(jnp.concatenate(ys, axis=1) + d_ref[...] * u)
    if n_t > 1:
        for c in range(n_lc):
            utb_ref[c] = yg[:, c * LANES:(c + 1) * LANES]
        for b in range(n_b):
            for c in range(n_lc):
                y_ref[:, b * WIDTH + c * LANES:b * WIDTH + (c + 1) * LANES] = utb_ref[c, pl.ds(b, n_t, stride=n_b), :]
    else:
        y_ref[...] = yg

    @pl.when(i == pl.num_programs(0) - 1)
    def _finish():
        xr_ref[...] = carry_ref[:, :n_p]
        xi_ref[...] = carry_ref[:, n_p:]


def _s5(u, x0_re, x0_im, lb_re, lb_im, bb4, cc4, d_skip, n_b, seq):
    n_p = lb_re.shape[1]
    n_t = min(S5_STEPS, seq)
    n_steps = seq // n_t
    rows = n_t * n_b
    u_block = (n_t, n_b * WIDTH) if seq > 1 else (n_b, WIDTH)
    const2 = lambda i: (0, 0)
    const3 = lambda i: (0, 0, 0)
    return pl.pallas_call(
        functools.partial(_s5_kernel, n_b, n_t),
        grid=(n_steps,),
        in_specs=[pl.BlockSpec(u_block, lambda i: (i, 0)),
                  pl.BlockSpec((n_b, n_p), const2), pl.BlockSpec((n_b, n_p), const2),
                  pl.BlockSpec((1, n_p), const2), pl.BlockSpec((1, n_p), const2),
                  pl.BlockSpec(bb4.shape, const3), pl.BlockSpec(cc4.shape, const3),
                  pl.BlockSpec((1, WIDTH), const2)],
        out_specs=[pl.BlockSpec(u_block, lambda i: (i, 0)),
                   pl.BlockSpec((n_b, n_p), const2), pl.BlockSpec((n_b, n_p), const2)],
        out_shape=[jax.ShapeDtypeStruct(u.shape, F32),
                   jax.ShapeDtypeStruct((n_b, n_p), F32), jax.ShapeDtypeStruct((n_b, n_p), F32)],
        scratch_shapes=[pltpu.VMEM((WIDTH // LANES, rows, LANES), F32), pltpu.VMEM((rows, 2 * n_p), F32),
                        pltpu.VMEM((n_b, 2 * n_p), F32)],
        compiler_params=_params(1),
        name="s5_scan",
    )(u, x0_re, x0_im, lb_re, lb_im, bb4, cc4, d_skip.reshape(1, WIDTH))


def _out_kernel(glu, a1_ref, g1_ref, a2_ref, g2_ref, x_ref, gate_ref, w_ref, wg_ref, o_ref):
    a1 = a1_ref[...].astype(F32)
    if glu:
        a1 = a1 * _sigmoid(jnp.dot(a1.astype(BF16), wg_ref[...], preferred_element_type=F32))
    m1 = a1 * g1_ref[...].astype(F32)
    m2 = a2_ref[...].astype(F32) * g2_ref[...].astype(F32)
    y = (jnp.dot(m1.astype(BF16), w_ref[0:WIDTH, :], preferred_element_type=F32)
         + jnp.dot(m2.astype(BF16), w_ref[WIDTH:2 * WIDTH, :], preferred_element_type=F32))
    o_ref[...] = x_ref[...] + gate_ref[...] * y


def _out_proj(a1, g1, a2, g2, x3, gate, w, wg, glu, a1_by_time):
    n_g, rows, d = x3.shape
    tm = min(OUT_ROW_TILE, rows)
    nt = rows // tm
    r_mod = gate.shape[1]
    tok = pl.BlockSpec((tm, WIDTH), lambda gi, i: (gi * nt + i, 0))
    a1_spec = pl.BlockSpec((tm, WIDTH), lambda gi, i: (i, gi)) if a1_by_time else tok
    const = lambda gi, i: (0, 0)
    return pl.pallas_call(
        functools.partial(_out_kernel, glu),
        grid=(n_g, nt),
        in_specs=[a1_spec, tok, tok, tok,
                  pl.BlockSpec((None, tm, d), lambda gi, i: (gi, i, 0)),
                  pl.BlockSpec((None, r_mod, d), lambda gi, i: (gi, 0, 0)),
                  pl.BlockSpec(w.shape, const), pl.BlockSpec(wg.shape, const)],
        out_specs=pl.BlockSpec((None, tm, d), lambda gi, i: (gi, i, 0)),
        out_shape=jax.ShapeDtypeStruct(x3.shape, F32),
        compiler_params=_params(2),
        name="out_proj",
    )(a1, g1, a2, g2, x3, gate, w, wg)


def _pad_cols(w, n):
    return jnp.pad(w, ((0, 0), (0, n - w.shape[1])))


def _even_layer(li, x3, mods, norm_g, p, attn_fn, delta_fn, kv_heads):
    shift, scale, gate = mods
    n_g, rows, d = x3.shape
    lam_init = 0.8 - 0.6 * math.exp(-0.3 * li)
    n_cols = p["w_in"].shape[1]
    n_pad = -(-n_cols // LANES) * LANES
    w_in = _pad_cols(p["w_in"], n_pad).astype(BF16)
    kv = "heads" if kv_heads else "rows"
    segs = ((0, WIDTH, "qnorm", BF16, "rows"), (WIDTH, WIDTH, "knorm", F32, kv),
            (2 * WIDTH, WIDTH, "raw", F32, kv), (3 * WIDTH, WIDTH, "silu", BF16, "rows"),
            (4 * WIDTH, 3 * WIDTH, "raw", F32, "rows"), (7 * WIDTH, WIDTH, "silu", BF16, "rows"),
            (8 * WIDTH, LANES, "raw", F32, "rows"))
    bd = _block_diag(jnp.full((WIDTH // DQK, DQK, DQK), 1.0 / DQK, F32)).astype(BF16)
    qg = jnp.tile(p["qn_g"], WIDTH // DQK).reshape(1, WIDTH)
    kg = jnp.tile(p["kn_g"], WIDTH // DQK).reshape(1, WIDTH)
    outs = _in_proj(x3, shift, scale, norm_g, w_in, segs, bd, qg, kg)
    if kv_heads:
        q, k, k_out, v, v_out, za, qkv_b, zb, ab = outs
    else:
        q, k, v, za, qkv_b, zb, ab = outs
        k_out, v_out = k, v
    lamp = jnp.zeros((SUBLANES, LANES), F32)
    for i, name in enumerate(("lam_q1", "lam_k1", "lam_q2", "lam_k2")):
        lamp = lamp.at[i, :DQK].set(p[name])
    sg = p["subln_g"].reshape(1, HEAD)
    oa = attn_fn(q, k, v, lamp=lamp, sg=sg, lam_init=lam_init)
    ob, conv_new, s_new = delta_fn(qkv_b, ab, p["conv_w"], p["a_log"], p["dt_bias"], p["gn_b"])
    w_out = p["w_out"].astype(BF16)
    x_new = _out_proj(oa, za, ob, zb, x3, gate, w_out, jnp.zeros((SUBLANES, LANES), BF16), False, False)
    return x_new, (k_out, v_out, conv_new, s_new)


def _odd_layer(x3, mods, norm_g, p, s5_mats, x0_re, x0_im, r0, by_time, seq_off=0):
    shift, scale, gate = mods
    n_g, rows, d = x3.shape
    w_in = p["w_in"].astype(BF16)
    segs = ((0, WIDTH, "raw", F32, "time" if by_time else "rows"), (WIDTH, WIDTH, "silu", BF16, "rows"),
            (2 * WIDTH, WIDTH, "raw", F32, "rows"), (3 * WIDTH, WIDTH, "raw", F32, "rows"),
            (4 * WIDTH, WIDTH, "raw", F32, "rows"), (5 * WIDTH, WIDTH, "silu", BF16, "rows"))
    dummy = jnp.zeros((SUBLANES, LANES), BF16)
    dummy_g = jnp.zeros((1, LANES), F32)
    u, zc, qd, kd, vd, zd = _in_proj(x3, shift, scale, norm_g, w_in, segs, dummy, dummy_g, dummy_g)
    lb_re, lb_im, bb4, cc4 = s5_mats
    n_b = n_g if by_time else rows
    seq = rows if by_time else 1
    yg, xr, xi = _s5(u, x0_re, x0_im, lb_re, lb_im, bb4, cc4, p["s5_d"], n_b, seq)
    if by_time:
        od, r_new = _retention(qd, kd, vd, p["gn_d"], r0, n_g, rows)
    else:
        od, r_new = _retention_step(qd, kd, vd, p["gn_d"], r0, seq_off)
    x_new = _out_proj(yg, zc, od, zd, x3, gate, p["w_out"].astype(BF16), p["w_glu"].astype(BF16), True, by_time)
    return x_new, (xr, xi, r_new)


def kernel(x_prompt, x_sample, c_prompt, c_sample, page_table, cache_k, cache_v, state_b_conv, state_b_ssm,
           state_c_re, state_c_im, state_d_ret, norm_g, w_ada, b_ada, w_in_e, w_out_e, qn_g, kn_g,
           lam_q1, lam_k1, lam_q2, lam_k2, subln_g, conv_w, a_log, dt_bias, gn_b, w_in_o, w_out_o,
           s5_a_re, s5_a_im, s5_b_re, s5_b_im, s5_c_re, s5_c_im, s5_d, s5_log_dt, w_glu, gn_d):
    n_bp, seq, d = x_prompt.shape
    n_bs = x_sample.shape[0]
    depth = norm_g.shape[0]
    n_pages, page = page_table.shape[1], cache_k.shape[2]
    n_g, n_p = s5_a_re.shape[1], s5_a_re.shape[2]
    conv_all = state_b_conv.reshape(-1, K_CONV - 1, 3 * WIDTH)
    ssm_all = state_b_ssm.reshape(-1, N_HEADS, HEAD, HEAD)
    ret_all = state_d_ret.reshape(-1, N_HEADS, HEAD, HEAD)

    mod = _modulation(jnp.concatenate([c_prompt, c_sample], axis=0), w_ada, b_ada)
    xp = x_prompt
    xs = x_sample.reshape(1, n_bs, d)
    outs = {name: [] for name in ("k_p", "v_p", "k_s", "v_s", "cv_p", "cv_s", "dl_p", "dl_s",
                                  "s5r_p", "s5i_p", "s5r_s", "s5i_s", "rt_p", "rt_s")}
    for li in range(depth):
        mods_p = tuple(mod[li, :n_bp, j * d:(j + 1) * d].reshape(n_bp, 1, d) for j in range(3))
        mods_s = tuple(mod[li, n_bp:, j * d:(j + 1) * d].reshape(1, n_bs, d) for j in range(3))
        if li % 2 == 0:
            e = li // 2
            p = dict(w_in=w_in_e[e], w_out=w_out_e[e], qn_g=qn_g[e], kn_g=kn_g[e], lam_q1=lam_q1[e],
                     lam_k1=lam_k1[e], lam_q2=lam_q2[e], lam_k2=lam_k2[e], subln_g=subln_g[e],
                     conv_w=conv_w[e], a_log=a_log[e], dt_bias=dt_bias[e], gn_b=gn_b[e])
            attn_p = functools.partial(_attn_prompt, n_b=n_bp, seq=seq)
            delta_p = functools.partial(
                _delta, cbuf=jnp.zeros((n_bp, K_CONV - 1, 3 * WIDTH), F32),
                s0=jnp.zeros((n_bp, N_HEADS, HEAD, HEAD), F32), n_b=n_bp, seq=seq)
            xp, (k_p, v_p, c_p, s_p) = _even_layer(li, xp, mods_p, norm_g[li], p, attn_p, delta_p, True)
            attn_s = functools.partial(_attn_decode, cache_k=cache_k, cache_v=cache_v, page_table=page_table,
                                       layer=e)
            delta_s = functools.partial(_delta_step, cbuf=conv_all, s0=ssm_all, seq_off=e * n_bs)
            xs, (k_s, v_s, c_s, s_s) = _even_layer(li, xs, mods_s, norm_g[li], p, attn_s, delta_s, False)
            outs["k_p"].append(k_p.reshape(n_bp, seq, N_HEADS, HEAD))
            outs["v_p"].append(v_p.reshape(n_bp, seq, N_HEADS, HEAD))
            outs["k_s"].append(k_s.reshape(n_bs, 1, N_HEADS, HEAD))
            outs["v_s"].append(v_s.reshape(n_bs, 1, N_HEADS, HEAD))
            outs["cv_p"].append(c_p)
            outs["cv_s"].append(c_s)
            outs["dl_p"].append(s_p)
            outs["dl_s"].append(s_s)
        else:
            o = li // 2
            p = dict(w_in=w_in_o[o], w_out=w_out_o[o], s5_d=s5_d[o], w_glu=w_glu[o], gn_d=gn_d[o])
            lb_re, lb_im, bbr, bbi = _s5_prep(s5_a_re[o], s5_a_im[o], s5_log_dt[o], s5_b_re[o], s5_b_im[o])
            hg = S5_HALF_GROUPS
            bb4 = jnp.stack([_block_diag(t[h * hg:(h + 1) * hg]) for h in range(2) for t in (bbr, bbi)]).astype(BF16)
            cre = jnp.swapaxes(s5_c_re[o], 1, 2)
            cim = -jnp.swapaxes(s5_c_im[o], 1, 2)
            cc4 = jnp.stack([_block_diag(t[h * hg:(h + 1) * hg]) for h in range(2) for t in (cre, cim)]).astype(BF16)
            s5_mats = (lb_re.reshape(1, n_g * n_p), lb_im.reshape(1, n_g * n_p), bb4, cc4)
            zeros_c = jnp.zeros((n_bp, n_g * n_p), F32)
            xp, (r_p, i_p, t_p) = _odd_layer(xp, mods_p, norm_g[li], p, s5_mats, zeros_c, zeros_c,
                                             jnp.zeros((n_bp, N_HEADS, HEAD, HEAD), F32), True)
            xs, (r_s, i_s, t_s) = _odd_layer(xs, mods_s, norm_g[li], p, s5_mats,
                                             state_c_re[o].reshape(n_bs, n_g * n_p),
                                             state_c_im[o].reshape(n_bs, n_g * n_p), ret_all, False, o * n_bs)
            outs["s5r_p"].append(r_p.reshape(n_bp, n_g, n_p))
            outs["s5i_p"].append(i_p.reshape(n_bp, n_g, n_p))
            outs["s5r_s"].append(r_s.reshape(n_bs, n_g, n_p))
            outs["s5i_s"].append(i_s.reshape(n_bs, n_g, n_p))
            outs["rt_p"].append(t_p)
            outs["rt_s"].append(t_s)
    st = lambda name: jnp.stack(outs[name])
    return (xp, xs.reshape(n_bs, 1, d), st("k_p"), st("v_p"), st("k_s"), st("v_s"), st("cv_p"), st("cv_s"),
            st("dl_p"), st("dl_s"), st("s5r_p"), st("s5i_p"), st("s5r_s"), st("s5i_s"), st("rt_p"), st("rt_s"))
```
